```python
import math
import jax, jax.numpy as jnp
from jax import lax
import numpy as np

D_MODEL = 1024
BATCH = 16
SEQ = 4096
DEPTH = 1

HEAD_DIM = 64
N_DIFF_HEADS = D_MODEL // 2 // (2 * HEAD_DIM)
DIFF_WIDTH = N_DIFF_HEADS * 2 * HEAD_DIM
N_DIL_HEADS = D_MODEL // 2 // HEAD_DIM
DIL_WIDTH = N_DIL_HEADS * HEAD_DIM
MIX_WIDTH = DIFF_WIDTH + DIL_WIDTH
DIL_PATTERNS = ((128, 1), (512, 4), (2048, 16))
N_BUCKETS = 32
REL_MAX_DIST = 1024
Q_BLOCK = 128
N_EXPERTS = 16
EC_K = 2
D_FF = 256 * ((8 * D_MODEL // 3 + 255) // 256)
NORM_EPS = 1e-6
SUBLN_EPS = 1e-5
NEG = -1e30

kernel_name = 'hybrid_diffattn_dilated_ec_moe_encoder'


def rms_norm(x, g, eps=NORM_EPS):
    xf = x.astype(jnp.float32)
    y = xf * lax.rsqrt(jnp.mean(xf * xf, axis=-1, keepdims=True) + eps)
    return (y * g.astype(jnp.float32)).astype(x.dtype)


def t5_bucket(rel):
    half = N_BUCKETS // 2
    max_exact = half // 2
    n = jnp.abs(rel)
    nf = jnp.maximum(n, 1).astype(jnp.float32)
    large = max_exact + (jnp.log(nf / max_exact) / math.log(REL_MAX_DIST / max_exact)
                         * (half - max_exact)).astype(jnp.int32)
    large = jnp.minimum(large, half - 1)
    return jnp.where(rel > 0, half, 0) + jnp.where(n < max_exact, n, large)


def diff_attention(qa, ka, va, lam, lambda_init, subln_g, bias_a):
    B, S = qa.shape[:2]
    H = N_DIFF_HEADS
    q = qa.reshape(B, S, H, 2, HEAD_DIM) * (HEAD_DIM ** -0.5)
    k = ka.reshape(B, S, H, 2, HEAD_DIM)
    k1 = k[:, :, :, 0].transpose(0, 2, 1, 3)
    k2 = k[:, :, :, 1].transpose(0, 2, 1, 3)
    v = va.reshape(B, S, H, 2 * HEAD_DIM).transpose(0, 2, 1, 3)
    n_blk = S // Q_BLOCK

    def to_qblocks(t):
        return t.reshape(B, n_blk, Q_BLOCK, H, HEAD_DIM).transpose(1, 0, 3, 2, 4)

    q1b = to_qblocks(q[:, :, :, 0])
    q2b = to_qblocks(q[:, :, :, 1])
    kpos = jnp.arange(S, dtype=jnp.int32)

    def block(args):
        j, q1, q2 = args
        qpos = j * Q_BLOCK + jnp.arange(Q_BLOCK, dtype=jnp.int32)
        bias = jnp.take(bias_a, t5_bucket(kpos[None, :] - qpos[:, None]), axis=0)
        bias = bias.transpose(2, 0, 1).astype(jnp.float32)[None]
        s1 = jnp.einsum('bhqd,bhkd->bhqk', q1, k1).astype(jnp.float32) + bias
        s2 = jnp.einsum('bhqd,bhkd->bhqk', q2, k2).astype(jnp.float32) + bias
        p = jax.nn.softmax(s1, axis=-1) - lam * jax.nn.softmax(s2, axis=-1)
        return jnp.einsum('bhqk,bhkd->bhqd', p.astype(v.dtype), v)

    o = lax.map(block, (jnp.arange(n_blk, dtype=jnp.int32), q1b, q2b))
    o = o.transpose(1, 0, 3, 2, 4).reshape(B, S, H, 2 * HEAD_DIM)
    o = rms_norm(o, subln_g, SUBLN_EPS) * (1.0 - lambda_init)
    return o.reshape(B, S, DIFF_WIDTH)


def dilated_pattern(q, k, v, window, dil, bias_b):
    B, S, H, Dh = q.shape
    half = window // (2 * dil)
    blk = half
    unit = dil * blk
    sp = -(-S // unit) * unit
    m_len = sp // dil
    nb = m_len // blk

    def to_blocks(t):
        t = jnp.pad(t, ((0, 0), (0, sp - S), (0, 0), (0, 0)))
        t = t.reshape(B, m_len, dil, H, Dh).transpose(0, 2, 3, 1, 4)
        return t.reshape(B, dil, H, nb, blk, Dh)

    def with_neighbours(t):
        tp = jnp.pad(t, ((0, 0), (0, 0), (0, 0), (1, 1), (0, 0), (0, 0)))
        return jnp.concatenate([tp[:, :, :, :-2], tp[:, :, :, 1:-1], tp[:, :, :, 2:]], axis=4)

    qb = to_blocks(q * (Dh ** -0.5))
    kn = with_neighbours(to_blocks(k))
    vn = with_neighbours(to_blocks(v))
    off = (jnp.arange(3 * blk, dtype=jnp.int32)[None, :] - blk
           - jnp.arange(blk, dtype=jnp.int32)[:, None])
    in_band = jnp.abs(off) <= half
    m_key = (jnp.arange(nb, dtype=jnp.int32)[:, None] * blk
             + jnp.arange(3 * blk, dtype=jnp.int32)[None, :] - blk)
    pos_key = m_key[None] * dil + jnp.arange(dil, dtype=jnp.int32)[:, None, None]
    key_ok = (m_key >= 0)[None] & (pos_key < S)
    mask = key_ok[:, :, None, :] & in_band[None, None]
    bias = jnp.take(bias_b, t5_bucket(off * dil), axis=0).transpose(2, 0, 1).astype(jnp.float32)
    s = jnp.einsum('brhnqd,brhnkd->brhnqk', qb, kn).astype(jnp.float32) + bias[None, None, :, None]
    s = jnp.where(mask[None, :, None], s, NEG)
    lse = jax.nn.logsumexp(s, axis=-1)
    p = jnp.exp(s - lse[..., None])
    o = jnp.einsum('brhnqk,brhnkd->brhnqd', p.astype(v.dtype), vn)
    o = o.reshape(B, dil, H, m_len, Dh).transpose(0, 3, 1, 2, 4).reshape(B, sp, H, Dh)[:, :S]
    lse = lse.reshape(B, dil, H, m_len).transpose(0, 3, 1, 2).reshape(B, sp, H)[:, :S]
    return o, lse


def dilated_attention(qd, kd, vd, bias_b):
    B, S = qd.shape[:2]
    q = qd.reshape(B, S, N_DIL_HEADS, HEAD_DIM)
    k = kd.reshape(B, S, N_DIL_HEADS, HEAD_DIM)
    v = vd.reshape(B, S, N_DIL_HEADS, HEAD_DIM)
    outs, lses = [], []
    for window, dil in DIL_PATTERNS:
        o, lse = dilated_pattern(q, k, v, window, dil, bias_b)
        outs.append(o)
        lses.append(lse)
    w = jax.nn.softmax(jnp.stack(lses, axis=0), axis=0)
    o = jnp.sum(w[..., None] * jnp.stack(outs, axis=0).astype(jnp.float32), axis=0)
    return o.astype(qd.dtype).reshape(B, S, DIL_WIDTH)


def expert_choice_ffn(h, w_router, w_gate, w_up, w_down):
    B, S, D = h.shape
    cap = EC_K * S // N_EXPERTS
    aff = jax.nn.softmax((h @ w_router).astype(jnp.float32), axis=-1)
    g, idx = lax.top_k(aff.transpose(0, 2, 1), cap)
    g = g.transpose(1, 0, 2)
    idx = idx.transpose(1, 0, 2)
    b_rows = jnp.arange(B, dtype=jnp.int32)[:, None]

    def run_expert(args):
        wg, wu, wd, ie, ge = args
        xe = h[b_rows, ie]
        he = jax.nn.silu(xe @ wg) * (xe @ wu)
        return (he @ wd) * ge[..., None].astype(h.dtype)

    ye = lax.map(run_expert, (w_gate, w_up, w_down, idx, g))
    b_idx = jnp.arange(B, dtype=jnp.int32)[None, :, None]
    return jnp.zeros_like(h).at[b_idx, idx].add(ye)


def setup_inputs(seed: int = 0) -> dict:
    key = jax.random.key(seed)
    ks = jax.random.split(key, 16)
    f = jnp.float32
    n_in = 3 * DIFF_WIDTH + 3 * DIL_WIDTH
    nrm = lambda k, shape, s: jax.random.normal(k, shape, f) * s
    return {
        'x': nrm(ks[0], (BATCH, SEQ, D_MODEL), 1.0),
        'norm1_g': 1.0 + nrm(ks[1], (DEPTH, D_MODEL), 0.02),
        'w_in': nrm(ks[2], (DEPTH, D_MODEL, n_in), D_MODEL ** -0.5),
        'lam_q1': nrm(ks[3], (DEPTH, HEAD_DIM), 0.1),
        'lam_k1': nrm(ks[4], (DEPTH, HEAD_DIM), 0.1),
        'lam_q2': nrm(ks[5], (DEPTH, HEAD_DIM), 0.1),
        'lam_k2': nrm(ks[6], (DEPTH, HEAD_DIM), 0.1),
        'subln_g': 1.0 + nrm(ks[7], (DEPTH, 2 * HEAD_DIM), 0.02),
        'w_out': nrm(ks[8], (DEPTH, MIX_WIDTH, D_MODEL), MIX_WIDTH ** -0.5),
        'rel_bias': nrm(ks[9], (N_BUCKETS, N_DIFF_HEADS + N_DIL_HEADS), 0.2),
        'norm2_g': 1.0 + nrm(ks[10], (DEPTH, D_MODEL), 0.02),
        'w_router': nrm(ks[11], (DEPTH, D_MODEL, N_EXPERTS), D_MODEL ** -0.5),
        'w_gate': nrm(ks[12], (DEPTH, N_EXPERTS, D_MODEL, D_FF), D_MODEL ** -0.5),
        'w_up': nrm(ks[13], (DEPTH, N_EXPERTS, D_MODEL, D_FF), D_MODEL ** -0.5),
        'w_down': nrm(ks[14], (DEPTH, N_EXPERTS, D_FF, D_MODEL), D_FF ** -0.5),
        'norm_f_g': 1.0 + nrm(ks[15], (D_MODEL,), 0.02),
    }


def reference(x, norm1_g, w_in, lam_q1, lam_k1, lam_q2, lam_k2, subln_g, w_out, rel_bias,
              norm2_g, w_router, w_gate, w_up, w_down, norm_f_g):
    splits = [DIFF_WIDTH, 2 * DIFF_WIDTH, 3 * DIFF_WIDTH,
              3 * DIFF_WIDTH + DIL_WIDTH, 3 * DIFF_WIDTH + 2 * DIL_WIDTH]
    bias_a = rel_bias[:, :N_DIFF_HEADS]
    bias_b = rel_bias[:, N_DIFF_HEADS:]
    for l in range(DEPTH):
        lambda_init = 0.8 - 0.6 * math.exp(-0.3 * l)
        h = rms_norm(x, norm1_g[l])
        proj = h @ w_in[l]
        qa, ka, va, qd, kd, vd = jnp.split(proj, splits, axis=-1)
        lam = (jnp.exp(jnp.sum(lam_q1[l].astype(jnp.float32) * lam_k1[l].astype(jnp.float32)))
               - jnp.exp(jnp.sum(lam_q2[l].astype(jnp.float32) * lam_k2[l].astype(jnp.float32)))
               + lambda_init)
        oa = diff_attention(qa, ka, va, lam, lambda_init, subln_g[l], bias_a)
        od = dilated_attention(qd, kd, vd, bias_b)
        x = x + jnp.concatenate([oa, od], axis=-1) @ w_out[l]
        x = x + expert_choice_ffn(rms_norm(x, norm2_g[l]), w_router[l], w_gate[l], w_up[l], w_down[l])
    return rms_norm(x, norm_f_g)
```

```python
import functools
import math

import jax
import jax.numpy as jnp
from jax import lax
from jax.experimental import pallas as pl
from jax.experimental.pallas import tpu as pltpu

F32 = jnp.float32
BF16 = jnp.bfloat16
I32 = jnp.int32

D_MODEL = 1024
HEAD_DIM = 64
N_DIFF_HEADS = 4
N_DIL_HEADS = 8
N_DIL_PAIRS = N_DIL_HEADS // 2
DIFF_WIDTH = 512
DIL_WIDTH = 512
N_IN = 3 * DIFF_WIDTH + 3 * DIL_WIDTH
DIL_PATTERNS = ((128, 1), (512, 4), (2048, 16))
N_BUCKETS = 32
REL_MAX_DIST = 1024
N_EXPERTS = 16
EC_K = 2
D_FF = 2816
NORM_EPS = 1e-6
SUBLN_EPS = 1e-5
NEG = -1e30
LAMBDA_INIT = 0.8 - 0.6 * math.exp(0.0)

LANES = 128
SUBLANES = 8
ROW_CHUNKS = D_MODEL // LANES

DA_TQ = 256
DA_TK = 512
DA_DMIN = -1280
DA_DMAX = 1024
DA_WTOT = DA_DMAX - DA_DMIN + DA_TK

DL_BQ = 128
DL_HALF = 64
DL_WIN = DL_BQ + 2 * DL_HALF

VMEM_LIMIT = 56 * 1024 * 1024


def _cparams(sem):
    return pltpu.CompilerParams(dimension_semantics=sem, vmem_limit_bytes=VMEM_LIMIT)


def _t5_bucket(rel):
    half = N_BUCKETS // 2
    max_exact = half // 2
    n = jnp.abs(rel)
    nf = jnp.maximum(n, 1).astype(F32)
    large = max_exact + (jnp.log(nf / max_exact) / math.log(REL_MAX_DIST / max_exact)
                         * (half - max_exact)).astype(I32)
    large = jnp.minimum(large, half - 1)
    return jnp.where(rel > 0, half, 0) + jnp.where(n < max_exact, n, large)


def _bias_lookup(rel, tab_ref, col):
    bucket = _t5_bucket(rel)
    acc = jnp.zeros(rel.shape, F32)
    for j in range(N_BUCKETS):
        acc = jnp.where(bucket == j, tab_ref[j, col], acc)
    return acc


def _diff_bias_body(tab_ref, e_ref):
    h = pl.program_id(0)

    def chunk(cc, _):
        c0 = pl.multiple_of(cc * LANES, LANES)
        i = lax.broadcasted_iota(I32, (DA_TQ, LANES), 0)
        c = lax.broadcasted_iota(I32, (DA_TQ, LANES), 1) + c0
        e_ref[0, :, pl.ds(c0, LANES)] = _bias_lookup(c - i + DA_DMIN, tab_ref, h)
        return 0

    lax.fori_loop(0, DA_WTOT // LANES, chunk, 0)


def _diff_bias(rel_bias):
    return pl.pallas_call(
        _diff_bias_body,
        grid=(N_DIFF_HEADS,),
        in_specs=[pl.BlockSpec(memory_space=pltpu.SMEM)],
        out_specs=pl.BlockSpec((1, DA_TQ, DA_WTOT), lambda h: (h, 0, 0)),
        out_shape=jax.ShapeDtypeStruct((N_DIFF_HEADS, DA_TQ, DA_WTOT), F32),
        compiler_params=_cparams(("arbitrary",)),
        name="diff_bias",
    )(rel_bias)


def _dil_bias_body(tab_ref, m_ref):
    p = pl.program_id(0)
    h = pl.program_id(1)
    dil = jnp.where(p == 0, DIL_PATTERNS[0][1], jnp.where(p == 1, DIL_PATTERNS[1][1], DIL_PATTERNS[2][1]))
    i = lax.broadcasted_iota(I32, (DL_BQ, DL_WIN), 0)
    c = lax.broadcasted_iota(I32, (DL_BQ, DL_WIN), 1)
    off = c - DL_HALF - i
    bias = _bias_lookup(off * dil, tab_ref, N_DIFF_HEADS + h)
    band = jnp.abs(off) <= DL_HALF
    m_ref[0, 0, 0] = jnp.where(band & (c >= DL_HALF), bias, NEG)
    m_ref[0, 0, 1] = jnp.where(band, bias, NEG)
    m_ref[0, 0, 2] = jnp.where(band & (c < DL_HALF + DL_BQ), bias, NEG)


def _dil_bias(rel_bias):
    n_pat = len(DIL_PATTERNS)
    return pl.pallas_call(
        _dil_bias_body,
        grid=(n_pat, N_DIL_HEADS),
        in_specs=[pl.BlockSpec(memory_space=pltpu.SMEM)],
        out_specs=pl.BlockSpec((1, 1, 3, DL_BQ, DL_WIN), lambda p, h: (p, h, 0, 0, 0)),
        out_shape=jax.ShapeDtypeStruct((n_pat, N_DIL_HEADS, 3, DL_BQ, DL_WIN), F32),
        compiler_params=_cparams(("arbitrary", "arbitrary")),
        name="dil_bias",
    )(rel_bias)


IP_TM = 512
IP_TN = 256
N_COLBLK = N_IN // LANES
N_DIL_COLBLK = 3 * DIL_WIDTH // LANES


def _in_proj_body(x_ref, g_ref, w_ref, p_ref, p4_ref, p16_ref, scr_ref):
    x = x_ref[0]
    ms = jnp.mean(x * x, axis=-1, keepdims=True)
    y = (x * lax.rsqrt(ms + NORM_EPS) * g_ref[...]).astype(BF16)
    q_scale = HEAD_DIM ** -0.5
    for n in range(N_IN // IP_TN):
        c0 = n * IP_TN
        r = jnp.dot(y, w_ref[:, c0:c0 + IP_TN], preferred_element_type=F32)
        is_q = (c0 < DIFF_WIDTH) or (3 * DIFF_WIDTH <= c0 < 3 * DIFF_WIDTH + DIL_WIDTH)
        if is_q:
            r = r * q_scale
        for k in range(IP_TN // LANES):
            p_ref[0, 2 * n + k] = r[:, k * LANES:(k + 1) * LANES].astype(BF16)
        if c0 >= 3 * DIFF_WIDTH:
            for k in range(IP_TN // LANES):
                scr_ref[(c0 - 3 * DIFF_WIDTH) // LANES + k] = r[:, k * LANES:(k + 1) * LANES]
    for c in range(N_DIL_COLBLK):
        for r_ in range(4):
            p4_ref[0, c, r_] = scr_ref[c, pl.ds(r_, IP_TM // 4, stride=4), :].astype(BF16)
        for r_ in range(16):
            p16_ref[0, c, r_] = scr_ref[c, pl.ds(r_, IP_TM // 16, stride=16), :].astype(BF16)


def _in_proj(x, g, w_bf16):
    B, S, _ = x.shape
    nt = S // IP_TM
    return pl.pallas_call(
        _in_proj_body,
        grid=(B, nt),
        in_specs=[
            pl.BlockSpec((1, IP_TM, D_MODEL), lambda b, t: (b, t, 0)),
            pl.BlockSpec((1, D_MODEL), lambda b, t: (0, 0)),
            pl.BlockSpec((D_MODEL, N_IN), lambda b, t: (0, 0)),
        ],
        out_specs=[
            pl.BlockSpec((1, N_COLBLK, IP_TM, LANES), lambda b, t: (b, 0, t, 0)),
            pl.BlockSpec((1, N_DIL_COLBLK, 4, IP_TM // 4, LANES), lambda b, t: (b, 0, 0, t, 0)),
            pl.BlockSpec((1, N_DIL_COLBLK, 16, IP_TM // 16, LANES), lambda b, t: (b, 0, 0, t, 0)),
        ],
        out_shape=[
            jax.ShapeDtypeStruct((B, N_COLBLK, S, LANES), BF16),
            jax.ShapeDtypeStruct((B, N_DIL_COLBLK, 4, S // 4, LANES), BF16),
            jax.ShapeDtypeStruct((B, N_DIL_COLBLK, 16, S // 16, LANES), BF16),
        ],
        scratch_shapes=[pltpu.VMEM((N_DIL_COLBLK, IP_TM, LANES), F32)],
        compiler_params=_cparams(("arbitrary", "arbitrary")),
        name="in_proj",
    )(x, g, w_bf16)


_NT_DIMS = (((1,), (1,)), ((), ()))


def _online_update(s, v, m, l, acc):
    m_new = jnp.maximum(m, jnp.max(s, axis=-1, keepdims=True))
    alpha = jnp.exp(m - m_new)
    e = jnp.exp(s - m_new)
    l_new = alpha * l + jnp.sum(e, axis=-1, keepdims=True)
    acc_new = alpha * acc + jnp.dot(e.astype(BF16), v, preferred_element_type=F32)
    return m_new, l_new, acc_new


def _diff_attn_body(q_ref, k_ref, v_ref, e_ref, lq1_ref, lk1_ref, lq2_ref, lk2_ref, sg_ref, o_ref):
    S = q_ref.shape[2]
    lam = (jnp.exp(jnp.sum(lq1_ref[...] * lk1_ref[...], axis=-1, keepdims=True))
           - jnp.exp(jnp.sum(lq2_ref[...] * lk2_ref[...], axis=-1, keepdims=True)) + LAMBDA_INIT)
    lane = lax.broadcasted_iota(I32, (DA_TQ, LANES), 1)
    lo = lane < HEAD_DIM

    def q_loop(qi, _):
        q0 = pl.multiple_of(qi * DA_TQ, DA_TQ)
        q = q_ref[0, 0, pl.ds(q0, DA_TQ), :]
        zero = jnp.zeros_like(q)
        q1 = jnp.where(lo, q, zero)
        q2 = jnp.where(lo, zero, q)

        def k_loop(ki, carry):
            m1, l1, a1, m2, l2, a2 = carry
            k0 = pl.multiple_of(ki * DA_TK, DA_TK)
            k = k_ref[0, 0, pl.ds(k0, DA_TK), :]
            v = v_ref[0, 0, pl.ds(k0, DA_TK), :]
            off = pl.multiple_of(jnp.clip(k0 - q0, DA_DMIN, DA_DMAX) - DA_DMIN, LANES)
            bias = e_ref[0, :, pl.ds(off, DA_TK)]
            s1 = lax.dot_general(q1, k, _NT_DIMS, preferred_element_type=F32) + bias
            s2 = lax.dot_general(q2, k, _NT_DIMS, preferred_element_type=F32) + bias
            m1, l1, a1 = _online_update(s1, v, m1, l1, a1)
            m2, l2, a2 = _online_update(s2, v, m2, l2, a2)
            return m1, l1, a1, m2, l2, a2

        neg_inf = jnp.full((DA_TQ, 1), -jnp.inf, F32)
        zc = jnp.zeros((DA_TQ, 1), F32)
        za = jnp.zeros((DA_TQ, LANES), F32)
        m1, l1, a1, m2, l2, a2 = lax.fori_loop(0, S // DA_TK, k_loop, (neg_inf, zc, za, neg_inf, zc, za))
        o = a1 / l1 - lam * (a2 / l2)
        ms = jnp.mean(o * o, axis=-1, keepdims=True)
        o = o * lax.rsqrt(ms + SUBLN_EPS) * sg_ref[...] * (1.0 - LAMBDA_INIT)
        o_ref[0, 0, pl.ds(q0, DA_TQ), :] = o.astype(BF16)
        return 0

    lax.fori_loop(0, S // DA_TQ, q_loop, 0)


def _diff_attn(p, e_bias, lq1, lk1, lq2, lk2, subln_g):
    B, _, S, _ = p.shape
    H = N_DIFF_HEADS
    vec = lambda n: pl.BlockSpec((1, n), lambda b, h: (0, 0))
    return pl.pallas_call(
        _diff_attn_body,
        grid=(B, H),
        in_specs=[
            pl.BlockSpec((1, 1, S, LANES), lambda b, h: (b, h, 0, 0)),
            pl.BlockSpec((1, 1, S, LANES), lambda b, h: (b, H + h, 0, 0)),
            pl.BlockSpec((1, 1, S, LANES), lambda b, h: (b, 2 * H + h, 0, 0)),
            pl.BlockSpec((1, DA_TQ, DA_WTOT), lambda b, h: (h, 0, 0)),
            vec(HEAD_DIM), vec(HEAD_DIM), vec(HEAD_DIM), vec(HEAD_DIM), vec(2 * HEAD_DIM),
        ],
        out_specs=pl.BlockSpec((1, 1, S, LANES), lambda b, h: (b, h, 0, 0)),
        out_shape=jax.ShapeDtypeStruct((B, H, S, LANES), BF16),
        compiler_params=_cparams(("arbitrary", "arbitrary")),
        name="diff_attn",
    )(p, p, p, e_bias, lq1, lk1, lq2, lk2, subln_g)


def _dil_attn_body(q1_ref, k1_ref, v1_ref, q4_ref, k4_ref, v4_ref, q16_ref, k16_ref, v16_ref, m_ref,
                   o_ref, kpad_ref, vpad_ref, oacc_ref, lse_ref):
    S = q1_ref.shape[2]
    lane = lax.broadcasted_iota(I32, (DL_BQ, LANES), 1)
    lo = lane < HEAD_DIM
    zpad = jnp.zeros((DL_HALF, LANES), BF16)

    def run_pattern(p, dil, get_q, get_k, get_v):
        L = S // dil
        nblk = L // DL_BQ

        def residue(r, _):
            kpad_ref[0:DL_HALF, :] = zpad
            vpad_ref[0:DL_HALF, :] = zpad
            kpad_ref[DL_HALF:DL_HALF + L, :] = get_k(r)[...]
            vpad_ref[DL_HALF:DL_HALF + L, :] = get_v(r)[...]
            kpad_ref[DL_HALF + L:2 * DL_HALF + L, :] = zpad
            vpad_ref[DL_HALF + L:2 * DL_HALF + L, :] = zpad
            q_r = get_q(r)

            def block(blk, _):
                q0 = pl.multiple_of(blk * DL_BQ, DL_BQ)
                q = q_r[pl.ds(q0, DL_BQ), :]
                kw = kpad_ref[pl.ds(q0, DL_WIN), :]
                vw = vpad_ref[pl.ds(q0, DL_WIN), :]
                edge = jnp.where(blk == 0, 0, jnp.where(blk == nblk - 1, 2, 1))
                zero = jnp.zeros_like(q)
                outs = []
                for hh in range(2):
                    qm = jnp.where(lo, q, zero) if hh == 0 else jnp.where(lo, zero, q)
                    s = lax.dot_general(qm, kw, _NT_DIMS, preferred_element_type=F32) + m_ref[p, hh, edge]
                    m = jnp.max(s, axis=-1, keepdims=True)
                    e = jnp.exp(s - m)
                    l = jnp.sum(e, axis=-1, keepdims=True)
                    o = jnp.dot(e.astype(BF16), vw, preferred_element_type=F32) / l
                    outs.append((o, m + jnp.log(l)))
                o = jnp.where(lo, outs[0][0], outs[1][0])
                lse = jnp.where(lo, outs[0][1], outs[1][1])
                if dil == 1:
                    oacc_ref[p, pl.ds(q0, DL_BQ), :] = o
                    lse_ref[p, pl.ds(q0, DL_BQ), :] = lse
                else:
                    oacc_ref[p, pl.ds(q0 * dil + r, DL_BQ, stride=dil), :] = o
                    lse_ref[p, pl.ds(q0 * dil + r, DL_BQ, stride=dil), :] = lse
                return 0

            lax.fori_loop(0, nblk, block, 0)
            return 0

        lax.fori_loop(0, dil, residue, 0)

    run_pattern(0, 1, lambda r: q1_ref.at[0, 0], lambda r: k1_ref.at[0, 0], lambda r: v1_ref.at[0, 0])
    run_pattern(1, 4, lambda r: q4_ref.at[0, 0, r], lambda r: k4_ref.at[0, 0, r], lambda r: v4_ref.at[0, 0, r])
    run_pattern(2, 16, lambda r: q16_ref.at[0, 0, r], lambda r: k16_ref.at[0, 0, r], lambda r: v16_ref.at[0, 0, r])

    MIX = 256

    def mix(i, _):
        r0 = pl.multiple_of(i * MIX, MIX)
        ls = [lse_ref[p, pl.ds(r0, MIX), :] for p in range(3)]
        mx = jnp.maximum(jnp.maximum(ls[0], ls[1]), ls[2])
        ws = [jnp.exp(l - mx) for l in ls]
        den = ws[0] + ws[1] + ws[2]
        num = ws[0] * oacc_ref[0, pl.ds(r0, MIX), :]
        num += ws[1] * oacc_ref[1, pl.ds(r0, MIX), :]
        num += ws[2] * oacc_ref[2, pl.ds(r0, MIX), :]
        o_ref[0, 0, pl.ds(r0, MIX), :] = (num / den).astype(BF16)
        return 0

    lax.fori_loop(0, S // MIX, mix, 0)


def _dil_attn(p, p4, p16, m_bias):
    B, _, S, _ = p.shape
    NP = N_DIL_PAIRS
    base = 3 * N_DIFF_HEADS
    nat = lambda o: pl.BlockSpec((1, 1, S, LANES), lambda b, h: (b, base + o * NP + h, 0, 0))
    res = lambda o, d: pl.BlockSpec((1, 1, d, S // d, LANES), lambda b, h: (b, o * NP + h, 0, 0, 0))
    return pl.pallas_call(
        _dil_attn_body,
        grid=(B, NP),
        in_specs=[nat(0), nat(1), nat(2), res(0, 4), res(1, 4), res(2, 4), res(0, 16), res(1, 16), res(2, 16),
                  pl.BlockSpec((3, 2, 3, DL_BQ, DL_WIN), lambda b, h: (0, h, 0, 0, 0))],
        out_specs=pl.BlockSpec((1, 1, S, LANES), lambda b, h: (b, h, 0, 0)),
        out_shape=jax.ShapeDtypeStruct((B, NP, S, LANES), BF16),
        scratch_shapes=[
            pltpu.VMEM((S + 2 * DL_HALF, LANES), BF16),
            pltpu.VMEM((S + 2 * DL_HALF, LANES), BF16),
            pltpu.VMEM((3, S, LANES), F32),
            pltpu.VMEM((3, S, LANES), F32),
        ],
        compiler_params=_cparams(("arbitrary", "arbitrary")),
        name="dil_attn",
    )(p, p, p, p4, p4, p4, p16, p16, p16, m_bias)


OP_TM = 512


def _store_token_major(ref, val):
    rows = val.shape[0]
    for j in range(ROW_CHUNKS):
        ref[0, pl.ds(j, rows, stride=ROW_CHUNKS), :] = val[:, j * LANES:(j + 1) * LANES]


def _out_proj_body(oa_ref, od_ref, x_ref, w_ref, g_ref, wr_ref, x2_ref, h2_ref, aff_ref):
    pieces = [oa_ref[0, i] for i in range(N_DIFF_HEADS)] + [od_ref[0, i] for i in range(N_DIL_PAIRS)]
    a = jnp.concatenate(pieces, axis=-1)
    x2 = x_ref[0] + jnp.dot(a, w_ref[...], preferred_element_type=F32)
    ms = jnp.mean(x2 * x2, axis=-1, keepdims=True)
    h = x2 * lax.rsqrt(ms + NORM_EPS) * g_ref[...]
    logits = lax.dot_general(wr_ref[...], h, _NT_DIMS, preferred_element_type=F32,
                             precision=lax.Precision.HIGHEST)
    mx = jnp.max(logits, axis=0, keepdims=True)
    ex = jnp.exp(logits - mx)
    aff_ref[0] = ex / jnp.sum(ex, axis=0, keepdims=True)
    _store_token_major(x2_ref, x2)
    _store_token_major(h2_ref, h)


def _out_proj(oa, od, x, w_bf16, g, wr_t):
    B, S, _ = x.shape
    nt = S // OP_TM
    attn = lambda n: pl.BlockSpec((1, n, OP_TM, LANES), lambda b, t: (b, 0, t, 0))
    tok_major = pl.BlockSpec((1, OP_TM * ROW_CHUNKS, LANES), lambda b, t: (b, t, 0))
    return pl.pallas_call(
        _out_proj_body,
        grid=(B, nt),
        in_specs=[
            attn(N_DIFF_HEADS), attn(N_DIL_PAIRS),
            pl.BlockSpec((1, OP_TM, D_MODEL), lambda b, t: (b, t, 0)),
            pl.BlockSpec((D_MODEL, D_MODEL), lambda b, t: (0, 0)),
            pl.BlockSpec((1, D_MODEL), lambda b, t: (0, 0)),
            pl.BlockSpec((N_EXPERTS, D_MODEL), lambda b, t: (0, 0)),
        ],
        out_specs=[tok_major, tok_major, pl.BlockSpec((1, N_EXPERTS, OP_TM), lambda b, t: (b, 0, t))],
        out_shape=[
            jax.ShapeDtypeStruct((B, S * ROW_CHUNKS, LANES), F32),
            jax.ShapeDtypeStruct((B, S * ROW_CHUNKS, LANES), F32),
            jax.ShapeDtypeStruct((B, N_EXPERTS, S), F32),
        ],
        compiler_params=_cparams(("arbitrary", "arbitrary")),
        name="out_proj",
    )(oa, od, x, w_bf16, g, wr_t)


RT_CHUNK = 1024


def _route_body(aff_ref, idx_ref, gate_ref, posm_ref):
    E, S = aff_ref.shape[1], aff_ref.shape[2]
    C = idx_ref.shape[2]
    a = aff_ref[0]
    bits = pltpu.bitcast(a, I32)
    tau = jnp.zeros((E, 1), I32)
    for bit in range(30, -1, -1):
        cand = tau | (1 << bit)
        cnt = jnp.sum(jnp.where(bits >= cand, 1.0, 0.0).astype(F32), axis=1, keepdims=True)
        tau = jnp.where(cnt >= C, cand, tau)
    gt = bits > tau
    eq = bits == tau
    need = C - jnp.sum(jnp.where(gt, 1.0, 0.0).astype(F32), axis=1, keepdims=True)

    tri = (lax.broadcasted_iota(I32, (LANES, LANES), 0) <= lax.broadcasted_iota(I32, (LANES, LANES), 1)).astype(BF16)

    def excl_cumsum(mask_f):
        outs = []
        carry = jnp.zeros((E, 1), F32)
        for j in range(S // LANES):
            mj = mask_f[:, j * LANES:(j + 1) * LANES]
            inc = jnp.dot(mj.astype(BF16), tri, preferred_element_type=F32) + carry
            outs.append(inc - mj)
            carry = inc[:, LANES - 1:LANES]
        return jnp.concatenate(outs, axis=1)

    eq_f = jnp.where(eq, 1.0, 0.0).astype(F32)
    tie_rank = excl_cumsum(eq_f)
    sel = gt | (eq & (tie_rank < need))
    sel_f = jnp.where(sel, 1.0, 0.0).astype(F32)
    pos = excl_cumsum(sel_f)
    posm_ref[...] = jnp.where(sel, pos, -1.0)

    c_iota = lax.broadcasted_iota(I32, (C, RT_CHUNK), 0).astype(F32)
    row = lax.broadcasted_iota(I32, (SUBLANES, RT_CHUNK), 0)
    t_iota = lax.broadcasted_iota(I32, (SUBLANES, RT_CHUNK), 1)

    def expert(e, _):
        acc = jnp.zeros((SUBLANES, C), F32)
        for j in range(S // RT_CHUNK):
            pm = posm_ref[pl.ds(e, 1), j * RT_CHUNK:(j + 1) * RT_CHUNK]
            onehot = jnp.where(pm == c_iota, 1.0, 0.0).astype(F32)
            av = aff_ref[0, pl.ds(e, 1), j * RT_CHUNK:(j + 1) * RT_CHUNK]
            vals = jnp.where(row == 0, (t_iota + j * RT_CHUNK).astype(F32), jnp.where(row == 1, av, 0.0))
            acc += lax.dot_general(vals, onehot, _NT_DIMS, preferred_element_type=F32,
                                   precision=lax.Precision.HIGHEST)
        idx_ref[0, pl.ds(e, 1), :] = (acc[0:1, :] + 0.5).astype(I32)
        gate_ref[0, pl.ds(e, 1), :] = acc[1:2, :]
        return 0

    lax.fori_loop(0, E, expert, 0)


def _route(aff):
    B, E, S = aff.shape
    C = EC_K * S // E
    return pl.pallas_call(
        _route_body,
        grid=(B,),
        in_specs=[pl.BlockSpec((1, E, S), lambda b: (b, 0, 0))],
        out_specs=[pl.BlockSpec((1, E, C), lambda b: (b, 0, 0)), pl.BlockSpec((1, E, C), lambda b: (b, 0, 0))],
        out_shape=[jax.ShapeDtypeStruct((B, E, C), I32), jax.ShapeDtypeStruct((B, E, C), F32)],
        scratch_shapes=[pltpu.VMEM((E, S), F32)],
        compiler_params=_cparams(("arbitrary",)),
        name="route",
    )(aff)


GA_UNROLL = 8


def _gather_body(idx_ref, h_ref, xe_ref, g_ref):
    C = idx_ref.shape[2]

    def rows(i, _):
        for u in range(GA_UNROLL):
            r = i * GA_UNROLL + u
            t = idx_ref[0, 0, r]
            src = pl.multiple_of(t * ROW_CHUNKS, ROW_CHUNKS)
            dst = pl.multiple_of(r * ROW_CHUNKS, ROW_CHUNKS)
            g_ref[pl.ds(dst, ROW_CHUNKS), :] = h_ref[0, pl.ds(src, ROW_CHUNKS), :]
        return 0

    lax.fori_loop(0, C // GA_UNROLL, rows, 0)
    for j in range(ROW_CHUNKS):
        xe_ref[0, :, j * LANES:(j + 1) * LANES] = g_ref[pl.ds(j, C, stride=ROW_CHUNKS), :].astype(BF16)


def _gather(idx, h2):
    B, E, C = idx.shape
    SR = h2.shape[1]
    return pl.pallas_call(
        _gather_body,
        grid=(B, E),
        in_specs=[
            pl.BlockSpec((1, 1, C), lambda b, e: (b * E + e, 0, 0), memory_space=pltpu.SMEM),
            pl.BlockSpec((1, SR, LANES), lambda b, e: (b, 0, 0), pipeline_mode=pl.Buffered(1)),
        ],
        out_specs=pl.BlockSpec((1, C, D_MODEL), lambda b, e: (e, b, 0)),
        out_shape=jax.ShapeDtypeStruct((E, B * C, D_MODEL), BF16),
        scratch_shapes=[pltpu.VMEM((C * ROW_CHUNKS, LANES), F32)],
        compiler_params=_cparams(("arbitrary", "arbitrary")),
        name="gather",
    )(idx.reshape(B * E, 1, C), h2)


FF_TM = 512
FF_TF = 1408


def _ffn_body(x_ref, wg_ref, wu_ref, wd_ref, gate_ref, y_ref, acc_ref):
    f = pl.program_id(2)
    x = x_ref[0]
    hg = jnp.dot(x, wg_ref[0], preferred_element_type=F32)
    hu = jnp.dot(x, wu_ref[0], preferred_element_type=F32)
    he = (hg * jax.nn.sigmoid(hg) * hu).astype(BF16)
    part = jnp.dot(he, wd_ref[0], preferred_element_type=F32)

    @pl.when(f == 0)
    def _():
        acc_ref[...] = part

    @pl.when(f > 0)
    def _():
        acc_ref[...] += part

    @pl.when(f == pl.num_programs(2) - 1)
    def _():
        _store_token_major(y_ref, acc_ref[...] * gate_ref[0])


def _ffn(xe, wg, wu, wd, gate_col):
    E, M, _ = xe.shape
    nm = M // FF_TM
    nf = D_FF // FF_TF
    return pl.pallas_call(
        _ffn_body,
        grid=(E, nm, nf),
        in_specs=[
            pl.BlockSpec((1, FF_TM, D_MODEL), lambda e, m, f: (e, m, 0)),
            pl.BlockSpec((1, D_MODEL, FF_TF), lambda e, m, f: (e, 0, f)),
            pl.BlockSpec((1, D_MODEL, FF_TF), lambda e, m, f: (e, 0, f)),
            pl.BlockSpec((1, FF_TF, D_MODEL), lambda e, m, f: (e, f, 0)),
            pl.BlockSpec((1, FF_TM, 1), lambda e, m, f: (e, m, 0)),
        ],
        out_specs=pl.BlockSpec((1, FF_TM * ROW_CHUNKS, LANES), lambda e, m, f: (e, m, 0)),
        out_shape=jax.ShapeDtypeStruct((E, M * ROW_CHUNKS, LANES), F32),
        scratch_shapes=[pltpu.VMEM((FF_TM, D_MODEL), F32)],
        compiler_params=_cparams(("arbitrary", "arbitrary", "arbitrary")),
        name="expert_ffn",
    )(xe, wg, wu, wd, gate_col)


CB_UNROLL = 4
CB_TM = 512


def _combine_body(idx_ref, y_ref, x2_ref, g_ref, o_ref, acc_ref, n_ref):
    E = N_EXPERTS
    C = idx_ref.shape[2]
    s = pl.program_id(1)

    @pl.when(s == 0)
    def _():
        def zero(i, _):
            r0 = pl.multiple_of(i * 1024, 1024)
            acc_ref[pl.ds(r0, 1024), :] = jnp.zeros((1024, LANES), F32)
            return 0
        lax.fori_loop(0, acc_ref.shape[0] // 1024, zero, 0)

    @pl.when(s < E)
    def _():
        def rows(i, _):
            dsts, sums = [], []
            for u in range(CB_UNROLL):
                r = i * CB_UNROLL + u
                t = idx_ref[0, 0, r]
                dst = pl.multiple_of(t * ROW_CHUNKS, ROW_CHUNKS)
                src = pl.multiple_of(r * ROW_CHUNKS, ROW_CHUNKS)
                dsts.append(dst)
                sums.append(acc_ref[pl.ds(dst, ROW_CHUNKS), :] + y_ref[0, pl.ds(src, ROW_CHUNKS), :])
            for dst, v in zip(dsts, sums):
                acc_ref[pl.ds(dst, ROW_CHUNKS), :] = v
            return 0
        lax.fori_loop(0, C // CB_UNROLL, rows, 0)

    @pl.when(s >= E)
    def _():
        r0 = pl.multiple_of((s - E) * (CB_TM * ROW_CHUNKS), CB_TM * ROW_CHUNKS)
        v = x2_ref[0] + acc_ref[pl.ds(r0, CB_TM * ROW_CHUNKS), :]
        v3 = v.reshape(CB_TM, ROW_CHUNKS, LANES)
        ss = jnp.sum(jnp.sum(v3 * v3, axis=2, keepdims=True), axis=1, keepdims=True)
        n3 = v3 * lax.rsqrt(ss * (1.0 / D_MODEL) + NORM_EPS) * g_ref[...][None]
        n_ref[...] = n3.reshape(CB_TM * ROW_CHUNKS, LANES)
        for j in range(ROW_CHUNKS):
            o_ref[0, :, j * LANES:(j + 1) * LANES] = n_ref[pl.ds(j, CB_TM, stride=ROW_CHUNKS), :]


def _combine(idx, ye, x2, g_f):
    B, E, C = idx.shape
    SR = x2.shape[1]
    S = SR // ROW_CHUNKS
    nfin = S // CB_TM
    return pl.pallas_call(
        _combine_body,
        grid=(B, E + nfin),
        in_specs=[
            pl.BlockSpec((1, 1, C), lambda b, s: (b * E + jnp.minimum(s, E - 1), 0, 0), memory_space=pltpu.SMEM),
            pl.BlockSpec((1, C * ROW_CHUNKS, LANES), lambda b, s: (jnp.minimum(s, E - 1), b, 0)),
            pl.BlockSpec((1, CB_TM * ROW_CHUNKS, LANES), lambda b, s: (b, jnp.maximum(s - E, 0), 0)),
            pl.BlockSpec((ROW_CHUNKS, LANES), lambda b, s: (0, 0)),
        ],
        out_specs=pl.BlockSpec((1, CB_TM, D_MODEL), lambda b, s: (b, jnp.maximum(s - E, 0), 0)),
        out_shape=jax.ShapeDtypeStruct((B, S, D_MODEL), F32),
        scratch_shapes=[pltpu.VMEM((SR, LANES), F32), pltpu.VMEM((CB_TM * ROW_CHUNKS, LANES), F32)],
        compiler_params=_cparams(("arbitrary", "arbitrary")),
        name="combine",
    )(idx.reshape(B * E, 1, C), ye, x2, g_f)


def kernel(x, norm1_g, w_in, lam_q1, lam_k1, lam_q2, lam_k2, subln_g, w_out, rel_bias,
           norm2_g, w_router, w_gate, w_up, w_down, norm_f_g):
    B, S, _ = x.shape
    C = EC_K * S // N_EXPERTS
    e_bias = _diff_bias(rel_bias)
    m_bias = _dil_bias(rel_bias)
    p, p4, p16 = _in_proj(x, norm1_g, w_in[0].astype(BF16))
    oa = _diff_attn(p, e_bias, lam_q1, lam_k1, lam_q2, lam_k2, subln_g)
    od = _dil_attn(p, p4, p16, m_bias)
    x2, h2, aff = _out_proj(oa, od, x, w_out[0].astype(BF16), norm2_g, w_router[0].T)
    idx, gate = _route(aff)
    xe = _gather(idx, h2)
    gate_col = gate.transpose(1, 0, 2).reshape(N_EXPERTS, B * C, 1)
    ye = _ffn(xe, w_gate[0].astype(BF16), w_up[0].astype(BF16), w_down[0].astype(BF16), gate_col)
    return _combine(idx, ye, x2, norm_f_g.reshape(ROW_CHUNKS, LANES))
```

```python
import functools
import math

import jax
import jax.numpy as jnp
from jax import lax
from jax.experimental import pallas as pl
from jax.experimental.pallas import tpu as pltpu

F32 = jnp.float32
BF16 = jnp.bfloat16
I32 = jnp.int32

D_MODEL = 1024
HEAD_DIM = 64
N_DIFF_HEADS = 4
N_DIL_HEADS = 8
N_DIL_PAIRS = N_DIL_HEADS // 2
DIFF_WIDTH = 512
DIL_WIDTH = 512
N_IN = 3 * DIFF_WIDTH + 3 * DIL_WIDTH
DIL_PATTERNS = ((128, 1), (512, 4), (2048, 16))
N_BUCKETS = 32
REL_MAX_DIST = 1024
N_EXPERTS = 16
EC_K = 2
D_FF = 2816
NORM_EPS = 1e-6
SUBLN_EPS = 1e-5
NEG = -1e30
LAMBDA_INIT = 0.8 - 0.6 * math.exp(0.0)

LANES = 128
SUBLANES = 8
ROW_CHUNKS = D_MODEL // LANES

DA_TQ = 128
DA_TK = 512
DA_DMIN = -1152
DA_DMAX = 768
DA_WTOT = DA_DMAX - DA_DMIN + DA_TK

DL_BQ = 128
DL_HALF = 64
DL_WIN = DL_BQ + 2 * DL_HALF
DL_GROUP = 4

LOG2E = math.log2(math.e)

VMEM_LIMIT = 56 * 1024 * 1024


def _cparams(sem):
    return pltpu.CompilerParams(dimension_semantics=sem, vmem_limit_bytes=VMEM_LIMIT)


def _t5_bucket(rel):
    half = N_BUCKETS // 2
    max_exact = half // 2
    n = jnp.abs(rel)
    nf = jnp.maximum(n, 1).astype(F32)
    large = max_exact + (jnp.log(nf / max_exact) / math.log(REL_MAX_DIST / max_exact)
                         * (half - max_exact)).astype(I32)
    large = jnp.minimum(large, half - 1)
    return jnp.where(rel > 0, half, 0) + jnp.where(n < max_exact, n, large)


def _bias_lookup(rel, tab_ref, col):
    bucket = _t5_bucket(rel)
    acc = jnp.zeros(rel.shape, F32)
    for j in range(N_BUCKETS):
        acc = jnp.where(bucket == j, tab_ref[j, col], acc)
    return acc


def _diff_bias_body(tab_ref, e_ref):
    h = pl.program_id(0)

    def chunk(cc, _):
        c0 = pl.multiple_of(cc * LANES, LANES)
        i = lax.broadcasted_iota(I32, (DA_TQ, LANES), 0)
        c = lax.broadcasted_iota(I32, (DA_TQ, LANES), 1) + c0
        e_ref[0, :, pl.ds(c0, LANES)] = _bias_lookup(c - i + DA_DMIN, tab_ref, h) * LOG2E
        return 0

    lax.fori_loop(0, DA_WTOT // LANES, chunk, 0)


def _diff_bias(rel_bias):
    return pl.pallas_call(
        _diff_bias_body,
        grid=(N_DIFF_HEADS,),
        in_specs=[pl.BlockSpec(memory_space=pltpu.SMEM)],
        out_specs=pl.BlockSpec((1, DA_TQ, DA_WTOT), lambda h: (h, 0, 0)),
        out_shape=jax.ShapeDtypeStruct((N_DIFF_HEADS, DA_TQ, DA_WTOT), F32),
        compiler_params=_cparams(("arbitrary",)),
        name="diff_bias",
    )(rel_bias)


def _dil_bias_body(tab_ref, m_ref):
    p = pl.program_id(0)
    h = pl.program_id(1)
    dil = jnp.where(p == 0, DIL_PATTERNS[0][1], jnp.where(p == 1, DIL_PATTERNS[1][1], DIL_PATTERNS[2][1]))
    i = lax.broadcasted_iota(I32, (DL_BQ, DL_WIN), 0)
    c = lax.broadcasted_iota(I32, (DL_BQ, DL_WIN), 1)
    off = c - DL_HALF - i
    bias = _bias_lookup(off * dil, tab_ref, N_DIFF_HEADS + h) * LOG2E
    band = jnp.abs(off) <= DL_HALF
    m_ref[0, 0, 0] = jnp.where(band & (c >= DL_HALF), bias, NEG)
    m_ref[0, 0, 1] = jnp.where(band, bias, NEG)
    m_ref[0, 0, 2] = jnp.where(band & (c < DL_HALF + DL_BQ), bias, NEG)


def _dil_bias(rel_bias):
    n_pat = len(DIL_PATTERNS)
    return pl.pallas_call(
        _dil_bias_body,
        grid=(n_pat, N_DIL_HEADS),
        in_specs=[pl.BlockSpec(memory_space=pltpu.SMEM)],
        out_specs=pl.BlockSpec((1, 1, 3, DL_BQ, DL_WIN), lambda p, h: (p, h, 0, 0, 0)),
        out_shape=jax.ShapeDtypeStruct((n_pat, N_DIL_HEADS, 3, DL_BQ, DL_WIN), F32),
        compiler_params=_cparams(("arbitrary", "arbitrary")),
        name="dil_bias",
    )(rel_bias)


IP_TM = 512
IP_TN = 256
N_COLBLK = N_IN // LANES
N_DIL_COLBLK = 3 * DIL_WIDTH // LANES


def _in_proj_body(x_ref, g_ref, w_ref, p_ref, p4_ref, p16_ref, scr_ref):
    x = x_ref[0]
    ms = jnp.mean(x * x, axis=-1, keepdims=True)
    y = (x * lax.rsqrt(ms + NORM_EPS) * g_ref[...]).astype(BF16)
    q_scale = HEAD_DIM ** -0.5 * LOG2E
    for n in range(N_IN // IP_TN):
        c0 = n * IP_TN
        r = jnp.dot(y, w_ref[:, c0:c0 + IP_TN], preferred_element_type=F32)
        is_q = (c0 < DIFF_WIDTH) or (3 * DIFF_WIDTH <= c0 < 3 * DIFF_WIDTH + DIL_WIDTH)
        if is_q:
            r = r * q_scale
        for k in range(IP_TN // LANES):
            p_ref[0, 2 * n + k] = r[:, k * LANES:(k + 1) * LANES].astype(BF16)
        if c0 >= 3 * DIFF_WIDTH:
            for k in range(IP_TN // LANES):
                scr_ref[(c0 - 3 * DIFF_WIDTH) // LANES + k] = r[:, k * LANES:(k + 1) * LANES]
    for c in range(N_DIL_COLBLK):
        for r_ in range(4):
            p4_ref[0, c, r_] = scr_ref[c, pl.ds(r_, IP_TM // 4, stride=4), :].astype(BF16)
        for r_ in range(16):
            p16_ref[0, c, r_] = scr_ref[c, pl.ds(r_, IP_TM // 16, stride=16), :].astype(BF16)


def _in_proj(x, g, w_bf16):
    B, S, _ = x.shape
    nt = S // IP_TM
    return pl.pallas_call(
        _in_proj_body,
        grid=(B, nt),
        in_specs=[
            pl.BlockSpec((1, IP_TM, D_MODEL), lambda b, t: (b, t, 0)),
            pl.BlockSpec((1, D_MODEL), lambda b, t: (0, 0)),
            pl.BlockSpec((D_MODEL, N_IN), lambda b, t: (0, 0)),
        ],
        out_specs=[
            pl.BlockSpec((1, N_COLBLK, IP_TM, LANES), lambda b, t: (b, 0, t, 0)),
            pl.BlockSpec((1, N_DIL_COLBLK, 4, IP_TM // 4, LANES), lambda b, t: (b, 0, 0, t, 0)),
            pl.BlockSpec((1, N_DIL_COLBLK, 16, IP_TM // 16, LANES), lambda b, t: (b, 0, 0, t, 0)),
        ],
        out_shape=[
            jax.ShapeDtypeStruct((B, N_COLBLK, S, LANES), BF16),
            jax.ShapeDtypeStruct((B, N_DIL_COLBLK, 4, S // 4, LANES), BF16),
            jax.ShapeDtypeStruct((B, N_DIL_COLBLK, 16, S // 16, LANES), BF16),
        ],
        scratch_shapes=[pltpu.VMEM((N_DIL_COLBLK, IP_TM, LANES), F32)],
        compiler_params=_cparams(("arbitrary", "arbitrary")),
        name="in_proj",
    )(x, g, w_bf16)


_NT_DIMS = (((1,), (1,)), ((), ()))


def _diff_attn_body(q_ref, k_ref, v_ref, e_ref, lq1_ref, lk1_ref, lq2_ref, lk2_ref, sg_ref, o_ref,
                    kx_ref, vx_ref, p_ref):
    S = q_ref.shape[2]
    TQ, TK = DA_TQ, DA_TK
    lam = (jnp.exp(jnp.sum(lq1_ref[...] * lk1_ref[...], axis=-1, keepdims=True))
           - jnp.exp(jnp.sum(lq2_ref[...] * lk2_ref[...], axis=-1, keepdims=True)) + LAMBDA_INIT)
    lane_q = lax.broadcasted_iota(I32, (TQ, LANES), 1)
    lo = lane_q < HEAD_DIM
    lane2 = lax.broadcasted_iota(I32, (2 * TQ, LANES), 1)

    def fill(i, _):
        r0 = pl.multiple_of(i * 512, 512)
        e0 = jnp.where(lax.broadcasted_iota(I32, (512, LANES), 1) == 0, 1.0, 0.0).astype(BF16)
        kx_ref[pl.ds(r0, 512), 0:LANES] = k_ref[0, 0, pl.ds(r0, 512), :]
        kx_ref[pl.ds(r0, 512), LANES:2 * LANES] = e0
        vx_ref[pl.ds(r0, 512), 0:LANES] = v_ref[0, 0, pl.ds(r0, 512), :]
        vx_ref[pl.ds(r0, 512), LANES:2 * LANES] = e0
        return 0

    lax.fori_loop(0, S // 512, fill, 0)

    def q_tile(q0, exact_shift):
        q = q_ref[0, 0, pl.ds(q0, TQ), :]
        zero = jnp.zeros_like(q)
        qa = jnp.concatenate([jnp.where(lo, q, zero), jnp.where(lo, zero, q)], axis=0)

        def bias_tile(k0):
            off = pl.multiple_of(jnp.clip(k0 - q0, DA_DMIN, DA_DMAX) - DA_DMIN, LANES)
            return e_ref[0, :, pl.ds(off, TK)]

        def tile_max(k0, pm):
            s = lax.dot_general(qa, k_ref[0, 0, pl.ds(k0, TK), :], _NT_DIMS, preferred_element_type=F32)
            bias = bias_tile(k0)
            out = []
            for half in range(2):
                sb = s[half * TQ:(half + 1) * TQ] + bias
                t = pm[half]
                for j in range(TK // LANES):
                    t = jnp.maximum(t, sb[:, j * LANES:(j + 1) * LANES])
                out.append(t)
            return tuple(out)

        neg = jnp.full((TQ, LANES), -jnp.inf, F32)
        if exact_shift:
            pm = lax.fori_loop(0, S // TK, lambda ki, pm: tile_max(pl.multiple_of(ki * TK, TK), pm), (neg, neg))
        else:
            pm = tile_max(0, (neg, neg))
        m = jnp.max(jnp.concatenate(pm, axis=0), axis=-1, keepdims=True)
        mcol = jnp.where(lane2 == 0, -m, 0.0).astype(BF16)
        qx = jnp.concatenate([qa, mcol], axis=1)

        for ki in range(S // TK):
            k0 = ki * TK
            s = lax.dot_general(qx, kx_ref[k0:k0 + TK, :], _NT_DIMS, preferred_element_type=F32)
            bias = bias_tile(k0)
            for half in range(2):
                pr = jnp.exp2(s[half * TQ:(half + 1) * TQ] + bias).astype(BF16)
                p_ref[half * TQ:(half + 1) * TQ, k0:k0 + TK] = pr

        acc = jnp.dot(p_ref[...], vx_ref[...], preferred_element_type=F32)
        o1 = acc[0:TQ, 0:LANES] / acc[0:TQ, LANES:LANES + 1]
        o2 = acc[TQ:2 * TQ, 0:LANES] / acc[TQ:2 * TQ, LANES:LANES + 1]
        o = o1 - lam * o2
        ms = jnp.mean(o * o, axis=-1, keepdims=True)
        o = o * lax.rsqrt(ms + SUBLN_EPS) * sg_ref[...] * (1.0 - LAMBDA_INIT)
        o_ref[0, 0, pl.ds(q0, TQ), :] = o.astype(BF16)
        return jnp.where(jnp.isfinite(o), 0.0, 1.0)

    def fast_loop(qi, bad):
        return jnp.maximum(bad, q_tile(pl.multiple_of(qi * TQ, TQ), exact_shift=False))

    bad = lax.fori_loop(0, S // TQ, fast_loop, jnp.zeros((TQ, LANES), F32))

    @pl.when(jnp.max(bad) > 0.0)
    def _():
        def exact_loop(qi, _):
            q_tile(pl.multiple_of(qi * TQ, TQ), exact_shift=True)
            return 0
        lax.fori_loop(0, S // TQ, exact_loop, 0)


def _diff_attn(p, e_bias, lq1, lk1, lq2, lk2, subln_g):
    B, _, S, _ = p.shape
    H = N_DIFF_HEADS
    vec = lambda n: pl.BlockSpec((1, n), lambda b, h: (0, 0))
    return pl.pallas_call(
        _diff_attn_body,
        grid=(B, H),
        in_specs=[
            pl.BlockSpec((1, 1, S, LANES), lambda b, h: (b, h, 0, 0)),
            pl.BlockSpec((1, 1, S, LANES), lambda b, h: (b, H + h, 0, 0)),
            pl.BlockSpec((1, 1, S, LANES), lambda b, h: (b, 2 * H + h, 0, 0)),
            pl.BlockSpec((1, DA_TQ, DA_WTOT), lambda b, h: (h, 0, 0)),
            vec(HEAD_DIM), vec(HEAD_DIM), vec(HEAD_DIM), vec(HEAD_DIM), vec(2 * HEAD_DIM),
        ],
        out_specs=pl.BlockSpec((1, 1, S, LANES), lambda b, h: (b, h, 0, 0)),
        out_shape=jax.ShapeDtypeStruct((B, H, S, LANES), BF16),
        scratch_shapes=[
            pltpu.VMEM((S, 2 * LANES), BF16),
            pltpu.VMEM((S, 2 * LANES), BF16),
            pltpu.VMEM((2 * DA_TQ, S), BF16),
        ],
        compiler_params=_cparams(("arbitrary", "arbitrary")),
        name="diff_attn",
    )(p, p, p, e_bias, lq1, lk1, lq2, lk2, subln_g)


def _dil_attn_body(q1_ref, k1_ref, v1_ref, q4_ref, k4_ref, v4_ref, q16_ref, k16_ref, v16_ref, m_ref,
                   o_ref, kpad_ref, vpad_ref, oacc_ref, lse_ref):
    S = q1_ref.shape[2]
    lane = lax.broadcasted_iota(I32, (DL_BQ, LANES), 1)
    lo = lane < HEAD_DIM
    zpad = jnp.zeros((DL_HALF, LANES), BF16)

    def run_pattern(p, dil, get_q, get_k, get_v):
        L = S // dil
        nblk = L // DL_BQ
        LP = L + 2 * DL_HALF

        def fill(r, _):
            base = pl.multiple_of(r * LP, DL_HALF)
            kpad_ref[pl.ds(base, DL_HALF), :] = zpad
            vpad_ref[pl.ds(base, DL_HALF), :] = zpad
            kpad_ref[pl.ds(base + DL_HALF, L), :] = get_k(r)[...]
            vpad_ref[pl.ds(base + DL_HALF, L), :] = get_v(r)[...]
            kpad_ref[pl.ds(base + DL_HALF + L, DL_HALF), :] = zpad
            vpad_ref[pl.ds(base + DL_HALF + L, DL_HALF), :] = zpad
            return 0

        lax.fori_loop(0, dil, fill, 0)

        def group(gi, _):
            for u in range(DL_GROUP):
                n = gi * DL_GROUP + u
                r = n // nblk
                blk = n % nblk
                q0 = pl.multiple_of(blk * DL_BQ, DL_BQ)
                w0 = pl.multiple_of(r * LP + q0, DL_HALF)
                q = get_q(r)[pl.ds(q0, DL_BQ), :]
                kw = kpad_ref[pl.ds(w0, DL_WIN), :]
                vw = vpad_ref[pl.ds(w0, DL_WIN), :]
                edge = jnp.where(blk == 0, 0, jnp.where(blk == nblk - 1, 2, 1))
                zero = jnp.zeros_like(q)
                outs = []
                for hh in range(2):
                    qm = jnp.where(lo, q, zero) if hh == 0 else jnp.where(lo, zero, q)
                    s = lax.dot_general(qm, kw, _NT_DIMS, preferred_element_type=F32) + m_ref[p, hh, edge]
                    m = jnp.max(s, axis=-1, keepdims=True)
                    e = jnp.exp2(s - m)
                    l = jnp.sum(e, axis=-1, keepdims=True)
                    o = jnp.dot(e.astype(BF16), vw, preferred_element_type=F32) / l
                    outs.append((o, m + jnp.log2(l)))
                o = jnp.where(lo, outs[0][0], outs[1][0])
                lse = jnp.where(lo, outs[0][1], outs[1][1])
                if dil == 1:
                    oacc_ref[p, pl.ds(q0, DL_BQ), :] = o
                    lse_ref[p, pl.ds(q0, DL_BQ), :] = lse
                else:
                    oacc_ref[p, pl.ds(q0 * dil + r, DL_BQ, stride=dil), :] = o
                    lse_ref[p, pl.ds(q0 * dil + r, DL_BQ, stride=dil), :] = lse
            return 0

        lax.fori_loop(0, dil * nblk // DL_GROUP, group, 0)

    run_pattern(0, 1, lambda r: q1_ref.at[0, 0], lambda r: k1_ref.at[0, 0], lambda r: v1_ref.at[0, 0])
    run_pattern(1, 4, lambda r: q4_ref.at[0, 0, r], lambda r: k4_ref.at[0, 0, r], lambda r: v4_ref.at[0, 0, r])
    run_pattern(2, 16, lambda r: q16_ref.at[0, 0, r], lambda r: k16_ref.at[0, 0, r], lambda r: v16_ref.at[0, 0, r])

    MIX = 256

    def mix(i, _):
        r0 = pl.multiple_of(i * MIX, MIX)
        ls = [lse_ref[p, pl.ds(r0, MIX), :] for p in range(3)]
        mx = jnp.maximum(jnp.maximum(ls[0], ls[1]), ls[2])
        ws = [jnp.exp2(l - mx) for l in ls]
        den = ws[0] + ws[1] + ws[2]
        num = ws[0] * oacc_ref[0, pl.ds(r0, MIX), :]
        num += ws[1] * oacc_ref[1, pl.ds(r0, MIX), :]
        num += ws[2] * oacc_ref[2, pl.ds(r0, MIX), :]
        o_ref[0, 0, pl.ds(r0, MIX), :] = (num / den).astype(BF16)
        return 0

    lax.fori_loop(0, S // MIX, mix, 0)


def _dil_attn(p, p4, p16, m_bias):
    B, _, S, _ = p.shape
    NP = N_DIL_PAIRS
    base = 3 * N_DIFF_HEADS
    nat = lambda o: pl.BlockSpec((1, 1, S, LANES), lambda b, h: (b, base + o * NP + h, 0, 0))
    res = lambda o, d: pl.BlockSpec((1, 1, d, S // d, LANES), lambda b, h: (b, o * NP + h, 0, 0, 0))
    return pl.pallas_call(
        _dil_attn_body,
        grid=(B, NP),
        in_specs=[nat(0), nat(1), nat(2), res(0, 4), res(1, 4), res(2, 4), res(0, 16), res(1, 16), res(2, 16),
                  pl.BlockSpec((3, 2, 3, DL_BQ, DL_WIN), lambda b, h: (0, h, 0, 0, 0))],
        out_specs=pl.BlockSpec((1, 1, S, LANES), lambda b, h: (b, h, 0, 0)),
        out_shape=jax.ShapeDtypeStruct((B, NP, S, LANES), BF16),
        scratch_shapes=[
            pltpu.VMEM((S + 2 * DL_HALF * DIL_PATTERNS[-1][1], LANES), BF16),
            pltpu.VMEM((S + 2 * DL_HALF * DIL_PATTERNS[-1][1], LANES), BF16),
            pltpu.VMEM((3, S, LANES), F32),
            pltpu.VMEM((3, S, LANES), F32),
        ],
        compiler_params=_cparams(("arbitrary", "arbitrary")),
        name="dil_attn",
    )(p, p, p, p4, p4, p4, p16, p16, p16, m_bias)


OP_TM = 512


def _store_token_major(ref, val):
    rows = val.shape[0]
    for j in range(ROW_CHUNKS):
        ref[0, pl.ds(j, rows, stride=ROW_CHUNKS), :] = val[:, j * LANES:(j + 1) * LANES]


def _out_proj_body(oa_ref, od_ref, x_ref, w_ref, g_ref, wr_ref, x2_ref, h2_ref, aff_ref):
    pieces = [oa_ref[0, i] for i in range(N_DIFF_HEADS)] + [od_ref[0, i] for i in range(N_DIL_PAIRS)]
    a = jnp.concatenate(pieces, axis=-1)
    x2 = x_ref[0] + jnp.dot(a, w_ref[...], preferred_element_type=F32)
    ms = jnp.mean(x2 * x2, axis=-1, keepdims=True)
    h = x2 * lax.rsqrt(ms + NORM_EPS) * g_ref[...]
    logits = lax.dot_general(wr_ref[...], h, _NT_DIMS, preferred_element_type=F32,
                             precision=lax.Precision.HIGHEST)
    mx = jnp.max(logits, axis=0, keepdims=True)
    ex = jnp.exp(logits - mx)
    aff_ref[0] = ex / jnp.sum(ex, axis=0, keepdims=True)
    _store_token_major(x2_ref, x2)
    _store_token_major(h2_ref, h)


def _out_proj(oa, od, x, w_bf16, g, wr_t):
    B, S, _ = x.shape
    nt = S // OP_TM
    attn = lambda n: pl.BlockSpec((1, n, OP_TM, LANES), lambda b, t: (b, 0, t, 0))
    tok_major = pl.BlockSpec((1, OP_TM * ROW_CHUNKS, LANES), lambda b, t: (b, t, 0))
    return pl.pallas_call(
        _out_proj_body,
        grid=(B, nt),
        in_specs=[
            attn(N_DIFF_HEADS), attn(N_DIL_PAIRS),
            pl.BlockSpec((1, OP_TM, D_MODEL), lambda b, t: (b, t, 0)),
            pl.BlockSpec((D_MODEL, D_MODEL), lambda b, t: (0, 0)),
            pl.BlockSpec((1, D_MODEL), lambda b, t: (0, 0)),
            pl.BlockSpec((N_EXPERTS, D_MODEL), lambda b, t: (0, 0)),
        ],
        out_specs=[tok_major, tok_major, pl.BlockSpec((1, N_EXPERTS, OP_TM), lambda b, t: (b, 0, t))],
        out_shape=[
            jax.ShapeDtypeStruct((B, S * ROW_CHUNKS, LANES), F32),
            jax.ShapeDtypeStruct((B, S * ROW_CHUNKS, LANES), F32),
            jax.ShapeDtypeStruct((B, N_EXPERTS, S), F32),
        ],
        compiler_params=_cparams(("arbitrary", "arbitrary")),
        name="out_proj",
    )(oa, od, x, w_bf16, g, wr_t)


RT_CHUNK = 1024


def _route_body(aff_ref, idx_ref, gate_ref, posm_ref):
    E, S = aff_ref.shape[1], aff_ref.shape[2]
    C = idx_ref.shape[2]
    a = aff_ref[0]
    bits = pltpu.bitcast(a, I32)
    tau = jnp.zeros((E, 1), I32)
    for bit in range(30, -1, -1):
        cand = tau | (1 << bit)
        cnt = jnp.sum(jnp.where(bits >= cand, 1.0, 0.0).astype(F32), axis=1, keepdims=True)
        tau = jnp.where(cnt >= C, cand, tau)
    gt = bits > tau
    eq = bits == tau
    need = C - jnp.sum(jnp.where(gt, 1.0, 0.0).astype(F32), axis=1, keepdims=True)

    tri = (lax.broadcasted_iota(I32, (LANES, LANES), 0) <= lax.broadcasted_iota(I32, (LANES, LANES), 1)).astype(BF16)

    def excl_cumsum(mask_f):
        outs = []
        carry = jnp.zeros((E, 1), F32)
        for j in range(S // LANES):
            mj = mask_f[:, j * LANES:(j + 1) * LANES]
            inc = jnp.dot(mj.astype(BF16), tri, preferred_element_type=F32) + carry
            outs.append(inc - mj)
            carry = inc[:, LANES - 1:LANES]
        return jnp.concatenate(outs, axis=1)

    eq_f = jnp.where(eq, 1.0, 0.0).astype(F32)
    tie_rank = excl_cumsum(eq_f)
    sel = gt | (eq & (tie_rank < need))
    sel_f = jnp.where(sel, 1.0, 0.0).astype(F32)
    pos = excl_cumsum(sel_f)
    posm_ref[...] = jnp.where(sel, pos, -1.0)

    c_iota = lax.broadcasted_iota(I32, (C, RT_CHUNK), 0).astype(F32)
    row = lax.broadcasted_iota(I32, (SUBLANES, RT_CHUNK), 0)
    t_iota = lax.broadcasted_iota(I32, (SUBLANES, RT_CHUNK), 1)

    def expert(e, _):
        acc = jnp.zeros((SUBLANES, C), F32)
        for j in range(S // RT_CHUNK):
            pm = posm_ref[pl.ds(e, 1), j * RT_CHUNK:(j + 1) * RT_CHUNK]
            onehot = jnp.where(pm == c_iota, 1.0, 0.0).astype(F32)
            av = aff_ref[0, pl.ds(e, 1), j * RT_CHUNK:(j + 1) * RT_CHUNK]
            vals = jnp.where(row == 0, (t_iota + j * RT_CHUNK).astype(F32), jnp.where(row == 1, av, 0.0))
            acc += lax.dot_general(vals, onehot, _NT_DIMS, preferred_element_type=F32,
                                   precision=lax.Precision.HIGHEST)
        idx_ref[0, pl.ds(e, 1), :] = (acc[0:1, :] + 0.5).astype(I32)
        gate_ref[0, pl.ds(e, 1), :] = acc[1:2, :]
        return 0

    lax.fori_loop(0, E, expert, 0)


def _route(aff):
    B, E, S = aff.shape
    C = EC_K * S // E
    return pl.pallas_call(
        _route_body,
        grid=(B,),
        in_specs=[pl.BlockSpec((1, E, S), lambda b: (b, 0, 0))],
        out_specs=[pl.BlockSpec((1, E, C), lambda b: (b, 0, 0)), pl.BlockSpec((1, E, C), lambda b: (b, 0, 0))],
        out_shape=[jax.ShapeDtypeStruct((B, E, C), I32), jax.ShapeDtypeStruct((B, E, C), F32)],
        scratch_shapes=[pltpu.VMEM((E, S), F32)],
        compiler_params=_cparams(("arbitrary",)),
        name="route",
    )(aff)


GA_UNROLL = 8


def _gather_body(idx_ref, h_ref, xe_ref, g_ref):
    C = idx_ref.shape[2]

    def rows(i, _):
        for u in range(GA_UNROLL):
            r = i * GA_UNROLL + u
            t = idx_ref[0, 0, r]
            src = pl.multiple_of(t * ROW_CHUNKS, ROW_CHUNKS)
            dst = pl.multiple_of(r * ROW_CHUNKS, ROW_CHUNKS)
            g_ref[pl.ds(dst, ROW_CHUNKS), :] = h_ref[0, pl.ds(src, ROW_CHUNKS), :]
        return 0

    lax.fori_loop(0, C // GA_UNROLL, rows, 0)
    for j in range(ROW_CHUNKS):
        xe_ref[0, :, j * LANES:(j + 1) * LANES] = g_ref[pl.ds(j, C, stride=ROW_CHUNKS), :].astype(BF16)


def _gather(idx, h2):
    B, E, C = idx.shape
    SR = h2.shape[1]
    return pl.pallas_call(
        _gather_body,
        grid=(B, E),
        in_specs=[
            pl.BlockSpec((1, 1, C), lambda b, e: (b * E + e, 0, 0), memory_space=pltpu.SMEM),
            pl.BlockSpec((1, SR, LANES), lambda b, e: (b, 0, 0), pipeline_mode=pl.Buffered(1)),
        ],
        out_specs=pl.BlockSpec((1, C, D_MODEL), lambda b, e: (e, b, 0)),
        out_shape=jax.ShapeDtypeStruct((E, B * C, D_MODEL), BF16),
        scratch_shapes=[pltpu.VMEM((C * ROW_CHUNKS, LANES), F32)],
        compiler_params=_cparams(("arbitrary", "arbitrary")),
        name="gather",
    )(idx.reshape(B * E, 1, C), h2)


FF_TM = 512
FF_TF = 1408


def _ffn_body(x_ref, wg_ref, wu_ref, wd_ref, gate_ref, y_ref, acc_ref):
    f = pl.program_id(2)
    x = x_ref[0]
    hg = jnp.dot(x, wg_ref[0], preferred_element_type=F32)
    hu = jnp.dot(x, wu_ref[0], preferred_element_type=F32)
    he = (hg * jax.nn.sigmoid(hg) * hu).astype(BF16)
    part = jnp.dot(he, wd_ref[0], preferred_element_type=F32)

    @pl.when(f == 0)
    def _():
        acc_ref[...] = part

    @pl.when(f > 0)
    def _():
        acc_ref[...] += part

    @pl.when(f == pl.num_programs(2) - 1)
    def _():
        _store_token_major(y_ref, acc_ref[...] * gate_ref[0])


def _ffn(xe, wg, wu, wd, gate_col):
    E, M, _ = xe.shape
    nm = M // FF_TM
    nf = D_FF // FF_TF
    return pl.pallas_call(
        _ffn_body,
        grid=(E, nm, nf),
        in_specs=[
            pl.BlockSpec((1, FF_TM, D_MODEL), lambda e, m, f: (e, m, 0)),
            pl.BlockSpec((1, D_MODEL, FF_TF), lambda e, m, f: (e, 0, f)),
            pl.BlockSpec((1, D_MODEL, FF_TF), lambda e, m, f: (e, 0, f)),
            pl.BlockSpec((1, FF_TF, D_MODEL), lambda e, m, f: (e, f, 0)),
            pl.BlockSpec((1, FF_TM, 1), lambda e, m, f: (e, m, 0)),
        ],
        out_specs=pl.BlockSpec((1, FF_TM * ROW_CHUNKS, LANES), lambda e, m, f: (e, m, 0)),
        out_shape=jax.ShapeDtypeStruct((E, M * ROW_CHUNKS, LANES), F32),
        scratch_shapes=[pltpu.VMEM((FF_TM, D_MODEL), F32)],
        compiler_params=_cparams(("arbitrary", "arbitrary", "arbitrary")),
        name="expert_ffn",
    )(xe, wg, wu, wd, gate_col)


CB_UNROLL = 4
CB_TM = 512


def _combine_body(idx_ref, y_ref, x2_ref, g_ref, o_ref, acc_ref, n_ref):
    E = N_EXPERTS
    C = idx_ref.shape[2]
    s = pl.program_id(1)

    @pl.when(s == 0)
    def _():
        def zero(i, _):
            r0 = pl.multiple_of(i * 1024, 1024)
            acc_ref[pl.ds(r0, 1024), :] = jnp.zeros((1024, LANES), F32)
            return 0
        lax.fori_loop(0, acc_ref.shape[0] // 1024, zero, 0)

    @pl.when(s < E)
    def _():
        def rows(i, _):
            dsts, sums = [], []
            for u in range(CB_UNROLL):
                r = i * CB_UNROLL + u
                t = idx_ref[0, 0, r]
                dst = pl.multiple_of(t * ROW_CHUNKS, ROW_CHUNKS)
                src = pl.multiple_of(r * ROW_CHUNKS, ROW_CHUNKS)
                dsts.append(dst)
                sums.append(acc_ref[pl.ds(dst, ROW_CHUNKS), :] + y_ref[0, pl.ds(src, ROW_CHUNKS), :])
            for dst, v in zip(dsts, sums):
                acc_ref[pl.ds(dst, ROW_CHUNKS), :] = v
            return 0
        lax.fori_loop(0, C // CB_UNROLL, rows, 0)

    @pl.when(s >= E)
    def _():
        r0 = pl.multiple_of((s - E) * (CB_TM * ROW_CHUNKS), CB_TM * ROW_CHUNKS)
        v = x2_ref[0] + acc_ref[pl.ds(r0, CB_TM * ROW_CHUNKS), :]
        v3 = v.reshape(CB_TM, ROW_CHUNKS, LANES)
        ss = jnp.sum(jnp.sum(v3 * v3, axis=2, keepdims=True), axis=1, keepdims=True)
        n3 = v3 * lax.rsqrt(ss * (1.0 / D_MODEL) + NORM_EPS) * g_ref[...][None]
        n_ref[...] = n3.reshape(CB_TM * ROW_CHUNKS, LANES)
        for j in range(ROW_CHUNKS):
            o_ref[0, :, j * LANES:(j + 1) * LANES] = n_ref[pl.ds(j, CB_TM, stride=ROW_CHUNKS), :]


def _combine(idx, ye, x2, g_f):
    B, E, C = idx.shape
    SR = x2.shape[1]
    S = SR // ROW_CHUNKS
    nfin = S // CB_TM
    return pl.pallas_call(
        _combine_body,
        grid=(B, E + nfin),
        in_specs=[
            pl.BlockSpec((1, 1, C), lambda b, s: (b * E + jnp.minimum(s, E - 1), 0, 0), memory_space=pltpu.SMEM),
            pl.BlockSpec((1, C * ROW_CHUNKS, LANES), lambda b, s: (jnp.minimum(s, E - 1), b, 0)),
            pl.BlockSpec((1, CB_TM * ROW_CHUNKS, LANES), lambda b, s: (b, jnp.maximum(s - E, 0), 0)),
            pl.BlockSpec((ROW_CHUNKS, LANES), lambda b, s: (0, 0)),
        ],
        out_specs=pl.BlockSpec((1, CB_TM, D_MODEL), lambda b, s: (b, jnp.maximum(s - E, 0), 0)),
        out_shape=jax.ShapeDtypeStruct((B, S, D_MODEL), F32),
        scratch_shapes=[pltpu.VMEM((SR, LANES), F32), pltpu.VMEM((CB_TM * ROW_CHUNKS, LANES), F32)],
        compiler_params=_cparams(("arbitrary", "arbitrary")),
        name="combine",
    )(idx.reshape(B * E, 1, C), ye, x2, g_f)


def kernel(x, norm1_g, w_in, lam_q1, lam_k1, lam_q2, lam_k2, subln_g, w_out, rel_bias,
           norm2_g, w_router, w_gate, w_up, w_down, norm_f_g):
    B, S, _ = x.shape
    C = EC_K * S // N_EXPERTS
    e_bias = _diff_bias(rel_bias)
    m_bias = _dil_bias(rel_bias)
    p, p4, p16 = _in_proj(x, norm1_g, w_in[0].astype(BF16))
    oa = _diff_attn(p, e_bias, lam_q1, lam_k1, lam_q2, lam_k2, subln_g)
    od = _dil_attn(p, p4, p16, m_bias)
    x2, h2, aff = _out_proj(oa, od, x, w_out[0].astype(BF16), norm2_g, w_router[0].T)
    idx, gate = _route(aff)
    xe = _gather(idx, h2)
    gate_col = gate.transpose(1, 0, 2).reshape(N_EXPERTS, B * C, 1)
    ye = _ffn(xe, w_gate[0].astype(BF16), w_up[0].astype(BF16), w_down[0].astype(BF16), gate_col)
    return _combine(idx, ye, x2, norm_f_g.reshape(ROW_CHUNKS, LANES))
```

```python
import functools
import math

import jax
import jax.numpy as jnp
from jax import lax
from jax.experimental import pallas as pl
from jax.experimental.pallas import tpu as pltpu

F32 = jnp.float32
BF16 = jnp.bfloat16
I32 = jnp.int32

D_MODEL = 1024
HEAD_DIM = 64
N_DIFF_HEADS = 4
N_DIL_HEADS = 8
N_DIL_PAIRS = N_DIL_HEADS // 2
DIFF_WIDTH = 512
DIL_WIDTH = 512
N_IN = 3 * DIFF_WIDTH + 3 * DIL_WIDTH
DIL_PATTERNS = ((128, 1), (512, 4), (2048, 16))
N_BUCKETS = 32
REL_MAX_DIST = 1024
N_EXPERTS = 16
EC_K = 2
D_FF = 2816
NORM_EPS = 1e-6
SUBLN_EPS = 1e-5
NEG = -1e30
LAMBDA_INIT = 0.8 - 0.6 * math.exp(0.0)

LANES = 128
SUBLANES = 8
ROW_CHUNKS = D_MODEL // LANES

DA_TQ = 128
DA_TK = 512
DA_GROUP = 4
DA_DMIN = -1152
DA_DMAX = 768
DA_WTOT = DA_DMAX - DA_DMIN + DA_TK

DL_BQ = 128
DL_HALF = 64
DL_WIN = DL_BQ + 2 * DL_HALF
DL_GROUP = 4

LOG2E = math.log2(math.e)

VMEM_LIMIT = 56 * 1024 * 1024


def _cparams(sem):
    return pltpu.CompilerParams(dimension_semantics=sem, vmem_limit_bytes=VMEM_LIMIT)


def _t5_bucket(rel):
    half = N_BUCKETS // 2
    max_exact = half // 2
    n = jnp.abs(rel)
    nf = jnp.maximum(n, 1).astype(F32)
    large = max_exact + (jnp.log(nf / max_exact) / math.log(REL_MAX_DIST / max_exact)
                         * (half - max_exact)).astype(I32)
    large = jnp.minimum(large, half - 1)
    return jnp.where(rel > 0, half, 0) + jnp.where(n < max_exact, n, large)


def _bias_lookup(rel, tab_ref, col):
    bucket = _t5_bucket(rel)
    acc = jnp.zeros(rel.shape, F32)
    for j in range(N_BUCKETS):
        acc = jnp.where(bucket == j, tab_ref[j, col], acc)
    return acc


def _diff_bias_body(tab_ref, e_ref):
    h = pl.program_id(0)

    def chunk(cc, _):
        c0 = pl.multiple_of(cc * LANES, LANES)
        i = lax.broadcasted_iota(I32, (DA_TQ, LANES), 0)
        c = lax.broadcasted_iota(I32, (DA_TQ, LANES), 1) + c0
        e_ref[0, :, pl.ds(c0, LANES)] = _bias_lookup(c - i + DA_DMIN, tab_ref, h) * LOG2E
        return 0

    lax.fori_loop(0, DA_WTOT // LANES, chunk, 0)


def _diff_bias(rel_bias):
    return pl.pallas_call(
        _diff_bias_body,
        grid=(N_DIFF_HEADS,),
        in_specs=[pl.BlockSpec(memory_space=pltpu.SMEM)],
        out_specs=pl.BlockSpec((1, DA_TQ, DA_WTOT), lambda h: (h, 0, 0)),
        out_shape=jax.ShapeDtypeStruct((N_DIFF_HEADS, DA_TQ, DA_WTOT), F32),
        compiler_params=_cparams(("arbitrary",)),
        name="diff_bias",
    )(rel_bias)


def _dil_bias_body(tab_ref, m_ref):
    p = pl.program_id(0)
    h = pl.program_id(1)
    dil = jnp.where(p == 0, DIL_PATTERNS[0][1], jnp.where(p == 1, DIL_PATTERNS[1][1], DIL_PATTERNS[2][1]))
    i = lax.broadcasted_iota(I32, (DL_BQ, DL_WIN), 0)
    c = lax.broadcasted_iota(I32, (DL_BQ, DL_WIN), 1)
    off = c - DL_HALF - i
    bias = _bias_lookup(off * dil, tab_ref, N_DIFF_HEADS + h) * LOG2E
    band = jnp.abs(off) <= DL_HALF
    m_ref[0, 0, 0] = jnp.where(band & (c >= DL_HALF), bias, NEG)
    m_ref[0, 0, 1] = jnp.where(band, bias, NEG)
    m_ref[0, 0, 2] = jnp.where(band & (c < DL_HALF + DL_BQ), bias, NEG)


def _dil_bias(rel_bias):
    n_pat = len(DIL_PATTERNS)
    return pl.pallas_call(
        _dil_bias_body,
        grid=(n_pat, N_DIL_HEADS),
        in_specs=[pl.BlockSpec(memory_space=pltpu.SMEM)],
        out_specs=pl.BlockSpec((1, 1, 3, DL_BQ, DL_WIN), lambda p, h: (p, h, 0, 0, 0)),
        out_shape=jax.ShapeDtypeStruct((n_pat, N_DIL_HEADS, 3, DL_BQ, DL_WIN), F32),
        compiler_params=_cparams(("arbitrary", "arbitrary")),
        name="dil_bias",
    )(rel_bias)


IP_TM = 512
IP_TN = 256
N_COLBLK = N_IN // LANES
N_DIL_COLBLK = 3 * DIL_WIDTH // LANES


def _in_proj_body(x_ref, g_ref, w_ref, p_ref, p4_ref, p16_ref, scr_ref):
    x = x_ref[0]
    ms = jnp.mean(x * x, axis=-1, keepdims=True)
    y = (x * lax.rsqrt(ms + NORM_EPS) * g_ref[...]).astype(BF16)
    q_scale = HEAD_DIM ** -0.5 * LOG2E
    for n in range(N_IN // IP_TN):
        c0 = n * IP_TN
        r = jnp.dot(y, w_ref[:, c0:c0 + IP_TN], preferred_element_type=F32)
        is_q = (c0 < DIFF_WIDTH) or (3 * DIFF_WIDTH <= c0 < 3 * DIFF_WIDTH + DIL_WIDTH)
        if is_q:
            r = r * q_scale
        for k in range(IP_TN // LANES):
            p_ref[0, 2 * n + k] = r[:, k * LANES:(k + 1) * LANES].astype(BF16)
        if c0 >= 3 * DIFF_WIDTH:
            for k in range(IP_TN // LANES):
                scr_ref[(c0 - 3 * DIFF_WIDTH) // LANES + k] = r[:, k * LANES:(k + 1) * LANES]
    for c in range(N_DIL_COLBLK):
        for r_ in range(4):
            p4_ref[0, c, r_] = scr_ref[c, pl.ds(r_, IP_TM // 4, stride=4), :].astype(BF16)
        for r_ in range(16):
            p16_ref[0, c, r_] = scr_ref[c, pl.ds(r_, IP_TM // 16, stride=16), :].astype(BF16)


def _in_proj(x, g, w_bf16):
    B, S, _ = x.shape
    nt = S // IP_TM
    return pl.pallas_call(
        _in_proj_body,
        grid=(B, nt),
        in_specs=[
            pl.BlockSpec((1, IP_TM, D_MODEL), lambda b, t: (b, t, 0)),
            pl.BlockSpec((1, D_MODEL), lambda b, t: (0, 0)),
            pl.BlockSpec((D_MODEL, N_IN), lambda b, t: (0, 0)),
        ],
        out_specs=[
            pl.BlockSpec((1, N_COLBLK, IP_TM, LANES), lambda b, t: (b, 0, t, 0)),
            pl.BlockSpec((1, N_DIL_COLBLK, 4, IP_TM // 4, LANES), lambda b, t: (b, 0, 0, t, 0)),
            pl.BlockSpec((1, N_DIL_COLBLK, 16, IP_TM // 16, LANES), lambda b, t: (b, 0, 0, t, 0)),
        ],
        out_shape=[
            jax.ShapeDtypeStruct((B, N_COLBLK, S, LANES), BF16),
            jax.ShapeDtypeStruct((B, N_DIL_COLBLK, 4, S // 4, LANES), BF16),
            jax.ShapeDtypeStruct((B, N_DIL_COLBLK, 16, S // 16, LANES), BF16),
        ],
        scratch_shapes=[pltpu.VMEM((N_DIL_COLBLK, IP_TM, LANES), F32)],
        compiler_params=_cparams(("arbitrary", "arbitrary")),
        name="in_proj",
    )(x, g, w_bf16)


_NT_DIMS = (((1,), (1,)), ((), ()))


def _diff_attn_body(q_ref, k_ref, v_ref, e_ref, lq1_ref, lk1_ref, lq2_ref, lk2_ref, sg_ref, o_ref,
                    kx_ref, vx_ref, p_ref):
    S = q_ref.shape[2]
    TQ, TK = DA_TQ, DA_TK
    lam = (jnp.exp(jnp.sum(lq1_ref[...] * lk1_ref[...], axis=-1, keepdims=True))
           - jnp.exp(jnp.sum(lq2_ref[...] * lk2_ref[...], axis=-1, keepdims=True)) + LAMBDA_INIT)
    lane_q = lax.broadcasted_iota(I32, (TQ, LANES), 1)
    lo = lane_q < HEAD_DIM
    lane2 = lax.broadcasted_iota(I32, (2 * TQ, LANES), 1)

    def fill(i, _):
        r0 = pl.multiple_of(i * 512, 512)
        e0 = jnp.where(lax.broadcasted_iota(I32, (512, LANES), 1) == 0, 1.0, 0.0).astype(BF16)
        kx_ref[pl.ds(r0, 512), 0:LANES] = k_ref[0, 0, pl.ds(r0, 512), :]
        kx_ref[pl.ds(r0, 512), LANES:2 * LANES] = e0
        vx_ref[pl.ds(r0, 512), 0:LANES] = v_ref[0, 0, pl.ds(r0, 512), :]
        vx_ref[pl.ds(r0, 512), LANES:2 * LANES] = e0
        return 0

    lax.fori_loop(0, S // 512, fill, 0)

    def q_tile(q0, exact_shift, slot=0):
        ps_ref = p_ref.at[slot]
        q = q_ref[0, 0, pl.ds(q0, TQ), :]
        zero = jnp.zeros_like(q)
        qa = jnp.concatenate([jnp.where(lo, q, zero), jnp.where(lo, zero, q)], axis=0)

        def bias_tile(k0):
            off = pl.multiple_of(jnp.clip(k0 - q0, DA_DMIN, DA_DMAX) - DA_DMIN, LANES)
            return e_ref[0, :, pl.ds(off, TK)]

        def tile_max(k0, pm):
            s = lax.dot_general(qa, k_ref[0, 0, pl.ds(k0, TK), :], _NT_DIMS, preferred_element_type=F32)
            bias = bias_tile(k0)
            out = []
            for half in range(2):
                sb = s[half * TQ:(half + 1) * TQ] + bias
                t = pm[half]
                for j in range(TK // LANES):
                    t = jnp.maximum(t, sb[:, j * LANES:(j + 1) * LANES])
                out.append(t)
            return tuple(out)

        neg = jnp.full((TQ, LANES), -jnp.inf, F32)
        if exact_shift:
            pm = lax.fori_loop(0, S // TK, lambda ki, pm: tile_max(pl.multiple_of(ki * TK, TK), pm), (neg, neg))
        else:
            pm = tile_max(0, (neg, neg))
        m = jnp.max(jnp.concatenate(pm, axis=0), axis=-1, keepdims=True)
        mcol = jnp.where(lane2 == 0, -m, 0.0).astype(BF16)
        qx = jnp.concatenate([qa, mcol], axis=1)

        for ki in range(S // TK):
            k0 = ki * TK
            s = lax.dot_general(qx, kx_ref[k0:k0 + TK, :], _NT_DIMS, preferred_element_type=F32)
            bias = bias_tile(k0)
            for half in range(2):
                pr = jnp.exp2(s[half * TQ:(half + 1) * TQ] + bias).astype(BF16)
                ps_ref[half * TQ:(half + 1) * TQ, k0:k0 + TK] = pr

        acc = jnp.dot(ps_ref[...], vx_ref[...], preferred_element_type=F32)
        o1 = acc[0:TQ, 0:LANES] / acc[0:TQ, LANES:LANES + 1]
        o2 = acc[TQ:2 * TQ, 0:LANES] / acc[TQ:2 * TQ, LANES:LANES + 1]
        o = o1 - lam * o2
        ms = jnp.mean(o * o, axis=-1, keepdims=True)
        o = o * lax.rsqrt(ms + SUBLN_EPS) * sg_ref[...] * (1.0 - LAMBDA_INIT)
        o_ref[0, 0, pl.ds(q0, TQ), :] = o.astype(BF16)
        return jnp.where(jnp.isfinite(o), 0.0, 1.0)

    def fast_loop(qi, bad):
        for u in range(DA_GROUP):
            q0 = pl.multiple_of((qi * DA_GROUP + u) * TQ, TQ)
            bad = jnp.maximum(bad, q_tile(q0, exact_shift=False, slot=u))
        return bad

    bad = lax.fori_loop(0, S // (TQ * DA_GROUP), fast_loop, jnp.zeros((TQ, LANES), F32))

    @pl.when(jnp.max(bad) > 0.0)
    def _():
        def exact_loop(qi, _):
            q_tile(pl.multiple_of(qi * TQ, TQ), exact_shift=True)
            return 0
        lax.fori_loop(0, S // TQ, exact_loop, 0)


def _diff_attn(p, e_bias, lq1, lk1, lq2, lk2, subln_g):
    B, _, S, _ = p.shape
    H = N_DIFF_HEADS
    vec = lambda n: pl.BlockSpec((1, n), lambda b, h: (0, 0))
    return pl.pallas_call(
        _diff_attn_body,
        grid=(B, H),
        in_specs=[
            pl.BlockSpec((1, 1, S, LANES), lambda b, h: (b, h, 0, 0)),
            pl.BlockSpec((1, 1, S, LANES), lambda b, h: (b, H + h, 0, 0)),
            pl.BlockSpec((1, 1, S, LANES), lambda b, h: (b, 2 * H + h, 0, 0)),
            pl.BlockSpec((1, DA_TQ, DA_WTOT), lambda b, h: (h, 0, 0)),
            vec(HEAD_DIM), vec(HEAD_DIM), vec(HEAD_DIM), vec(HEAD_DIM), vec(2 * HEAD_DIM),
        ],
        out_specs=pl.BlockSpec((1, 1, S, LANES), lambda b, h: (b, h, 0, 0)),
        out_shape=jax.ShapeDtypeStruct((B, H, S, LANES), BF16),
        scratch_shapes=[
            pltpu.VMEM((S, 2 * LANES), BF16),
            pltpu.VMEM((S, 2 * LANES), BF16),
            pltpu.VMEM((DA_GROUP, 2 * DA_TQ, S), BF16),
        ],
        compiler_params=_cparams(("arbitrary", "arbitrary")),
        name="diff_attn",
    )(p, p, p, e_bias, lq1, lk1, lq2, lk2, subln_g)


def _dil_attn_body(q1_ref, k1_ref, v1_ref, q4_ref, k4_ref, v4_ref, q16_ref, k16_ref, v16_ref, m_ref,
                   o_ref, kpad_ref, vpad_ref, oacc_ref, lse_ref):
    S = q1_ref.shape[2]
    lane = lax.broadcasted_iota(I32, (DL_BQ, LANES), 1)
    lo = lane < HEAD_DIM
    zpad = jnp.zeros((DL_HALF, LANES), BF16)

    def run_pattern(p, dil, get_q, get_k, get_v):
        L = S // dil
        nblk = L // DL_BQ
        LP = L + 2 * DL_HALF

        def fill(r, _):
            base = pl.multiple_of(r * LP, DL_HALF)
            kpad_ref[pl.ds(base, DL_HALF), :] = zpad
            vpad_ref[pl.ds(base, DL_HALF), :] = zpad
            kpad_ref[pl.ds(base + DL_HALF, L), :] = get_k(r)[...]
            vpad_ref[pl.ds(base + DL_HALF, L), :] = get_v(r)[...]
            kpad_ref[pl.ds(base + DL_HALF + L, DL_HALF), :] = zpad
            vpad_ref[pl.ds(base + DL_HALF + L, DL_HALF), :] = zpad
            return 0

        lax.fori_loop(0, dil, fill, 0)

        def group(gi, _):
            for u in range(DL_GROUP):
                n = gi * DL_GROUP + u
                r = n // nblk
                blk = n % nblk
                q0 = pl.multiple_of(blk * DL_BQ, DL_BQ)
                w0 = pl.multiple_of(r * LP + q0, DL_HALF)
                q = get_q(r)[pl.ds(q0, DL_BQ), :]
                kw = kpad_ref[pl.ds(w0, DL_WIN), :]
                vw = vpad_ref[pl.ds(w0, DL_WIN), :]
                edge = jnp.where(blk == 0, 0, jnp.where(blk == nblk - 1, 2, 1))
                zero = jnp.zeros_like(q)
                outs = []
                for hh in range(2):
                    qm = jnp.where(lo, q, zero) if hh == 0 else jnp.where(lo, zero, q)
                    s = lax.dot_general(qm, kw, _NT_DIMS, preferred_element_type=F32) + m_ref[p, hh, edge]
                    m = jnp.max(s, axis=-1, keepdims=True)
                    e = jnp.exp2(s - m)
                    l = jnp.sum(e, axis=-1, keepdims=True)
                    o = jnp.dot(e.astype(BF16), vw, preferred_element_type=F32) / l
                    outs.append((o, m + jnp.log2(l)))
                o = jnp.where(lo, outs[0][0], outs[1][0])
                lse = jnp.where(lo, outs[0][1], outs[1][1])
                if dil == 1:
                    oacc_ref[p, pl.ds(q0, DL_BQ), :] = o
                    lse_ref[p, pl.ds(q0, DL_BQ), :] = lse
                else:
                    oacc_ref[p, pl.ds(q0 * dil + r, DL_BQ, stride=dil), :] = o
                    lse_ref[p, pl.ds(q0 * dil + r, DL_BQ, stride=dil), :] = lse
            return 0

        lax.fori_loop(0, dil * nblk // DL_GROUP, group, 0)

    run_pattern(0, 1, lambda r: q1_ref.at[0, 0], lambda r: k1_ref.at[0, 0], lambda r: v1_ref.at[0, 0])
    run_pattern(1, 4, lambda r: q4_ref.at[0, 0, r], lambda r: k4_ref.at[0, 0, r], lambda r: v4_ref.at[0, 0, r])
    run_pattern(2, 16, lambda r: q16_ref.at[0, 0, r], lambda r: k16_ref.at[0, 0, r], lambda r: v16_ref.at[0, 0, r])

    MIX = 256

    def mix(i, _):
        r0 = pl.multiple_of(i * MIX, MIX)
        ls = [lse_ref[p, pl.ds(r0, MIX), :] for p in range(3)]
        mx = jnp.maximum(jnp.maximum(ls[0], ls[1]), ls[2])
        ws = [jnp.exp2(l - mx) for l in ls]
        den = ws[0] + ws[1] + ws[2]
        num = ws[0] * oacc_ref[0, pl.ds(r0, MIX), :]
        num += ws[1] * oacc_ref[1, pl.ds(r0, MIX), :]
        num += ws[2] * oacc_ref[2, pl.ds(r0, MIX), :]
        o_ref[0, 0, pl.ds(r0, MIX), :] = (num / den).astype(BF16)
        return 0

    lax.fori_loop(0, S // MIX, mix, 0)


def _dil_attn(p, p4, p16, m_bias):
    B, _, S, _ = p.shape
    NP = N_DIL_PAIRS
    base = 3 * N_DIFF_HEADS
    nat = lambda o: pl.BlockSpec((1, 1, S, LANES), lambda b, h: (b, base + o * NP + h, 0, 0))
    res = lambda o, d: pl.BlockSpec((1, 1, d, S // d, LANES), lambda b, h: (b, o * NP + h, 0, 0, 0))
    return pl.pallas_call(
        _dil_attn_body,
        grid=(B, NP),
        in_specs=[nat(0), nat(1), nat(2), res(0, 4), res(1, 4), res(2, 4), res(0, 16), res(1, 16), res(2, 16),
                  pl.BlockSpec((3, 2, 3, DL_BQ, DL_WIN), lambda b, h: (0, h, 0, 0, 0))],
        out_specs=pl.BlockSpec((1, 1, S, LANES), lambda b, h: (b, h, 0, 0)),
        out_shape=jax.ShapeDtypeStruct((B, NP, S, LANES), BF16),
        scratch_shapes=[
            pltpu.VMEM((S + 2 * DL_HALF * DIL_PATTERNS[-1][1], LANES), BF16),
            pltpu.VMEM((S + 2 * DL_HALF * DIL_PATTERNS[-1][1], LANES), BF16),
            pltpu.VMEM((3, S, LANES), F32),
            pltpu.VMEM((3, S, LANES), F32),
        ],
        compiler_params=_cparams(("arbitrary", "arbitrary")),
        name="dil_attn",
    )(p, p, p, p4, p4, p4, p16, p16, p16, m_bias)


OP_TM = 512


def _store_token_major(ref, val):
    rows = val.shape[0]
    for j in range(ROW_CHUNKS):
        ref[0, pl.ds(j, rows, stride=ROW_CHUNKS), :] = val[:, j * LANES:(j + 1) * LANES]


def _out_proj_body(oa_ref, od_ref, x_ref, w_ref, g_ref, wr_ref, x2_ref, h2_ref, aff_ref):
    pieces = [oa_ref[0, i] for i in range(N_DIFF_HEADS)] + [od_ref[0, i] for i in range(N_DIL_PAIRS)]
    a = jnp.concatenate(pieces, axis=-1)
    x2 = x_ref[0] + jnp.dot(a, w_ref[...], preferred_element_type=F32)
    ms = jnp.mean(x2 * x2, axis=-1, keepdims=True)
    h = x2 * lax.rsqrt(ms + NORM_EPS) * g_ref[...]
    logits = lax.dot_general(wr_ref[...], h, _NT_DIMS, preferred_element_type=F32,
                             precision=lax.Precision.HIGHEST)
    mx = jnp.max(logits, axis=0, keepdims=True)
    ex = jnp.exp(logits - mx)
    aff_ref[0] = ex / jnp.sum(ex, axis=0, keepdims=True)
    _store_token_major(x2_ref, x2)
    _store_token_major(h2_ref, h)


def _out_proj(oa, od, x, w_bf16, g, wr_t):
    B, S, _ = x.shape
    nt = S // OP_TM
    attn = lambda n: pl.BlockSpec((1, n, OP_TM, LANES), lambda b, t: (b, 0, t, 0))
    tok_major = pl.BlockSpec((1, OP_TM * ROW_CHUNKS, LANES), lambda b, t: (b, t, 0))
    return pl.pallas_call(
        _out_proj_body,
        grid=(B, nt),
        in_specs=[
            attn(N_DIFF_HEADS), attn(N_DIL_PAIRS),
            pl.BlockSpec((1, OP_TM, D_MODEL), lambda b, t: (b, t, 0)),
            pl.BlockSpec((D_MODEL, D_MODEL), lambda b, t: (0, 0)),
            pl.BlockSpec((1, D_MODEL), lambda b, t: (0, 0)),
            pl.BlockSpec((N_EXPERTS, D_MODEL), lambda b, t: (0, 0)),
        ],
        out_specs=[tok_major, tok_major, pl.BlockSpec((1, N_EXPERTS, OP_TM), lambda b, t: (b, 0, t))],
        out_shape=[
            jax.ShapeDtypeStruct((B, S * ROW_CHUNKS, LANES), F32),
            jax.ShapeDtypeStruct((B, S * ROW_CHUNKS, LANES), F32),
            jax.ShapeDtypeStruct((B, N_EXPERTS, S), F32),
        ],
        compiler_params=_cparams(("arbitrary", "arbitrary")),
        name="out_proj",
    )(oa, od, x, w_bf16, g, wr_t)


RT_GROUP = 2


def _route_body(aff_ref, idx_ref, xh_ref, xl_ref, st_ref, en_ref):
    E, S = aff_ref.shape[1], aff_ref.shape[2]
    C = idx_ref.shape[2]
    n_chunks = S // LANES
    a = aff_ref[0]
    bits = pltpu.bitcast(a, I32)
    tau = jnp.zeros((E, 1), I32)
    for bit in range(30, -1, -1):
        cand = tau | (1 << bit)
        cnt = jnp.sum(jnp.where(bits >= cand, 1.0, 0.0).astype(F32), axis=1, keepdims=True)
        tau = jnp.where(cnt >= C, cand, tau)
    gt = bits > tau
    eq = bits == tau
    need = C - jnp.sum(jnp.where(gt, 1.0, 0.0).astype(F32), axis=1, keepdims=True)

    tri = (lax.broadcasted_iota(I32, (LANES, LANES), 0) <= lax.broadcasted_iota(I32, (LANES, LANES), 1)).astype(BF16)

    def chunk_cumsum(mj, carry):
        return jnp.dot(mj.astype(BF16), tri, preferred_element_type=F32) + carry

    sel_chunks = []
    carry = jnp.zeros((E, 1), F32)
    for j in range(n_chunks):
        cs = slice(j * LANES, (j + 1) * LANES)
        eq_j = jnp.where(eq[:, cs], 1.0, 0.0).astype(F32)
        inc = chunk_cumsum(eq_j, carry)
        carry = inc[:, LANES - 1:LANES]
        take_tie = jnp.where(inc - eq_j < need, eq_j, 0.0)
        sel_chunks.append(jnp.where(gt[:, cs], 1.0, take_tie))

    lane_e = lax.broadcasted_iota(I32, (E, LANES), 1)
    st = jnp.full((E, LANES), float(C), F32)
    en = jnp.full((E, LANES), float(C), F32)
    carry = jnp.zeros((E, 1), F32)
    for j in range(n_chunks):
        sel_j = sel_chunks[j]
        inc = chunk_cumsum(sel_j, carry)
        posm = jnp.where(sel_j > 0.0, inc - sel_j, -1.0)
        hi = jnp.floor(posm * 0.5)
        xh_ref[pl.ds(j * E, E), :] = hi
        xl_ref[pl.ds(j * E, E), :] = posm - 2.0 * hi
        st = jnp.where(lane_e == j, carry, st)
        carry = inc[:, LANES - 1:LANES]
        en = jnp.where(lane_e == j, carry, en)
    pad_rows = (LANES - n_chunks) * E
    xh_ref[pl.ds(n_chunks * E, pad_rows), :] = jnp.zeros((pad_rows, LANES), F32)
    xl_ref[pl.ds(n_chunks * E, pad_rows), :] = jnp.zeros((pad_rows, LANES), F32)
    st_ref[...] = st
    en_ref[...] = en

    c_iota = lax.broadcasted_iota(I32, (C, LANES), 0).astype(F32)
    lane_row = jnp.where(lax.broadcasted_iota(I32, (SUBLANES, LANES), 0) == 0,
                         lax.broadcasted_iota(I32, (SUBLANES, LANES), 1), 0).astype(F32).astype(BF16)

    def expert_group(g, _):
        for u in range(RT_GROUP):
            e = g * RT_GROUP + u
            st_e = st_ref[pl.ds(e, 1), :]
            en_e = en_ref[pl.ds(e, 1), :]
            in_chunk = jnp.where(st_e <= c_iota, jnp.where(c_iota < en_e, 1.0, 0.0), 0.0).astype(BF16)
            xh = xh_ref[pl.ds(e, LANES, stride=E), :].astype(BF16)
            xl = xl_ref[pl.ds(e, LANES, stride=E), :].astype(BF16)
            hl = jnp.dot(in_chunk, jnp.concatenate([xh, xl], axis=1), preferred_element_type=F32)
            slot_of = 2.0 * hl[:, 0:LANES] + hl[:, LANES:2 * LANES]
            hit = jnp.where(slot_of == c_iota, 1.0, 0.0).astype(BF16)
            chunk_id = lax.dot_general(lane_row, in_chunk, _NT_DIMS, preferred_element_type=F32)
            lane_id = lax.dot_general(lane_row, hit, _NT_DIMS, preferred_element_type=F32)
            idx_ref[0, pl.ds(e, 1), :] = (chunk_id[0:1, :] * float(LANES) + lane_id[0:1, :] + 0.5).astype(I32)
        return 0

    lax.fori_loop(0, E // RT_GROUP, expert_group, 0)


def _route(aff):
    B, E, S = aff.shape
    C = EC_K * S // E
    return pl.pallas_call(
        _route_body,
        grid=(B,),
        in_specs=[pl.BlockSpec((1, E, S), lambda b: (b, 0, 0))],
        out_specs=pl.BlockSpec((1, E, C), lambda b: (b, 0, 0)),
        out_shape=jax.ShapeDtypeStruct((B, E, C), I32),
        scratch_shapes=[pltpu.VMEM((LANES * E, LANES), F32), pltpu.VMEM((LANES * E, LANES), F32),
                        pltpu.VMEM((E, LANES), F32), pltpu.VMEM((E, LANES), F32)],
        compiler_params=_cparams(("arbitrary",)),
        name="route",
    )(aff)


GA_UNROLL = 8


def _gather_body(idx_ref, h_ref, xe_ref, g_ref):
    C = idx_ref.shape[2]

    def rows(i, _):
        for u in range(GA_UNROLL):
            r = i * GA_UNROLL + u
            t = idx_ref[0, 0, r]
            src = pl.multiple_of(t * ROW_CHUNKS, ROW_CHUNKS)
            dst = pl.multiple_of(r * ROW_CHUNKS, ROW_CHUNKS)
            g_ref[pl.ds(dst, ROW_CHUNKS), :] = h_ref[0, pl.ds(src, ROW_CHUNKS), :]
        return 0

    lax.fori_loop(0, C // GA_UNROLL, rows, 0)
    for j in range(ROW_CHUNKS):
        xe_ref[0, :, j * LANES:(j + 1) * LANES] = g_ref[pl.ds(j, C, stride=ROW_CHUNKS), :].astype(BF16)


def _gather(idx, h2):
    B, E, C = idx.shape
    SR = h2.shape[1]
    return pl.pallas_call(
        _gather_body,
        grid=(B, E),
        in_specs=[
            pl.BlockSpec((1, 1, C), lambda b, e: (b * E + e, 0, 0), memory_space=pltpu.SMEM),
            pl.BlockSpec((1, SR, LANES), lambda b, e: (b, 0, 0), pipeline_mode=pl.Buffered(1)),
        ],
        out_specs=pl.BlockSpec((1, C, D_MODEL), lambda b, e: (e, b, 0)),
        out_shape=jax.ShapeDtypeStruct((E, B * C, D_MODEL), BF16),
        scratch_shapes=[pltpu.VMEM((C * ROW_CHUNKS, LANES), F32)],
        compiler_params=_cparams(("arbitrary", "arbitrary")),
        name="gather",
    )(idx.reshape(B * E, 1, C), h2)


FF_TM = 512
FF_TF = 1408


def _ffn_body(x_ref, wg_ref, wu_ref, wd_ref, y_ref, acc_ref):
    f = pl.program_id(2)
    x = x_ref[0]
    hg = jnp.dot(x, wg_ref[0], preferred_element_type=F32)
    hu = jnp.dot(x, wu_ref[0], preferred_element_type=F32)
    he = (hg * jax.nn.sigmoid(hg) * hu).astype(BF16)
    part = jnp.dot(he, wd_ref[0], preferred_element_type=F32)

    @pl.when(f == 0)
    def _():
        acc_ref[...] = part

    @pl.when(f > 0)
    def _():
        acc_ref[...] += part

    @pl.when(f == pl.num_programs(2) - 1)
    def _():
        _store_token_major(y_ref, acc_ref[...])


def _ffn(xe, wg, wu, wd):
    E, M, _ = xe.shape
    nm = M // FF_TM
    nf = D_FF // FF_TF
    return pl.pallas_call(
        _ffn_body,
        grid=(E, nm, nf),
        in_specs=[
            pl.BlockSpec((1, FF_TM, D_MODEL), lambda e, m, f: (e, m, 0)),
            pl.BlockSpec((1, D_MODEL, FF_TF), lambda e, m, f: (e, 0, f)),
            pl.BlockSpec((1, D_MODEL, FF_TF), lambda e, m, f: (e, 0, f)),
            pl.BlockSpec((1, FF_TF, D_MODEL), lambda e, m, f: (e, f, 0)),
        ],
        out_specs=pl.BlockSpec((1, FF_TM * ROW_CHUNKS, LANES), lambda e, m, f: (e, m, 0)),
        out_shape=jax.ShapeDtypeStruct((E, M * ROW_CHUNKS, LANES), F32),
        scratch_shapes=[pltpu.VMEM((FF_TM, D_MODEL), F32)],
        compiler_params=_cparams(("arbitrary", "arbitrary", "arbitrary")),
        name="expert_ffn",
    )(xe, wg, wu, wd)


CB_UNROLL = 4
CB_TM = 512


def _combine_body(idx_ref, aff_ref, y_ref, x2_ref, g_ref, o_ref, acc_ref, n_ref):
    E = N_EXPERTS
    C = idx_ref.shape[2]
    s = pl.program_id(1)

    @pl.when(s == 0)
    def _():
        def zero(i, _):
            r0 = pl.multiple_of(i * 1024, 1024)
            acc_ref[pl.ds(r0, 1024), :] = jnp.zeros((1024, LANES), F32)
            return 0
        lax.fori_loop(0, acc_ref.shape[0] // 1024, zero, 0)

    @pl.when(s < E)
    def _():
        def rows(i, _):
            dsts, sums = [], []
            for u in range(CB_UNROLL):
                r = i * CB_UNROLL + u
                t = idx_ref[0, 0, r]
                gate = aff_ref[0, 0, t]
                dst = pl.multiple_of(t * ROW_CHUNKS, ROW_CHUNKS)
                src = pl.multiple_of(r * ROW_CHUNKS, ROW_CHUNKS)
                dsts.append(dst)
                sums.append(acc_ref[pl.ds(dst, ROW_CHUNKS), :] + gate * y_ref[0, pl.ds(src, ROW_CHUNKS), :])
            for dst, v in zip(dsts, sums):
                acc_ref[pl.ds(dst, ROW_CHUNKS), :] = v
            return 0
        lax.fori_loop(0, C // CB_UNROLL, rows, 0)

    @pl.when(s >= E)
    def _():
        r0 = pl.multiple_of((s - E) * (CB_TM * ROW_CHUNKS), CB_TM * ROW_CHUNKS)
        v = x2_ref[0] + acc_ref[pl.ds(r0, CB_TM * ROW_CHUNKS), :]
        v3 = v.reshape(CB_TM, ROW_CHUNKS, LANES)
        ss = jnp.sum(jnp.sum(v3 * v3, axis=2, keepdims=True), axis=1, keepdims=True)
        n3 = v3 * lax.rsqrt(ss * (1.0 / D_MODEL) + NORM_EPS) * g_ref[...][None]
        n_ref[...] = n3.reshape(CB_TM * ROW_CHUNKS, LANES)
        for j in range(ROW_CHUNKS):
            o_ref[0, :, j * LANES:(j + 1) * LANES] = n_ref[pl.ds(j, CB_TM, stride=ROW_CHUNKS), :]


def _combine(idx, aff, ye, x2, g_f):
    B, E, C = idx.shape
    SR = x2.shape[1]
    S = SR // ROW_CHUNKS
    nfin = S // CB_TM
    return pl.pallas_call(
        _combine_body,
        grid=(B, E + nfin),
        in_specs=[
            pl.BlockSpec((1, 1, C), lambda b, s: (b * E + jnp.minimum(s, E - 1), 0, 0), memory_space=pltpu.SMEM),
            pl.BlockSpec((1, 1, S), lambda b, s: (b * E + jnp.minimum(s, E - 1), 0, 0), memory_space=pltpu.SMEM),
            pl.BlockSpec((1, C * ROW_CHUNKS, LANES), lambda b, s: (jnp.minimum(s, E - 1), b, 0)),
            pl.BlockSpec((1, CB_TM * ROW_CHUNKS, LANES), lambda b, s: (b, jnp.maximum(s - E, 0), 0)),
            pl.BlockSpec((ROW_CHUNKS, LANES), lambda b, s: (0, 0)),
        ],
        out_specs=pl.BlockSpec((1, CB_TM, D_MODEL), lambda b, s: (b, jnp.maximum(s - E, 0), 0)),
        out_shape=jax.ShapeDtypeStruct((B, S, D_MODEL), F32),
        scratch_shapes=[pltpu.VMEM((SR, LANES), F32), pltpu.VMEM((CB_TM * ROW_CHUNKS, LANES), F32)],
        compiler_params=_cparams(("arbitrary", "arbitrary")),
        name="combine",
    )(idx.reshape(B * E, 1, C), aff.reshape(B * E, 1, S), ye, x2, g_f)


def kernel(x, norm1_g, w_in, lam_q1, lam_k1, lam_q2, lam_k2, subln_g, w_out, rel_bias,
           norm2_g, w_router, w_gate, w_up, w_down, norm_f_g):
    B, S, _ = x.shape
    C = EC_K * S // N_EXPERTS
    e_bias = _diff_bias(rel_bias)
    m_bias = _dil_bias(rel_bias)
    p, p4, p16 = _in_proj(x, norm1_g, w_in[0].astype(BF16))
    oa = _diff_attn(p, e_bias, lam_q1, lam_k1, lam_q2, lam_k2, subln_g)
    od = _dil_attn(p, p4, p16, m_bias)
    x2, h2, aff = _out_proj(oa, od, x, w_out[0].astype(BF16), norm2_g, w_router[0].T)
    idx = _route(aff)
    xe = _gather(idx, h2)
    ye = _ffn(xe, w_gate[0].astype(BF16), w_up[0].astype(BF16), w_down[0].astype(BF16))
    return _combine(idx, aff, ye, x2, norm_f_g.reshape(ROW_CHUNKS, LANES))
```

```python
import functools
import math

import jax
import jax.numpy as jnp
from jax import lax
from jax.experimental import pallas as pl
from jax.experimental.pallas import tpu as pltpu

F32 = jnp.float32
BF16 = jnp.bfloat16
I32 = jnp.int32

D_MODEL = 1024
HEAD_DIM = 64
N_DIFF_HEADS = 4
N_DIL_HEADS = 8
N_DIL_PAIRS = N_DIL_HEADS // 2
DIFF_WIDTH = 512
DIL_WIDTH = 512
N_IN = 3 * DIFF_WIDTH + 3 * DIL_WIDTH
DIL_PATTERNS = ((128, 1), (512, 4), (2048, 16))
N_BUCKETS = 32
REL_MAX_DIST = 1024
N_EXPERTS = 16
EC_K = 2
D_FF = 2816
NORM_EPS = 1e-6
SUBLN_EPS = 1e-5
NEG = -1e30
LAMBDA_INIT = 0.8 - 0.6 * math.exp(0.0)

LANES = 128
SUBLANES = 8
ROW_CHUNKS = D_MODEL // LANES

DA_TQ = 128
DA_TK = 512
DA_GROUP = 4
DA_DMIN = -1152
DA_DMAX = 768
DA_WTOT = DA_DMAX - DA_DMIN + DA_TK

DL_BQ = 128
DL_HALF = 64
DL_WIN = DL_BQ + 2 * DL_HALF
DL_GROUP = 8

LOG2E = math.log2(math.e)

VMEM_LIMIT = 56 * 1024 * 1024


def _cparams(sem):
    return pltpu.CompilerParams(dimension_semantics=sem, vmem_limit_bytes=VMEM_LIMIT)


def _t5_bucket(rel):
    half = N_BUCKETS // 2
    max_exact = half // 2
    n = jnp.abs(rel)
    nf = jnp.maximum(n, 1).astype(F32)
    large = max_exact + (jnp.log(nf / max_exact) / math.log(REL_MAX_DIST / max_exact)
                         * (half - max_exact)).astype(I32)
    large = jnp.minimum(large, half - 1)
    return jnp.where(rel > 0, half, 0) + jnp.where(n < max_exact, n, large)


def _bias_lookup(rel, tab_ref, col):
    bucket = _t5_bucket(rel)
    acc = jnp.zeros(rel.shape, F32)
    for j in range(N_BUCKETS):
        acc = jnp.where(bucket == j, tab_ref[j, col], acc)
    return acc


def _diff_bias_body(tab_ref, e_ref):
    h = pl.program_id(0)

    def chunk(cc, _):
        c0 = pl.multiple_of(cc * LANES, LANES)
        i = lax.broadcasted_iota(I32, (DA_TQ, LANES), 0)
        c = lax.broadcasted_iota(I32, (DA_TQ, LANES), 1) + c0
        e_ref[0, :, pl.ds(c0, LANES)] = _bias_lookup(c - i + DA_DMIN, tab_ref, h) * LOG2E
        return 0

    lax.fori_loop(0, DA_WTOT // LANES, chunk, 0)


def _diff_bias(rel_bias):
    return pl.pallas_call(
        _diff_bias_body,
        grid=(N_DIFF_HEADS,),
        in_specs=[pl.BlockSpec(memory_space=pltpu.SMEM)],
        out_specs=pl.BlockSpec((1, DA_TQ, DA_WTOT), lambda h: (h, 0, 0)),
        out_shape=jax.ShapeDtypeStruct((N_DIFF_HEADS, DA_TQ, DA_WTOT), F32),
        compiler_params=_cparams(("arbitrary",)),
        name="diff_bias",
    )(rel_bias)


def _dil_bias_body(tab_ref, m_ref):
    p = pl.program_id(0)
    h = pl.program_id(1)
    dil = jnp.where(p == 0, DIL_PATTERNS[0][1], jnp.where(p == 1, DIL_PATTERNS[1][1], DIL_PATTERNS[2][1]))
    i = lax.broadcasted_iota(I32, (DL_BQ, DL_WIN), 0)
    c = lax.broadcasted_iota(I32, (DL_BQ, DL_WIN), 1)
    off = c - DL_HALF - i
    bias = _bias_lookup(off * dil, tab_ref, N_DIFF_HEADS + h) * LOG2E
    band = jnp.abs(off) <= DL_HALF
    m_ref[0, 0, 0] = jnp.where(band & (c >= DL_HALF), bias, NEG)
    m_ref[0, 0, 1] = jnp.where(band, bias, NEG)
    m_ref[0, 0, 2] = jnp.where(band & (c < DL_HALF + DL_BQ), bias, NEG)


def _dil_bias(rel_bias):
    n_pat = len(DIL_PATTERNS)
    return pl.pallas_call(
        _dil_bias_body,
        grid=(n_pat, N_DIL_HEADS),
        in_specs=[pl.BlockSpec(memory_space=pltpu.SMEM)],
        out_specs=pl.BlockSpec((1, 1, 3, DL_BQ, DL_WIN), lambda p, h: (p, h, 0, 0, 0)),
        out_shape=jax.ShapeDtypeStruct((n_pat, N_DIL_HEADS, 3, DL_BQ, DL_WIN), F32),
        compiler_params=_cparams(("arbitrary", "arbitrary")),
        name="dil_bias",
    )(rel_bias)


IP_TM = 512
IP_SUB = 256
IP_TN = 256
N_COLBLK = N_IN // LANES
N_DIL_COLBLK = 3 * DIL_WIDTH // LANES


def _in_proj_body(x_ref, g_ref, w_ref, p_ref, p4_ref, p16_ref, scr_ref):
    q_scale = HEAD_DIM ** -0.5 * LOG2E
    for sub in range(IP_TM // IP_SUB):
        r0 = sub * IP_SUB
        x = x_ref[0, r0:r0 + IP_SUB, :]
        ms = jnp.mean(x * x, axis=-1, keepdims=True)
        y = (x * lax.rsqrt(ms + NORM_EPS) * g_ref[...]).astype(BF16)
        for n in range(N_IN // IP_TN):
            c0 = n * IP_TN
            r = jnp.dot(y, w_ref[:, c0:c0 + IP_TN], preferred_element_type=F32)
            is_q = (c0 < DIFF_WIDTH) or (3 * DIFF_WIDTH <= c0 < 3 * DIFF_WIDTH + DIL_WIDTH)
            if is_q:
                r = r * q_scale
            for k in range(IP_TN // LANES):
                p_ref[0, 2 * n + k, r0:r0 + IP_SUB, :] = r[:, k * LANES:(k + 1) * LANES].astype(BF16)
            if c0 >= 3 * DIFF_WIDTH:
                for k in range(IP_TN // LANES):
                    scr_ref[(c0 - 3 * DIFF_WIDTH) // LANES + k, r0:r0 + IP_SUB, :] = r[:, k * LANES:(k + 1) * LANES]
        for c in range(N_DIL_COLBLK):
            for d, dst_ref in ((4, p4_ref), (16, p16_ref)):
                n_rows = IP_SUB // d
                for r_ in range(d):
                    dst_ref[0, c, r_, sub * n_rows:(sub + 1) * n_rows, :] = (
                        scr_ref[c, pl.ds(r0 + r_, n_rows, stride=d), :].astype(BF16))


def _in_proj(x, g, w_bf16):
    B, S, _ = x.shape
    nt = S // IP_TM
    return pl.pallas_call(
        _in_proj_body,
        grid=(B, nt),
        in_specs=[
            pl.BlockSpec((1, IP_TM, D_MODEL), lambda b, t: (b, t, 0)),
            pl.BlockSpec((1, D_MODEL), lambda b, t: (0, 0)),
            pl.BlockSpec((D_MODEL, N_IN), lambda b, t: (0, 0)),
        ],
        out_specs=[
            pl.BlockSpec((1, N_COLBLK, IP_TM, LANES), lambda b, t: (b, 0, t, 0)),
            pl.BlockSpec((1, N_DIL_COLBLK, 4, IP_TM // 4, LANES), lambda b, t: (b, 0, 0, t, 0)),
            pl.BlockSpec((1, N_DIL_COLBLK, 16, IP_TM // 16, LANES), lambda b, t: (b, 0, 0, t, 0)),
        ],
        out_shape=[
            jax.ShapeDtypeStruct((B, N_COLBLK, S, LANES), BF16),
            jax.ShapeDtypeStruct((B, N_DIL_COLBLK, 4, S // 4, LANES), BF16),
            jax.ShapeDtypeStruct((B, N_DIL_COLBLK, 16, S // 16, LANES), BF16),
        ],
        scratch_shapes=[pltpu.VMEM((N_DIL_COLBLK, IP_TM, LANES), F32)],
        compiler_params=_cparams(("arbitrary", "arbitrary")),
        name="in_proj",
    )(x, g, w_bf16)


_NT_DIMS = (((1,), (1,)), ((), ()))


def _diff_attn_body(q_ref, k_ref, v_ref, e_ref, lq1_ref, lk1_ref, lq2_ref, lk2_ref, sg_ref, o_ref,
                    kx_ref, vx_ref, p_ref):
    S = q_ref.shape[2]
    TQ, TK = DA_TQ, DA_TK
    lam = (jnp.exp(jnp.sum(lq1_ref[...] * lk1_ref[...], axis=-1, keepdims=True))
           - jnp.exp(jnp.sum(lq2_ref[...] * lk2_ref[...], axis=-1, keepdims=True)) + LAMBDA_INIT)
    lane_q = lax.broadcasted_iota(I32, (TQ, LANES), 1)
    lo = lane_q < HEAD_DIM
    lane2 = lax.broadcasted_iota(I32, (2 * TQ, LANES), 1)

    def fill(i, _):
        r0 = pl.multiple_of(i * 512, 512)
        e0 = jnp.where(lax.broadcasted_iota(I32, (512, LANES), 1) == 0, 1.0, 0.0).astype(BF16)
        kx_ref[pl.ds(r0, 512), 0:LANES] = k_ref[0, 0, pl.ds(r0, 512), :]
        kx_ref[pl.ds(r0, 512), LANES:2 * LANES] = e0
        vx_ref[pl.ds(r0, 512), 0:LANES] = v_ref[0, 0, pl.ds(r0, 512), :]
        vx_ref[pl.ds(r0, 512), LANES:2 * LANES] = e0
        return 0

    lax.fori_loop(0, S // 512, fill, 0)

    def q_tile(q0, exact_shift, slot=0):
        ps_ref = p_ref.at[slot]
        q = q_ref[0, 0, pl.ds(q0, TQ), :]
        zero = jnp.zeros_like(q)
        qa = jnp.concatenate([jnp.where(lo, q, zero), jnp.where(lo, zero, q)], axis=0)

        def bias_tile(k0):
            off = pl.multiple_of(jnp.clip(k0 - q0, DA_DMIN, DA_DMAX) - DA_DMIN, LANES)
            return e_ref[0, :, pl.ds(off, TK)]

        def tile_max(k0, pm):
            s = lax.dot_general(qa, k_ref[0, 0, pl.ds(k0, TK), :], _NT_DIMS, preferred_element_type=F32)
            bias = bias_tile(k0)
            out = []
            for half in range(2):
                sb = s[half * TQ:(half + 1) * TQ] + bias
                t = pm[half]
                for j in range(TK // LANES):
                    t = jnp.maximum(t, sb[:, j * LANES:(j + 1) * LANES])
                out.append(t)
            return tuple(out)

        neg = jnp.full((TQ, LANES), -jnp.inf, F32)
        if exact_shift:
            pm = lax.fori_loop(0, S // TK, lambda ki, pm: tile_max(pl.multiple_of(ki * TK, TK), pm), (neg, neg))
        else:
            pm = tile_max(0, (neg, neg))
        m = jnp.max(jnp.concatenate(pm, axis=0), axis=-1, keepdims=True)
        mcol = jnp.where(lane2 == 0, -m, 0.0).astype(BF16)
        qx = jnp.concatenate([qa, mcol], axis=1)

        for ki in range(S // TK):
            k0 = ki * TK
            s = lax.dot_general(qx, kx_ref[k0:k0 + TK, :], _NT_DIMS, preferred_element_type=F32)
            bias = bias_tile(k0)
            for half in range(2):
                pr = jnp.exp2(s[half * TQ:(half + 1) * TQ] + bias).astype(BF16)
                ps_ref[half * TQ:(half + 1) * TQ, k0:k0 + TK] = pr

        acc = jnp.dot(ps_ref[...], vx_ref[...], preferred_element_type=F32)
        o1 = acc[0:TQ, 0:LANES] / acc[0:TQ, LANES:LANES + 1]
        o2 = acc[TQ:2 * TQ, 0:LANES] / acc[TQ:2 * TQ, LANES:LANES + 1]
        o = o1 - lam * o2
        ms = jnp.mean(o * o, axis=-1, keepdims=True)
        o = o * lax.rsqrt(ms + SUBLN_EPS) * sg_ref[...] * (1.0 - LAMBDA_INIT)
        o_ref[0, 0, pl.ds(q0, TQ), :] = o.astype(BF16)
        return jnp.where(jnp.isfinite(o), 0.0, 1.0)

    def fast_loop(qi, bad):
        for u in range(DA_GROUP):
            q0 = pl.multiple_of((qi * DA_GROUP + u) * TQ, TQ)
            bad = jnp.maximum(bad, q_tile(q0, exact_shift=False, slot=u))
        return bad

    bad = lax.fori_loop(0, S // (TQ * DA_GROUP), fast_loop, jnp.zeros((TQ, LANES), F32))

    @pl.when(jnp.max(bad) > 0.0)
    def _():
        def exact_loop(qi, _):
            q_tile(pl.multiple_of(qi * TQ, TQ), exact_shift=True)
            return 0
        lax.fori_loop(0, S // TQ, exact_loop, 0)


def _diff_attn(p, e_bias, lq1, lk1, lq2, lk2, subln_g):
    B, _, S, _ = p.shape
    H = N_DIFF_HEADS
    vec = lambda n: pl.BlockSpec((1, n), lambda b, h: (0, 0))
    return pl.pallas_call(
        _diff_attn_body,
        grid=(B, H),
        in_specs=[
            pl.BlockSpec((1, 1, S, LANES), lambda b, h: (b, h, 0, 0)),
            pl.BlockSpec((1, 1, S, LANES), lambda b, h: (b, H + h, 0, 0)),
            pl.BlockSpec((1, 1, S, LANES), lambda b, h: (b, 2 * H + h, 0, 0)),
            pl.BlockSpec((1, DA_TQ, DA_WTOT), lambda b, h: (h, 0, 0)),
            vec(HEAD_DIM), vec(HEAD_DIM), vec(HEAD_DIM), vec(HEAD_DIM), vec(2 * HEAD_DIM),
        ],
        out_specs=pl.BlockSpec((1, 1, S, LANES), lambda b, h: (b, h, 0, 0)),
        out_shape=jax.ShapeDtypeStruct((B, H, S, LANES), BF16),
        scratch_shapes=[
            pltpu.VMEM((S, 2 * LANES), BF16),
            pltpu.VMEM((S, 2 * LANES), BF16),
            pltpu.VMEM((DA_GROUP, 2 * DA_TQ, S), BF16),
        ],
        compiler_params=_cparams(("arbitrary", "arbitrary")),
        name="diff_attn",
    )(p, p, p, e_bias, lq1, lk1, lq2, lk2, subln_g)


def _dil_attn_body(q1_ref, k1_ref, v1_ref, q4_ref, k4_ref, v4_ref, q16_ref, k16_ref, v16_ref, m_ref,
                   o_ref, kpad_ref, vlo_ref, vhi_ref, oacc_ref, lse_ref):
    S = q1_ref.shape[2]
    lane = lax.broadcasted_iota(I32, (DL_BQ, LANES), 1)
    lo = lane < HEAD_DIM
    zpad = jnp.zeros((DL_HALF, LANES), BF16)

    def ones_fill(i, _):
        r0 = pl.multiple_of(i * 512, 512)
        lo_f = lax.broadcasted_iota(I32, (512, LANES), 1) < HEAD_DIM
        vlo_ref[pl.ds(r0, 512), LANES:2 * LANES] = jnp.where(lo_f, 1.0, 0.0).astype(BF16)
        vhi_ref[pl.ds(r0, 512), LANES:2 * LANES] = jnp.where(lo_f, 0.0, 1.0).astype(BF16)
        return 0

    lax.fori_loop(0, vlo_ref.shape[0] // 512, ones_fill, 0)

    def run_pattern(p, dil, get_q, get_k, get_v):
        L = S // dil
        nblk = L // DL_BQ
        LP = L + 2 * DL_HALF

        def fill(r, _):
            base = pl.multiple_of(r * LP, DL_HALF)
            for off in (0, DL_HALF + L):
                kpad_ref[pl.ds(base + off, DL_HALF), :] = zpad
                vlo_ref[pl.ds(base + off, DL_HALF), 0:LANES] = zpad
                vhi_ref[pl.ds(base + off, DL_HALF), 0:LANES] = zpad
            kpad_ref[pl.ds(base + DL_HALF, L), :] = get_k(r)[...]
            v = get_v(r)[...]
            lo_v = lax.broadcasted_iota(I32, v.shape, 1) < HEAD_DIM
            zero = jnp.zeros_like(v)
            vlo_ref[pl.ds(base + DL_HALF, L), 0:LANES] = jnp.where(lo_v, v, zero)
            vhi_ref[pl.ds(base + DL_HALF, L), 0:LANES] = jnp.where(lo_v, zero, v)
            return 0

        lax.fori_loop(0, dil, fill, 0)

        def group(gi, _):
            for u in range(DL_GROUP):
                n = gi * DL_GROUP + u
                r = n // nblk
                blk = n % nblk
                q0 = pl.multiple_of(blk * DL_BQ, DL_BQ)
                w0 = pl.multiple_of(r * LP + q0, DL_HALF)
                q = get_q(r)[pl.ds(q0, DL_BQ), :]
                kw = kpad_ref[pl.ds(w0, DL_WIN), :]
                vw = jnp.concatenate([vlo_ref[pl.ds(w0, DL_WIN), :], vhi_ref[pl.ds(w0, DL_WIN), :]], axis=0)
                edge = jnp.where(blk == 0, 0, jnp.where(blk == nblk - 1, 2, 1))
                zero = jnp.zeros_like(q)
                probs, maxes = [], []
                for hh in range(2):
                    qm = jnp.where(lo, q, zero) if hh == 0 else jnp.where(lo, zero, q)
                    s = lax.dot_general(qm, kw, _NT_DIMS, preferred_element_type=F32) + m_ref[p, hh, edge]
                    m = jnp.max(s, axis=-1, keepdims=True)
                    probs.append(jnp.exp2(s - m).astype(BF16))
                    maxes.append(m)
                both = jnp.dot(jnp.concatenate(probs, axis=1), vw, preferred_element_type=F32)
                l = both[:, LANES:2 * LANES]
                o = both[:, 0:LANES] / l
                lse = jnp.where(lo, maxes[0], maxes[1]) + jnp.log2(l)
                if dil == 1:
                    oacc_ref[p, pl.ds(q0, DL_BQ), :] = o
                    lse_ref[p, pl.ds(q0, DL_BQ), :] = lse
                else:
                    oacc_ref[p, pl.ds(q0 * dil + r, DL_BQ, stride=dil), :] = o
                    lse_ref[p, pl.ds(q0 * dil + r, DL_BQ, stride=dil), :] = lse
            return 0

        lax.fori_loop(0, dil * nblk // DL_GROUP, group, 0)

    run_pattern(0, 1, lambda r: q1_ref.at[0, 0], lambda r: k1_ref.at[0, 0], lambda r: v1_ref.at[0, 0])
    run_pattern(1, 4, lambda r: q4_ref.at[0, 0, r], lambda r: k4_ref.at[0, 0, r], lambda r: v4_ref.at[0, 0, r])
    run_pattern(2, 16, lambda r: q16_ref.at[0, 0, r], lambda r: k16_ref.at[0, 0, r], lambda r: v16_ref.at[0, 0, r])

    MIX = 256

    def mix(i, _):
        r0 = pl.multiple_of(i * MIX, MIX)
        ls = [lse_ref[p, pl.ds(r0, MIX), :] for p in range(3)]
        mx = jnp.maximum(jnp.maximum(ls[0], ls[1]), ls[2])
        ws = [jnp.exp2(l - mx) for l in ls]
        den = ws[0] + ws[1] + ws[2]
        num = ws[0] * oacc_ref[0, pl.ds(r0, MIX), :]
        num += ws[1] * oacc_ref[1, pl.ds(r0, MIX), :]
        num += ws[2] * oacc_ref[2, pl.ds(r0, MIX), :]
        o_ref[0, 0, pl.ds(r0, MIX), :] = (num / den).astype(BF16)
        return 0

    lax.fori_loop(0, S // MIX, mix, 0)


def _dil_attn(p, p4, p16, m_bias):
    B, _, S, _ = p.shape
    NP = N_DIL_PAIRS
    base = 3 * N_DIFF_HEADS
    nat = lambda o: pl.BlockSpec((1, 1, S, LANES), lambda b, h: (b, base + o * NP + h, 0, 0))
    res = lambda o, d: pl.BlockSpec((1, 1, d, S // d, LANES), lambda b, h: (b, o * NP + h, 0, 0, 0))
    return pl.pallas_call(
        _dil_attn_body,
        grid=(B, NP),
        in_specs=[nat(0), nat(1), nat(2), res(0, 4), res(1, 4), res(2, 4), res(0, 16), res(1, 16), res(2, 16),
                  pl.BlockSpec((3, 2, 3, DL_BQ, DL_WIN), lambda b, h: (0, h, 0, 0, 0))],
        out_specs=pl.BlockSpec((1, 1, S, LANES), lambda b, h: (b, h, 0, 0)),
        out_shape=jax.ShapeDtypeStruct((B, NP, S, LANES), BF16),
        scratch_shapes=[
            pltpu.VMEM((S + 2 * DL_HALF * DIL_PATTERNS[-1][1], LANES), BF16),
            pltpu.VMEM((S + 2 * DL_HALF * DIL_PATTERNS[-1][1], 2 * LANES), BF16),
            pltpu.VMEM((S + 2 * DL_HALF * DIL_PATTERNS[-1][1], 2 * LANES), BF16),
            pltpu.VMEM((3, S, LANES), F32),
            pltpu.VMEM((3, S, LANES), F32),
        ],
        compiler_params=_cparams(("arbitrary", "arbitrary")),
        name="dil_attn",
    )(p, p, p, p4, p4, p4, p16, p16, p16, m_bias)


OP_TM = 512


OP_SUB = 256


def _store_token_major(ref, val, row0=0):
    rows = val.shape[0]
    for j in range(ROW_CHUNKS):
        ref[0, pl.ds(row0 * ROW_CHUNKS + j, rows, stride=ROW_CHUNKS), :] = val[:, j * LANES:(j + 1) * LANES]


def _out_proj_body(oa_ref, od_ref, x_ref, w_ref, g_ref, wr_ref, x2_ref, h2_ref, aff_ref):
    for sub in range(OP_TM // OP_SUB):
        r0 = sub * OP_SUB
        pieces = ([oa_ref[0, i, r0:r0 + OP_SUB, :] for i in range(N_DIFF_HEADS)]
                  + [od_ref[0, i, r0:r0 + OP_SUB, :] for i in range(N_DIL_PAIRS)])
        a = jnp.concatenate(pieces, axis=-1)
        x2 = x_ref[0, r0:r0 + OP_SUB, :] + jnp.dot(a, w_ref[...], preferred_element_type=F32)
        ms = jnp.mean(x2 * x2, axis=-1, keepdims=True)
        h = x2 * lax.rsqrt(ms + NORM_EPS) * g_ref[...]
        logits = lax.dot_general(wr_ref[...], h, _NT_DIMS, preferred_element_type=F32,
                                 precision=lax.Precision.HIGHEST)
        mx = jnp.max(logits, axis=0, keepdims=True)
        ex = jnp.exp(logits - mx)
        aff_ref[0, :, r0:r0 + OP_SUB] = ex / jnp.sum(ex, axis=0, keepdims=True)
        _store_token_major(x2_ref, x2, r0)
        _store_token_major(h2_ref, h, r0)


def _out_proj(oa, od, x, w_bf16, g, wr_t):
    B, S, _ = x.shape
    nt = S // OP_TM
    attn = lambda n: pl.BlockSpec((1, n, OP_TM, LANES), lambda b, t: (b, 0, t, 0))
    tok_major = pl.BlockSpec((1, OP_TM * ROW_CHUNKS, LANES), lambda b, t: (b, t, 0))
    return pl.pallas_call(
        _out_proj_body,
        grid=(B, nt),
        in_specs=[
            attn(N_DIFF_HEADS), attn(N_DIL_PAIRS),
            pl.BlockSpec((1, OP_TM, D_MODEL), lambda b, t: (b, t, 0)),
            pl.BlockSpec((D_MODEL, D_MODEL), lambda b, t: (0, 0)),
            pl.BlockSpec((1, D_MODEL), lambda b, t: (0, 0)),
            pl.BlockSpec((N_EXPERTS, D_MODEL), lambda b, t: (0, 0)),
        ],
        out_specs=[tok_major, tok_major, pl.BlockSpec((1, N_EXPERTS, OP_TM), lambda b, t: (b, 0, t))],
        out_shape=[
            jax.ShapeDtypeStruct((B, S * ROW_CHUNKS, LANES), F32),
            jax.ShapeDtypeStruct((B, S * ROW_CHUNKS, LANES), F32),
            jax.ShapeDtypeStruct((B, N_EXPERTS, S), F32),
        ],
        compiler_params=_cparams(("arbitrary", "arbitrary")),
        name="out_proj",
    )(oa, od, x, w_bf16, g, wr_t)


RT_GROUP = 2


def _route_body(aff_ref, idx_ref, xh_ref, xl_ref, st_ref, en_ref):
    E, S = aff_ref.shape[1], aff_ref.shape[2]
    C = idx_ref.shape[2]
    n_chunks = S // LANES
    a = aff_ref[0]
    bits = pltpu.bitcast(a, I32)
    tau = jnp.zeros((E, 1), I32)
    for bit in range(30, -1, -1):
        cand = tau | (1 << bit)
        cnt = jnp.sum(jnp.where(bits >= cand, 1.0, 0.0).astype(F32), axis=1, keepdims=True)
        tau = jnp.where(cnt >= C, cand, tau)
    gt = bits > tau
    eq = bits == tau
    need = C - jnp.sum(jnp.where(gt, 1.0, 0.0).astype(F32), axis=1, keepdims=True)

    tri = (lax.broadcasted_iota(I32, (LANES, LANES), 0) <= lax.broadcasted_iota(I32, (LANES, LANES), 1)).astype(BF16)

    def chunk_cumsum(mj, carry):
        return jnp.dot(mj.astype(BF16), tri, preferred_element_type=F32) + carry

    sel_chunks = []
    carry = jnp.zeros((E, 1), F32)
    for j in range(n_chunks):
        cs = slice(j * LANES, (j + 1) * LANES)
        eq_j = jnp.where(eq[:, cs], 1.0, 0.0).astype(F32)
        inc = chunk_cumsum(eq_j, carry)
        carry = inc[:, LANES - 1:LANES]
        take_tie = jnp.where(inc - eq_j < need, eq_j, 0.0)
        sel_chunks.append(jnp.where(gt[:, cs], 1.0, take_tie))

    lane_e = lax.broadcasted_iota(I32, (E, LANES), 1)
    st = jnp.full((E, LANES), float(C), F32)
    en = jnp.full((E, LANES), float(C), F32)
    carry = jnp.zeros((E, 1), F32)
    for j in range(n_chunks):
        sel_j = sel_chunks[j]
        inc = chunk_cumsum(sel_j, carry)
        posm = jnp.where(sel_j > 0.0, inc - sel_j, -1.0)
        hi = jnp.floor(posm * 0.5)
        xh_ref[pl.ds(j * E, E), :] = hi
        xl_ref[pl.ds(j * E, E), :] = posm - 2.0 * hi
        st = jnp.where(lane_e == j, carry, st)
        carry = inc[:, LANES - 1:LANES]
        en = jnp.where(lane_e == j, carry, en)
    pad_rows = (LANES - n_chunks) * E
    xh_ref[pl.ds(n_chunks * E, pad_rows), :] = jnp.zeros((pad_rows, LANES), F32)
    xl_ref[pl.ds(n_chunks * E, pad_rows), :] = jnp.zeros((pad_rows, LANES), F32)
    st_ref[...] = st
    en_ref[...] = en

    c_iota = lax.broadcasted_iota(I32, (C, LANES), 0).astype(F32)
    lane_row = jnp.where(lax.broadcasted_iota(I32, (SUBLANES, LANES), 0) == 0,
                         lax.broadcasted_iota(I32, (SUBLANES, LANES), 1), 0).astype(F32).astype(BF16)

    def expert_group(g, _):
        for u in range(RT_GROUP):
            e = g * RT_GROUP + u
            st_e = st_ref[pl.ds(e, 1), :]
            en_e = en_ref[pl.ds(e, 1), :]
            in_chunk = jnp.where(st_e <= c_iota, jnp.where(c_iota < en_e, 1.0, 0.0), 0.0).astype(BF16)
            xh = xh_ref[pl.ds(e, LANES, stride=E), :].astype(BF16)
            xl = xl_ref[pl.ds(e, LANES, stride=E), :].astype(BF16)
            hl = jnp.dot(in_chunk, jnp.concatenate([xh, xl], axis=1), preferred_element_type=F32)
            slot_of = 2.0 * hl[:, 0:LANES] + hl[:, LANES:2 * LANES]
            hit = jnp.where(slot_of == c_iota, 1.0, 0.0).astype(BF16)
            chunk_id = lax.dot_general(lane_row, in_chunk, _NT_DIMS, preferred_element_type=F32)
            lane_id = lax.dot_general(lane_row, hit, _NT_DIMS, preferred_element_type=F32)
            idx_ref[0, pl.ds(e, 1), :] = (chunk_id[0:1, :] * float(LANES) + lane_id[0:1, :] + 0.5).astype(I32)
        return 0

    lax.fori_loop(0, E // RT_GROUP, expert_group, 0)


def _route(aff):
    B, E, S = aff.shape
    C = EC_K * S // E
    return pl.pallas_call(
        _route_body,
        grid=(B,),
        in_specs=[pl.BlockSpec((1, E, S), lambda b: (b, 0, 0))],
        out_specs=pl.BlockSpec((1, E, C), lambda b: (b, 0, 0)),
        out_shape=jax.ShapeDtypeStruct((B, E, C), I32),
        scratch_shapes=[pltpu.VMEM((LANES * E, LANES), F32), pltpu.VMEM((LANES * E, LANES), F32),
                        pltpu.VMEM((E, LANES), F32), pltpu.VMEM((E, LANES), F32)],
        compiler_params=_cparams(("arbitrary",)),
        name="route",
    )(aff)


GA_UNROLL = 8


def _gather_body(idx_ref, h_ref, xe_ref, g_ref):
    C = idx_ref.shape[2]

    def rows(i, _):
        for u in range(GA_UNROLL):
            r = i * GA_UNROLL + u
            t = idx_ref[0, 0, r]
            src = pl.multiple_of(t * ROW_CHUNKS, ROW_CHUNKS)
            dst = pl.multiple_of(r * ROW_CHUNKS, ROW_CHUNKS)
            g_ref[pl.ds(dst, ROW_CHUNKS), :] = h_ref[0, pl.ds(src, ROW_CHUNKS), :]
        return 0

    lax.fori_loop(0, C // GA_UNROLL, rows, 0)
    for j in range(ROW_CHUNKS):
        xe_ref[0, :, j * LANES:(j + 1) * LANES] = g_ref[pl.ds(j, C, stride=ROW_CHUNKS), :].astype(BF16)


def _gather(idx, h2):
    B, E, C = idx.shape
    SR = h2.shape[1]
    return pl.pallas_call(
        _gather_body,
        grid=(B, E),
        in_specs=[
            pl.BlockSpec((1, 1, C), lambda b, e: (b * E + e, 0, 0), memory_space=pltpu.SMEM),
            pl.BlockSpec((1, SR, LANES), lambda b, e: (b, 0, 0), pipeline_mode=pl.Buffered(1)),
        ],
        out_specs=pl.BlockSpec((1, C, D_MODEL), lambda b, e: (e, b, 0)),
        out_shape=jax.ShapeDtypeStruct((E, B * C, D_MODEL), BF16),
        scratch_shapes=[pltpu.VMEM((C * ROW_CHUNKS, LANES), F32)],
        compiler_params=_cparams(("arbitrary", "arbitrary")),
        name="gather",
    )(idx.reshape(B * E, 1, C), h2)


FF_TM = 1024
FF_TF = 1408


def _ffn_body(x_ref, wg_ref, wu_ref, wd_ref, y_ref, acc_ref):
    f = pl.program_id(2)
    x = x_ref[0]
    hg = jnp.dot(x, wg_ref[0], preferred_element_type=F32)
    hu = jnp.dot(x, wu_ref[0], preferred_element_type=F32)
    he = (hg * jax.nn.sigmoid(hg) * hu).astype(BF16)
    part = jnp.dot(he, wd_ref[0], preferred_element_type=F32)

    @pl.when(f == 0)
    def _():
        acc_ref[...] = part

    @pl.when(f > 0)
    def _():
        acc_ref[...] += part

    @pl.when(f == pl.num_programs(2) - 1)
    def _():
        _store_token_major(y_ref, acc_ref[...])


def _ffn(xe, wg, wu, wd):
    E, M, _ = xe.shape
    nm = M // FF_TM
    nf = D_FF // FF_TF
    return pl.pallas_call(
        _ffn_body,
        grid=(E, nm, nf),
        in_specs=[
            pl.BlockSpec((1, FF_TM, D_MODEL), lambda e, m, f: (e, m, 0)),
            pl.BlockSpec((1, D_MODEL, FF_TF), lambda e, m, f: (e, 0, f)),
            pl.BlockSpec((1, D_MODEL, FF_TF), lambda e, m, f: (e, 0, f)),
            pl.BlockSpec((1, FF_TF, D_MODEL), lambda e, m, f: (e, f, 0)),
        ],
        out_specs=pl.BlockSpec((1, FF_TM * ROW_CHUNKS, LANES), lambda e, m, f: (e, m, 0)),
        out_shape=jax.ShapeDtypeStruct((E, M * ROW_CHUNKS, LANES), F32),
        scratch_shapes=[pltpu.VMEM((FF_TM, D_MODEL), F32)],
        compiler_params=_cparams(("arbitrary", "arbitrary", "arbitrary")),
        name="expert_ffn",
    )(xe, wg, wu, wd)


CB_UNROLL = 4
CB_TM = 512


def _combine_body(idx_ref, aff_ref, y_ref, x2_ref, g_ref, o_ref, acc_ref, n_ref):
    E = N_EXPERTS
    C = idx_ref.shape[2]
    s = pl.program_id(1)

    @pl.when(s == 0)
    def _():
        def zero(i, _):
            r0 = pl.multiple_of(i * 1024, 1024)
            acc_ref[pl.ds(r0, 1024), :] = jnp.zeros((1024, LANES), F32)
            return 0
        lax.fori_loop(0, acc_ref.shape[0] // 1024, zero, 0)

    @pl.when(s < E)
    def _():
        def rows(i, _):
            dsts, sums = [], []
            for u in range(CB_UNROLL):
                r = i * CB_UNROLL + u
                t = idx_ref[0, 0, r]
                gate = aff_ref[0, 0, t]
                dst = pl.multiple_of(t * ROW_CHUNKS, ROW_CHUNKS)
                src = pl.multiple_of(r * ROW_CHUNKS, ROW_CHUNKS)
                dsts.append(dst)
                sums.append(acc_ref[pl.ds(dst, ROW_CHUNKS), :] + gate * y_ref[0, pl.ds(src, ROW_CHUNKS), :])
            for dst, v in zip(dsts, sums):
                acc_ref[pl.ds(dst, ROW_CHUNKS), :] = v
            return 0
        lax.fori_loop(0, C // CB_UNROLL, rows, 0)

    @pl.when(s >= E)
    def _():
        r0 = pl.multiple_of((s - E) * (CB_TM * ROW_CHUNKS), CB_TM * ROW_CHUNKS)
        v = x2_ref[0] + acc_ref[pl.ds(r0, CB_TM * ROW_CHUNKS), :]
        v3 = v.reshape(CB_TM, ROW_CHUNKS, LANES)
        ss = jnp.sum(jnp.sum(v3 * v3, axis=2, keepdims=True), axis=1, keepdims=True)
        n3 = v3 * lax.rsqrt(ss * (1.0 / D_MODEL) + NORM_EPS) * g_ref[...][None]
        n_ref[...] = n3.reshape(CB_TM * ROW_CHUNKS, LANES)
        for j in range(ROW_CHUNKS):
            o_ref[0, :, j * LANES:(j + 1) * LANES] = n_ref[pl.ds(j, CB_TM, stride=ROW_CHUNKS), :]


def _combine(idx, aff, ye, x2, g_f):
    B, E, C = idx.shape
    SR = x2.shape[1]
    S = SR // ROW_CHUNKS
    nfin = S // CB_TM
    return pl.pallas_call(
        _combine_body,
        grid=(B, E + nfin),
        in_specs=[
            pl.BlockSpec((1, 1, C), lambda b, s: (b * E + jnp.minimum(s, E - 1), 0, 0), memory_space=pltpu.SMEM),
            pl.BlockSpec((1, 1, S), lambda b, s: (b * E + jnp.minimum(s, E - 1), 0, 0), memory_space=pltpu.SMEM),
            pl.BlockSpec((1, C * ROW_CHUNKS, LANES), lambda b, s: (jnp.minimum(s, E - 1), b, 0)),
            pl.BlockSpec((1, CB_TM * ROW_CHUNKS, LANES), lambda b, s: (b, jnp.maximum(s - E, 0), 0)),
            pl.BlockSpec((ROW_CHUNKS, LANES), lambda b, s: (0, 0)),
        ],
        out_specs=pl.BlockSpec((1, CB_TM, D_MODEL), lambda b, s: (b, jnp.maximum(s - E, 0), 0)),
        out_shape=jax.ShapeDtypeStruct((B, S, D_MODEL), F32),
        scratch_shapes=[pltpu.VMEM((SR, LANES), F32), pltpu.VMEM((CB_TM * ROW_CHUNKS, LANES), F32)],
        compiler_params=_cparams(("arbitrary", "arbitrary")),
        name="combine",
    )(idx.reshape(B * E, 1, C), aff.reshape(B * E, 1, S), ye, x2, g_f)


def kernel(x, norm1_g, w_in, lam_q1, lam_k1, lam_q2, lam_k2, subln_g, w_out, rel_bias,
           norm2_g, w_router, w_gate, w_up, w_down, norm_f_g):
    B, S, _ = x.shape
    C = EC_K * S // N_EXPERTS
    e_bias = _diff_bias(rel_bias)
    m_bias = _dil_bias(rel_bias)
    p, p4, p16 = _in_proj(x, norm1_g, w_in[0].astype(BF16))
    oa = _diff_attn(p, e_bias, lam_q1, lam_k1, lam_q2, lam_k2, subln_g)
    od = _dil_attn(p, p4, p16, m_bias)
    x2, h2, aff = _out_proj(oa, od, x, w_out[0].astype(BF16), norm2_g, w_router[0].T)
    idx = _route(aff)
    xe = _gather(idx, h2)
    ye = _ffn(xe, w_gate[0].astype(BF16), w_up[0].astype(BF16), w_down[0].astype(BF16))
    return _combine(idx, aff, ye, x2, norm_f_g.reshape(ROW_CHUNKS, LANES))
```

```python
import functools
import math

import jax
import jax.numpy as jnp
from jax import lax
from jax.experimental import pallas as pl
from jax.experimental.pallas import tpu as pltpu

F32 = jnp.float32
BF16 = jnp.bfloat16
I32 = jnp.int32

D_MODEL = 1024
HEAD_DIM = 64
N_DIFF_HEADS = 4
N_DIL_HEADS = 8
N_DIL_PAIRS = N_DIL_HEADS // 2
DIFF_WIDTH = 512
DIL_WIDTH = 512
N_IN = 3 * DIFF_WIDTH + 3 * DIL_WIDTH
DIL_PATTERNS = ((128, 1), (512, 4), (2048, 16))
N_BUCKETS = 32
REL_MAX_DIST = 1024
N_EXPERTS = 16
EC_K = 2
D_FF = 2816
NORM_EPS = 1e-6
SUBLN_EPS = 1e-5
NEG = -1e30
LAMBDA_INIT = 0.8 - 0.6 * math.exp(0.0)

LANES = 128
SUBLANES = 8
ROW_CHUNKS = D_MODEL // LANES

DA_TQ = 256
DA_TK = 512
DA_GROUP = 4
DA_EST = 128
DA_DMIN = -1280
DA_DMAX = 1024
DA_WTOT = DA_DMAX - DA_DMIN + DA_TK

DL_BQ = 128
DL_HALF = 64
DL_WIN = DL_BQ + 2 * DL_HALF
DL_GROUP = 8

LOG2E = math.log2(math.e)

VMEM_LIMIT = 56 * 1024 * 1024


def _cparams(sem):
    return pltpu.CompilerParams(dimension_semantics=sem, vmem_limit_bytes=VMEM_LIMIT)


def _t5_bucket(rel):
    half = N_BUCKETS // 2
    max_exact = half // 2
    n = jnp.abs(rel)
    nf = jnp.maximum(n, 1).astype(F32)
    large = max_exact + (jnp.log(nf / max_exact) / math.log(REL_MAX_DIST / max_exact)
                         * (half - max_exact)).astype(I32)
    large = jnp.minimum(large, half - 1)
    return jnp.where(rel > 0, half, 0) + jnp.where(n < max_exact, n, large)


def _bias_lookup(rel, tab_ref, col):
    bucket = _t5_bucket(rel)
    acc = jnp.zeros(rel.shape, F32)
    for j in range(N_BUCKETS):
        acc = jnp.where(bucket == j, tab_ref[j, col], acc)
    return acc


def _diff_bias_body(tab_ref, e_ref):
    h = pl.program_id(0)

    def chunk(cc, _):
        c0 = pl.multiple_of(cc * LANES, LANES)
        i = lax.broadcasted_iota(I32, (DA_TQ, LANES), 0)
        c = lax.broadcasted_iota(I32, (DA_TQ, LANES), 1) + c0
        e_ref[0, :, pl.ds(c0, LANES)] = _bias_lookup(c - i + DA_DMIN, tab_ref, h) * LOG2E
        return 0

    lax.fori_loop(0, DA_WTOT // LANES, chunk, 0)


def _diff_bias(rel_bias):
    return pl.pallas_call(
        _diff_bias_body,
        grid=(N_DIFF_HEADS,),
        in_specs=[pl.BlockSpec(memory_space=pltpu.SMEM)],
        out_specs=pl.BlockSpec((1, DA_TQ, DA_WTOT), lambda h: (h, 0, 0)),
        out_shape=jax.ShapeDtypeStruct((N_DIFF_HEADS, DA_TQ, DA_WTOT), F32),
        compiler_params=_cparams(("arbitrary",)),
        name="diff_bias",
    )(rel_bias)


def _dil_bias_body(tab_ref, m_ref):
    p = pl.program_id(0)
    h = pl.program_id(1)
    dil = jnp.where(p == 0, DIL_PATTERNS[0][1], jnp.where(p == 1, DIL_PATTERNS[1][1], DIL_PATTERNS[2][1]))
    i = lax.broadcasted_iota(I32, (DL_BQ, DL_WIN), 0)
    c = lax.broadcasted_iota(I32, (DL_BQ, DL_WIN), 1)
    off = c - DL_HALF - i
    bias = _bias_lookup(off * dil, tab_ref, N_DIFF_HEADS + h) * LOG2E
    band = jnp.abs(off) <= DL_HALF
    m_ref[0, 0, 0] = jnp.where(band & (c >= DL_HALF), bias, NEG)
    m_ref[0, 0, 1] = jnp.where(band, bias, NEG)
    m_ref[0, 0, 2] = jnp.where(band & (c < DL_HALF + DL_BQ), bias, NEG)


def _dil_bias(rel_bias):
    n_pat = len(DIL_PATTERNS)
    return pl.pallas_call(
        _dil_bias_body,
        grid=(n_pat, N_DIL_HEADS),
        in_specs=[pl.BlockSpec(memory_space=pltpu.SMEM)],
        out_specs=pl.BlockSpec((1, 1, 3, DL_BQ, DL_WIN), lambda p, h: (p, h, 0, 0, 0)),
        out_shape=jax.ShapeDtypeStruct((n_pat, N_DIL_HEADS, 3, DL_BQ, DL_WIN), F32),
        compiler_params=_cparams(("arbitrary", "arbitrary")),
        name="dil_bias",
    )(rel_bias)


IP_TM = 512
IP_TN = 256
N_COLBLK = N_IN // LANES
N_DIL_COLBLK = 3 * DIL_WIDTH // LANES


def _in_proj_body(x_ref, g_ref, w_ref, p_ref, p4_ref, p16_ref, scr_ref):
    q_scale = HEAD_DIM ** -0.5 * LOG2E
    x = x_ref[0]
    ms = jnp.mean(x * x, axis=-1, keepdims=True)
    y = (x * lax.rsqrt(ms + NORM_EPS) * g_ref[...]).astype(BF16)
    for n in range(N_IN // IP_TN):
        c0 = n * IP_TN
        r = jnp.dot(y, w_ref[:, c0:c0 + IP_TN], preferred_element_type=F32)
        is_q = (c0 < DIFF_WIDTH) or (3 * DIFF_WIDTH <= c0 < 3 * DIFF_WIDTH + DIL_WIDTH)
        if is_q:
            r = r * q_scale
        for k in range(IP_TN // LANES):
            p_ref[0, 2 * n + k] = r[:, k * LANES:(k + 1) * LANES].astype(BF16)
        if c0 >= 3 * DIFF_WIDTH:
            for k in range(IP_TN // LANES):
                scr_ref[(c0 - 3 * DIFF_WIDTH) // LANES + k] = r[:, k * LANES:(k + 1) * LANES]
    for c in range(N_DIL_COLBLK):
        for d, dst_ref in ((4, p4_ref), (16, p16_ref)):
            for r_ in range(d):
                dst_ref[0, c, r_] = scr_ref[c, pl.ds(r_, IP_TM // d, stride=d), :].astype(BF16)


def _in_proj(x, g, w_bf16):
    B, S, _ = x.shape
    nt = S // IP_TM
    return pl.pallas_call(
        _in_proj_body,
        grid=(B, nt),
        in_specs=[
            pl.BlockSpec((1, IP_TM, D_MODEL), lambda b, t: (b, t, 0)),
            pl.BlockSpec((1, D_MODEL), lambda b, t: (0, 0)),
            pl.BlockSpec((D_MODEL, N_IN), lambda b, t: (0, 0)),
        ],
        out_specs=[
            pl.BlockSpec((1, N_COLBLK, IP_TM, LANES), lambda b, t: (b, 0, t, 0)),
            pl.BlockSpec((1, N_DIL_COLBLK, 4, IP_TM // 4, LANES), lambda b, t: (b, 0, 0, t, 0)),
            pl.BlockSpec((1, N_DIL_COLBLK, 16, IP_TM // 16, LANES), lambda b, t: (b, 0, 0, t, 0)),
        ],
        out_shape=[
            jax.ShapeDtypeStruct((B, N_COLBLK, S, LANES), BF16),
            jax.ShapeDtypeStruct((B, N_DIL_COLBLK, 4, S // 4, LANES), BF16),
            jax.ShapeDtypeStruct((B, N_DIL_COLBLK, 16, S // 16, LANES), BF16),
        ],
        scratch_shapes=[pltpu.VMEM((N_DIL_COLBLK, IP_TM, LANES), F32)],
        compiler_params=_cparams(("arbitrary", "arbitrary")),
        name="in_proj",
    )(x, g, w_bf16)


_NT_DIMS = (((1,), (1,)), ((), ()))


def _diff_attn_body(q_ref, k_ref, v_ref, e_ref, lq1_ref, lk1_ref, lq2_ref, lk2_ref, sg_ref, o_ref,
                    kx_ref, vx_ref, p_ref):
    S = q_ref.shape[2]
    TQ, TK = DA_TQ, DA_TK
    lam = (jnp.exp(jnp.sum(lq1_ref[...] * lk1_ref[...], axis=-1, keepdims=True))
           - jnp.exp(jnp.sum(lq2_ref[...] * lk2_ref[...], axis=-1, keepdims=True)) + LAMBDA_INIT)
    lane_q = lax.broadcasted_iota(I32, (TQ, LANES), 1)
    lo = lane_q < HEAD_DIM
    lane2 = lax.broadcasted_iota(I32, (2 * TQ, LANES), 1)

    def fill(i, _):
        r0 = pl.multiple_of(i * 512, 512)
        e0 = jnp.where(lax.broadcasted_iota(I32, (512, LANES), 1) == 0, 1.0, 0.0).astype(BF16)
        kx_ref[pl.ds(r0, 512), 0:LANES] = k_ref[0, 0, pl.ds(r0, 512), :]
        kx_ref[pl.ds(r0, 512), LANES:2 * LANES] = e0
        vx_ref[pl.ds(r0, 512), 0:LANES] = v_ref[0, 0, pl.ds(r0, 512), :]
        vx_ref[pl.ds(r0, 512), LANES:2 * LANES] = e0
        return 0

    lax.fori_loop(0, S // 512, fill, 0)

    def q_tile(q0, exact_shift, slot=0):
        ps_ref = p_ref.at[slot]
        q = q_ref[0, 0, pl.ds(q0, TQ), :]
        zero = jnp.zeros_like(q)
        qa = jnp.concatenate([jnp.where(lo, q, zero), jnp.where(lo, zero, q)], axis=0)

        def bias_tile(k0):
            off = pl.multiple_of(jnp.clip(k0 - q0, DA_DMIN, DA_DMAX) - DA_DMIN, LANES)
            return e_ref[0, :, pl.ds(off, TK)]

        def tile_max(k0, pm, width=TK):
            s = lax.dot_general(qa, k_ref[0, 0, pl.ds(k0, width), :], _NT_DIMS, preferred_element_type=F32)
            bias = bias_tile(k0)[:, 0:width]
            out = []
            for half in range(2):
                sb = s[half * TQ:(half + 1) * TQ] + bias
                t = pm[half]
                for j in range(width // LANES):
                    t = jnp.maximum(t, sb[:, j * LANES:(j + 1) * LANES])
                out.append(t)
            return tuple(out)

        neg = jnp.full((TQ, LANES), -jnp.inf, F32)
        if exact_shift:
            pm = lax.fori_loop(0, S // TK, lambda ki, pm: tile_max(pl.multiple_of(ki * TK, TK), pm), (neg, neg))
        else:
            pm = tile_max(0, (neg, neg), width=DA_EST)
        m = jnp.max(jnp.concatenate(pm, axis=0), axis=-1, keepdims=True)
        mcol = jnp.where(lane2 == 0, -m, 0.0).astype(BF16)
        qx = jnp.concatenate([qa, mcol], axis=1)

        for ki in range(S // TK):
            k0 = ki * TK
            s = lax.dot_general(qx, kx_ref[k0:k0 + TK, :], _NT_DIMS, preferred_element_type=F32)
            bias = bias_tile(k0)
            for half in range(2):
                pr = jnp.exp2(s[half * TQ:(half + 1) * TQ] + bias).astype(BF16)
                ps_ref[half * TQ:(half + 1) * TQ, k0:k0 + TK] = pr

        acc = jnp.dot(ps_ref[...], vx_ref[...], preferred_element_type=F32)
        o1 = acc[0:TQ, 0:LANES] / acc[0:TQ, LANES:LANES + 1]
        o2 = acc[TQ:2 * TQ, 0:LANES] / acc[TQ:2 * TQ, LANES:LANES + 1]
        o = o1 - lam * o2
        ms = jnp.mean(o * o, axis=-1, keepdims=True)
        o = o * lax.rsqrt(ms + SUBLN_EPS) * sg_ref[...] * (1.0 - LAMBDA_INIT)
        o_ref[0, 0, pl.ds(q0, TQ), :] = o.astype(BF16)
        return jnp.where(jnp.isfinite(o), 0.0, 1.0)

    def fast_loop(qi, bad):
        for u in range(DA_GROUP):
            q0 = pl.multiple_of((qi * DA_GROUP + u) * TQ, TQ)
            bad = jnp.maximum(bad, q_tile(q0, exact_shift=False, slot=u))
        return bad

    bad = lax.fori_loop(0, S // (TQ * DA_GROUP), fast_loop, jnp.zeros((TQ, LANES), F32))

    @pl.when(jnp.max(bad) > 0.0)
    def _():
        def exact_loop(qi, _):
            q_tile(pl.multiple_of(qi * TQ, TQ), exact_shift=True)
            return 0
        lax.fori_loop(0, S // TQ, exact_loop, 0)


def _diff_attn(p, e_bias, lq1, lk1, lq2, lk2, subln_g):
    B, _, S, _ = p.shape
    H = N_DIFF_HEADS
    vec = lambda n: pl.BlockSpec((1, n), lambda b, h: (0, 0))
    return pl.pallas_call(
        _diff_attn_body,
        grid=(B, H),
        in_specs=[
            pl.BlockSpec((1, 1, S, LANES), lambda b, h: (b, h, 0, 0)),
            pl.BlockSpec((1, 1, S, LANES), lambda b, h: (b, H + h, 0, 0)),
            pl.BlockSpec((1, 1, S, LANES), lambda b, h: (b, 2 * H + h, 0, 0)),
            pl.BlockSpec((1, DA_TQ, DA_WTOT), lambda b, h: (h, 0, 0)),
            vec(HEAD_DIM), vec(HEAD_DIM), vec(HEAD_DIM), vec(HEAD_DIM), vec(2 * HEAD_DIM),
        ],
        out_specs=pl.BlockSpec((1, 1, S, LANES), lambda b, h: (b, h, 0, 0)),
        out_shape=jax.ShapeDtypeStruct((B, H, S, LANES), BF16),
        scratch_shapes=[
            pltpu.VMEM((S, 2 * LANES), BF16),
            pltpu.VMEM((S, 2 * LANES), BF16),
            pltpu.VMEM((DA_GROUP, 2 * DA_TQ, S), BF16),
        ],
        compiler_params=_cparams(("arbitrary", "arbitrary")),
        name="diff_attn",
    )(p, p, p, e_bias, lq1, lk1, lq2, lk2, subln_g)


def _dil_attn_body(q1_ref, k1_ref, v1_ref, q4_ref, k4_ref, v4_ref, q16_ref, k16_ref, v16_ref, m_ref,
                   o_ref, kpad_ref, vlo_ref, vhi_ref, oacc_ref, lse_ref):
    S = q1_ref.shape[2]
    lane = lax.broadcasted_iota(I32, (DL_BQ, LANES), 1)
    lo = lane < HEAD_DIM
    zpad = jnp.zeros((DL_HALF, LANES), BF16)

    def ones_fill(i, _):
        r0 = pl.multiple_of(i * 512, 512)
        lo_f = lax.broadcasted_iota(I32, (512, LANES), 1) < HEAD_DIM
        vlo_ref[pl.ds(r0, 512), LANES:2 * LANES] = jnp.where(lo_f, 1.0, 0.0).astype(BF16)
        vhi_ref[pl.ds(r0, 512), LANES:2 * LANES] = jnp.where(lo_f, 0.0, 1.0).astype(BF16)
        return 0

    lax.fori_loop(0, vlo_ref.shape[0] // 512, ones_fill, 0)

    def run_pattern(p, dil, get_q, get_k, get_v):
        L = S // dil
        nblk = L // DL_BQ
        LP = L + 2 * DL_HALF

        def fill(r, _):
            base = pl.multiple_of(r * LP, DL_HALF)
            for off in (0, DL_HALF + L):
                kpad_ref[pl.ds(base + off, DL_HALF), :] = zpad
                vlo_ref[pl.ds(base + off, DL_HALF), 0:LANES] = zpad
                vhi_ref[pl.ds(base + off, DL_HALF), 0:LANES] = zpad
            kpad_ref[pl.ds(base + DL_HALF, L), :] = get_k(r)[...]
            v = get_v(r)[...]
            lo_v = lax.broadcasted_iota(I32, v.shape, 1) < HEAD_DIM
            zero = jnp.zeros_like(v)
            vlo_ref[pl.ds(base + DL_HALF, L), 0:LANES] = jnp.where(lo_v, v, zero)
            vhi_ref[pl.ds(base + DL_HALF, L), 0:LANES] = jnp.where(lo_v, zero, v)
            return 0

        lax.fori_loop(0, dil, fill, 0)

        def group(gi, _):
            for u in range(DL_GROUP):
                n = gi * DL_GROUP + u
                r = n // nblk
                blk = n % nblk
                q0 = pl.multiple_of(blk * DL_BQ, DL_BQ)
                w0 = pl.multiple_of(r * LP + q0, DL_HALF)
                q = get_q(r)[pl.ds(q0, DL_BQ), :]
                kw = kpad_ref[pl.ds(w0, DL_WIN), :]
                vw = jnp.concatenate([vlo_ref[pl.ds(w0, DL_WIN), :], vhi_ref[pl.ds(w0, DL_WIN), :]], axis=0)
                edge = jnp.where(blk == 0, 0, jnp.where(blk == nblk - 1, 2, 1))
                zero = jnp.zeros_like(q)
                probs, maxes = [], []
                for hh in range(2):
                    qm = jnp.where(lo, q, zero) if hh == 0 else jnp.where(lo, zero, q)
                    s = lax.dot_general(qm, kw, _NT_DIMS, preferred_element_type=F32) + m_ref[p, hh, edge]
                    m = jnp.max(s, axis=-1, keepdims=True)
                    probs.append(jnp.exp2(s - m).astype(BF16))
                    maxes.append(m)
                both = jnp.dot(jnp.concatenate(probs, axis=1), vw, preferred_element_type=F32)
                l = both[:, LANES:2 * LANES]
                o = both[:, 0:LANES] / l
                lse = jnp.where(lo, maxes[0], maxes[1]) + jnp.log2(l)
                if dil == 1:
                    oacc_ref[p, pl.ds(q0, DL_BQ), :] = o
                    lse_ref[p, pl.ds(q0, DL_BQ), :] = lse
                else:
                    oacc_ref[p, pl.ds(q0 * dil + r, DL_BQ, stride=dil), :] = o
                    lse_ref[p, pl.ds(q0 * dil + r, DL_BQ, stride=dil), :] = lse
            return 0

        lax.fori_loop(0, dil * nblk // DL_GROUP, group, 0)

    run_pattern(0, 1, lambda r: q1_ref.at[0, 0], lambda r: k1_ref.at[0, 0], lambda r: v1_ref.at[0, 0])
    run_pattern(1, 4, lambda r: q4_ref.at[0, 0, r], lambda r: k4_ref.at[0, 0, r], lambda r: v4_ref.at[0, 0, r])
    run_pattern(2, 16, lambda r: q16_ref.at[0, 0, r], lambda r: k16_ref.at[0, 0, r], lambda r: v16_ref.at[0, 0, r])

    MIX = 256

    def mix(i, _):
        r0 = pl.multiple_of(i * MIX, MIX)
        ls = [lse_ref[p, pl.ds(r0, MIX), :] for p in range(3)]
        mx = jnp.maximum(jnp.maximum(ls[0], ls[1]), ls[2])
        ws = [jnp.exp2(l - mx) for l in ls]
        den = ws[0] + ws[1] + ws[2]
        num = ws[0] * oacc_ref[0, pl.ds(r0, MIX), :]
        num += ws[1] * oacc_ref[1, pl.ds(r0, MIX), :]
        num += ws[2] * oacc_ref[2, pl.ds(r0, MIX), :]
        o_ref[0, 0, pl.ds(r0, MIX), :] = (num / den).astype(BF16)
        return 0

    lax.fori_loop(0, S // MIX, mix, 0)


def _dil_attn(p, p4, p16, m_bias):
    B, _, S, _ = p.shape
    NP = N_DIL_PAIRS
    base = 3 * N_DIFF_HEADS
    nat = lambda o: pl.BlockSpec((1, 1, S, LANES), lambda b, h: (b, base + o * NP + h, 0, 0))
    res = lambda o, d: pl.BlockSpec((1, 1, d, S // d, LANES), lambda b, h: (b, o * NP + h, 0, 0, 0))
    return pl.pallas_call(
        _dil_attn_body,
        grid=(B, NP),
        in_specs=[nat(0), nat(1), nat(2), res(0, 4), res(1, 4), res(2, 4), res(0, 16), res(1, 16), res(2, 16),
                  pl.BlockSpec((3, 2, 3, DL_BQ, DL_WIN), lambda b, h: (0, h, 0, 0, 0))],
        out_specs=pl.BlockSpec((1, 1, S, LANES), lambda b, h: (b, h, 0, 0)),
        out_shape=jax.ShapeDtypeStruct((B, NP, S, LANES), BF16),
        scratch_shapes=[
            pltpu.VMEM((S + 2 * DL_HALF * DIL_PATTERNS[-1][1], LANES), BF16),
            pltpu.VMEM((S + 2 * DL_HALF * DIL_PATTERNS[-1][1], 2 * LANES), BF16),
            pltpu.VMEM((S + 2 * DL_HALF * DIL_PATTERNS[-1][1], 2 * LANES), BF16),
            pltpu.VMEM((3, S, LANES), F32),
            pltpu.VMEM((3, S, LANES), F32),
        ],
        compiler_params=_cparams(("arbitrary", "arbitrary")),
        name="dil_attn",
    )(p, p, p, p4, p4, p4, p16, p16, p16, m_bias)


OP_TM = 512


def _store_token_major(ref, val):
    rows = val.shape[0]
    for j in range(ROW_CHUNKS):
        ref[0, pl.ds(j, rows, stride=ROW_CHUNKS), :] = val[:, j * LANES:(j + 1) * LANES]


def _out_proj_body(oa_ref, od_ref, x_ref, w_ref, g_ref, wr_ref, x2_ref, h2_ref, aff_ref):
    pieces = [oa_ref[0, i] for i in range(N_DIFF_HEADS)] + [od_ref[0, i] for i in range(N_DIL_PAIRS)]
    a = jnp.concatenate(pieces, axis=-1)
    x2 = x_ref[0] + jnp.dot(a, w_ref[...], preferred_element_type=F32)
    ms = jnp.mean(x2 * x2, axis=-1, keepdims=True)
    h = x2 * lax.rsqrt(ms + NORM_EPS) * g_ref[...]
    logits = lax.dot_general(wr_ref[...], h, _NT_DIMS, preferred_element_type=F32,
                             precision=lax.Precision.HIGHEST)
    mx = jnp.max(logits, axis=0, keepdims=True)
    ex = jnp.exp(logits - mx)
    aff_ref[0] = ex / jnp.sum(ex, axis=0, keepdims=True)
    _store_token_major(x2_ref, x2)
    _store_token_major(h2_ref, h)


def _out_proj(oa, od, x, w_bf16, g, wr_t):
    B, S, _ = x.shape
    nt = S // OP_TM
    attn = lambda n: pl.BlockSpec((1, n, OP_TM, LANES), lambda b, t: (b, 0, t, 0))
    tok_major = pl.BlockSpec((1, OP_TM * ROW_CHUNKS, LANES), lambda b, t: (b, t, 0))
    return pl.pallas_call(
        _out_proj_body,
        grid=(B, nt),
        in_specs=[
            attn(N_DIFF_HEADS), attn(N_DIL_PAIRS),
            pl.BlockSpec((1, OP_TM, D_MODEL), lambda b, t: (b, t, 0)),
            pl.BlockSpec((D_MODEL, D_MODEL), lambda b, t: (0, 0)),
            pl.BlockSpec((1, D_MODEL), lambda b, t: (0, 0)),
            pl.BlockSpec((N_EXPERTS, D_MODEL), lambda b, t: (0, 0)),
        ],
        out_specs=[tok_major, tok_major, pl.BlockSpec((1, N_EXPERTS, OP_TM), lambda b, t: (b, 0, t))],
        out_shape=[
            jax.ShapeDtypeStruct((B, S * ROW_CHUNKS, LANES), F32),
            jax.ShapeDtypeStruct((B, S * ROW_CHUNKS, LANES), F32),
            jax.ShapeDtypeStruct((B, N_EXPERTS, S), F32),
        ],
        compiler_params=_cparams(("arbitrary", "arbitrary")),
        name="out_proj",
    )(oa, od, x, w_bf16, g, wr_t)


RT_GROUP = 2


def _route_body(aff_ref, idx_ref, xh_ref, xl_ref, st_ref, en_ref):
    E, S = aff_ref.shape[1], aff_ref.shape[2]
    C = idx_ref.shape[2]
    n_chunks = S // LANES
    a = aff_ref[0]
    bits = pltpu.bitcast(a, I32)
    tau = jnp.zeros((E, 1), I32)
    for bit in range(30, -1, -1):
        cand = tau | (1 << bit)
        cnt = jnp.sum(jnp.where(bits >= cand, 1.0, 0.0).astype(F32), axis=1, keepdims=True)
        tau = jnp.where(cnt >= C, cand, tau)
    gt = bits > tau
    eq = bits == tau
    need = C - jnp.sum(jnp.where(gt, 1.0, 0.0).astype(F32), axis=1, keepdims=True)

    tri = (lax.broadcasted_iota(I32, (LANES, LANES), 0) <= lax.broadcasted_iota(I32, (LANES, LANES), 1)).astype(BF16)

    def chunk_cumsum(mj, carry):
        return jnp.dot(mj.astype(BF16), tri, preferred_element_type=F32) + carry

    sel_chunks = []
    carry = jnp.zeros((E, 1), F32)
    for j in range(n_chunks):
        cs = slice(j * LANES, (j + 1) * LANES)
        eq_j = jnp.where(eq[:, cs], 1.0, 0.0).astype(F32)
        inc = chunk_cumsum(eq_j, carry)
        carry = inc[:, LANES - 1:LANES]
        take_tie = jnp.where(inc - eq_j < need, eq_j, 0.0)
        sel_chunks.append(jnp.where(gt[:, cs], 1.0, take_tie))

    lane_e = lax.broadcasted_iota(I32, (E, LANES), 1)
    st = jnp.full((E, LANES), float(C), F32)
    en = jnp.full((E, LANES), float(C), F32)
    carry = jnp.zeros((E, 1), F32)
    for j in range(n_chunks):
        sel_j = sel_chunks[j]
        inc = chunk_cumsum(sel_j, carry)
        posm = jnp.where(sel_j > 0.0, inc - sel_j, -1.0)
        hi = jnp.floor(posm * 0.5)
        xh_ref[pl.ds(j * E, E), :] = hi
        xl_ref[pl.ds(j * E, E), :] = posm - 2.0 * hi
        st = jnp.where(lane_e == j, carry, st)
        carry = inc[:, LANES - 1:LANES]
        en = jnp.where(lane_e == j, carry, en)
    pad_rows = (LANES - n_chunks) * E
    xh_ref[pl.ds(n_chunks * E, pad_rows), :] = jnp.zeros((pad_rows, LANES), F32)
    xl_ref[pl.ds(n_chunks * E, pad_rows), :] = jnp.zeros((pad_rows, LANES), F32)
    st_ref[...] = st
    en_ref[...] = en

    c_iota = lax.broadcasted_iota(I32, (C, LANES), 0).astype(F32)
    lane_row = jnp.where(lax.broadcasted_iota(I32, (SUBLANES, LANES), 0) == 0,
                         lax.broadcasted_iota(I32, (SUBLANES, LANES), 1), 0).astype(F32).astype(BF16)

    def expert_group(g, _):
        for u in range(RT_GROUP):
            e = g * RT_GROUP + u
            st_e = st_ref[pl.ds(e, 1), :]
            en_e = en_ref[pl.ds(e, 1), :]
            in_chunk = jnp.where(st_e <= c_iota, jnp.where(c_iota < en_e, 1.0, 0.0), 0.0).astype(BF16)
            xh = xh_ref[pl.ds(e, LANES, stride=E), :].astype(BF16)
            xl = xl_ref[pl.ds(e, LANES, stride=E), :].astype(BF16)
            hl = jnp.dot(in_chunk, jnp.concatenate([xh, xl], axis=1), preferred_element_type=F32)
            slot_of = 2.0 * hl[:, 0:LANES] + hl[:, LANES:2 * LANES]
            hit = jnp.where(slot_of == c_iota, 1.0, 0.0).astype(BF16)
            chunk_id = lax.dot_general(lane_row, in_chunk, _NT_DIMS, preferred_element_type=F32)
            lane_id = lax.dot_general(lane_row, hit, _NT_DIMS, preferred_element_type=F32)
            idx_ref[0, pl.ds(e, 1), :] = (chunk_id[0:1, :] * float(LANES) + lane_id[0:1, :] + 0.5).astype(I32)
        return 0

    lax.fori_loop(0, E // RT_GROUP, expert_group, 0)


def _route(aff):
    B, E, S = aff.shape
    C = EC_K * S // E
    return pl.pallas_call(
        _route_body,
        grid=(B,),
        in_specs=[pl.BlockSpec((1, E, S), lambda b: (b, 0, 0))],
        out_specs=pl.BlockSpec((1, E, C), lambda b: (b, 0, 0)),
        out_shape=jax.ShapeDtypeStruct((B, E, C), I32),
        scratch_shapes=[pltpu.VMEM((LANES * E, LANES), F32), pltpu.VMEM((LANES * E, LANES), F32),
                        pltpu.VMEM((E, LANES), F32), pltpu.VMEM((E, LANES), F32)],
        compiler_params=_cparams(("arbitrary",)),
        name="route",
    )(aff)


GA_UNROLL = 8


def _gather_body(idx_ref, h_ref, xe_ref, g_ref):
    C = idx_ref.shape[2]

    def rows(i, _):
        for u in range(GA_UNROLL):
            r = i * GA_UNROLL + u
            t = idx_ref[0, 0, r]
            src = pl.multiple_of(t * ROW_CHUNKS, ROW_CHUNKS)
            dst = pl.multiple_of(r * ROW_CHUNKS, ROW_CHUNKS)
            g_ref[pl.ds(dst, ROW_CHUNKS), :] = h_ref[0, pl.ds(src, ROW_CHUNKS), :]
        return 0

    lax.fori_loop(0, C // GA_UNROLL, rows, 0)
    for j in range(ROW_CHUNKS):
        xe_ref[0, :, j * LANES:(j + 1) * LANES] = g_ref[pl.ds(j, C, stride=ROW_CHUNKS), :].astype(BF16)


def _gather(idx, h2):
    B, E, C = idx.shape
    SR = h2.shape[1]
    return pl.pallas_call(
        _gather_body,
        grid=(B, E),
        in_specs=[
            pl.BlockSpec((1, 1, C), lambda b, e: (b * E + e, 0, 0), memory_space=pltpu.SMEM),
            pl.BlockSpec((1, SR, LANES), lambda b, e: (b, 0, 0), pipeline_mode=pl.Buffered(1)),
        ],
        out_specs=pl.BlockSpec((1, C, D_MODEL), lambda b, e: (e, b, 0)),
        out_shape=jax.ShapeDtypeStruct((E, B * C, D_MODEL), BF16),
        scratch_shapes=[pltpu.VMEM((C * ROW_CHUNKS, LANES), F32)],
        compiler_params=_cparams(("arbitrary", "arbitrary")),
        name="gather",
    )(idx.reshape(B * E, 1, C), h2)


FF_TM = 1024
FF_TF = 1408


def _ffn_body(x_ref, wg_ref, wu_ref, wd_ref, y_ref, acc_ref):
    f = pl.program_id(2)
    x = x_ref[0]
    hg = jnp.dot(x, wg_ref[0], preferred_element_type=F32)
    hu = jnp.dot(x, wu_ref[0], preferred_element_type=F32)
    he = (hg * jax.nn.sigmoid(hg) * hu).astype(BF16)
    part = jnp.dot(he, wd_ref[0], preferred_element_type=F32)

    @pl.when(f == 0)
    def _():
        acc_ref[...] = part

    @pl.when(f > 0)
    def _():
        acc_ref[...] += part

    @pl.when(f == pl.num_programs(2) - 1)
    def _():
        _store_token_major(y_ref, acc_ref[...])


def _ffn(xe, wg, wu, wd):
    E, M, _ = xe.shape
    nm = M // FF_TM
    nf = D_FF // FF_TF
    return pl.pallas_call(
        _ffn_body,
        grid=(E, nm, nf),
        in_specs=[
            pl.BlockSpec((1, FF_TM, D_MODEL), lambda e, m, f: (e, m, 0)),
            pl.BlockSpec((1, D_MODEL, FF_TF), lambda e, m, f: (e, 0, f)),
            pl.BlockSpec((1, D_MODEL, FF_TF), lambda e, m, f: (e, 0, f)),
            pl.BlockSpec((1, FF_TF, D_MODEL), lambda e, m, f: (e, f, 0)),
        ],
        out_specs=pl.BlockSpec((1, FF_TM * ROW_CHUNKS, LANES), lambda e, m, f: (e, m, 0)),
        out_shape=jax.ShapeDtypeStruct((E, M * ROW_CHUNKS, LANES), F32),
        scratch_shapes=[pltpu.VMEM((FF_TM, D_MODEL), F32)],
        compiler_params=_cparams(("arbitrary", "arbitrary", "arbitrary")),
        name="expert_ffn",
    )(xe, wg, wu, wd)


CB_UNROLL = 4
CB_TM = 512


def _combine_body(idx_ref, aff_ref, y_ref, x2_ref, g_ref, o_ref, acc_ref, n_ref):
    E = N_EXPERTS
    C = idx_ref.shape[2]
    s = pl.program_id(1)

    @pl.when(s == 0)
    def _():
        def zero(i, _):
            r0 = pl.multiple_of(i * 1024, 1024)
            acc_ref[pl.ds(r0, 1024), :] = jnp.zeros((1024, LANES), F32)
            return 0
        lax.fori_loop(0, acc_ref.shape[0] // 1024, zero, 0)

    @pl.when(s < E)
    def _():
        def rows(i, _):
            dsts, sums = [], []
            for u in range(CB_UNROLL):
                r = i * CB_UNROLL + u
                t = idx_ref[0, 0, r]
                gate = aff_ref[0, 0, t]
                dst = pl.multiple_of(t * ROW_CHUNKS, ROW_CHUNKS)
                src = pl.multiple_of(r * ROW_CHUNKS, ROW_CHUNKS)
                dsts.append(dst)
                sums.append(acc_ref[pl.ds(dst, ROW_CHUNKS), :] + gate * y_ref[0, pl.ds(src, ROW_CHUNKS), :])
            for dst, v in zip(dsts, sums):
                acc_ref[pl.ds(dst, ROW_CHUNKS), :] = v
            return 0
        lax.fori_loop(0, C // CB_UNROLL, rows, 0)

    @pl.when(s >= E)
    def _():
        r0 = pl.multiple_of((s - E) * (CB_TM * ROW_CHUNKS), CB_TM * ROW_CHUNKS)
        v = x2_ref[0] + acc_ref[pl.ds(r0, CB_TM * ROW_CHUNKS), :]
        v3 = v.reshape(CB_TM, ROW_CHUNKS, LANES)
        ss = jnp.sum(jnp.sum(v3 * v3, axis=2, keepdims=True), axis=1, keepdims=True)
        n3 = v3 * lax.rsqrt(ss * (1.0 / D_MODEL) + NORM_EPS) * g_ref[...][None]
        n_ref[...] = n3.reshape(CB_TM * ROW_CHUNKS, LANES)
        for j in range(ROW_CHUNKS):
            o_ref[0, :, j * LANES:(j + 1) * LANES] = n_ref[pl.ds(j, CB_TM, stride=ROW_CHUNKS), :]


def _combine(idx, aff, ye, x2, g_f):
    B, E, C = idx.shape
    SR = x2.shape[1]
    S = SR // ROW_CHUNKS
    nfin = S // CB_TM
    return pl.pallas_call(
        _combine_body,
        grid=(B, E + nfin),
        in_specs=[
            pl.BlockSpec((1, 1, C), lambda b, s: (b * E + jnp.minimum(s, E - 1), 0, 0), memory_space=pltpu.SMEM),
            pl.BlockSpec((1, 1, S), lambda b, s: (b * E + jnp.minimum(s, E - 1), 0, 0), memory_space=pltpu.SMEM),
            pl.BlockSpec((1, C * ROW_CHUNKS, LANES), lambda b, s: (jnp.minimum(s, E - 1), b, 0)),
            pl.BlockSpec((1, CB_TM * ROW_CHUNKS, LANES), lambda b, s: (b, jnp.maximum(s - E, 0), 0)),
            pl.BlockSpec((ROW_CHUNKS, LANES), lambda b, s: (0, 0)),
        ],
        out_specs=pl.BlockSpec((1, CB_TM, D_MODEL), lambda b, s: (b, jnp.maximum(s - E, 0), 0)),
        out_shape=jax.ShapeDtypeStruct((B, S, D_MODEL), F32),
        scratch_shapes=[pltpu.VMEM((SR, LANES), F32), pltpu.VMEM((CB_TM * ROW_CHUNKS, LANES), F32)],
        compiler_params=_cparams(("arbitrary", "arbitrary")),
        name="combine",
    )(idx.reshape(B * E, 1, C), aff.reshape(B * E, 1, S), ye, x2, g_f)


def kernel(x, norm1_g, w_in, lam_q1, lam_k1, lam_q2, lam_k2, subln_g, w_out, rel_bias,
           norm2_g, w_router, w_gate, w_up, w_down, norm_f_g):
    B, S, _ = x.shape
    C = EC_K * S // N_EXPERTS
    e_bias = _diff_bias(rel_bias)
    m_bias = _dil_bias(rel_bias)
    p, p4, p16 = _in_proj(x, norm1_g, w_in[0].astype(BF16))
    oa = _diff_attn(p, e_bias, lam_q1, lam_k1, lam_q2, lam_k2, subln_g)
    od = _dil_attn(p, p4, p16, m_bias)
    x2, h2, aff = _out_proj(oa, od, x, w_out[0].astype(BF16), norm2_g, w_router[0].T)
    idx = _route(aff)
    xe = _gather(idx, h2)
    ye = _ffn(xe, w_gate[0].astype(BF16), w_up[0].astype(BF16), w_down[0].astype(BF16))
    return _combine(idx, aff, ye, x2, norm_f_g.reshape(ROW_CHUNKS, LANES))
```

```python
import functools
import math

import jax
import jax.numpy as jnp
from jax import lax
from jax.experimental import pallas as pl
from jax.experimental.pallas import tpu as pltpu

F32 = jnp.float32
BF16 = jnp.bfloat16
I32 = jnp.int32

D_MODEL = 1024
HEAD_DIM = 64
N_DIFF_HEADS = 4
N_DIL_HEADS = 8
N_DIL_PAIRS = N_DIL_HEADS // 2
DIFF_WIDTH = 512
DIL_WIDTH = 512
N_IN = 3 * DIFF_WIDTH + 3 * DIL_WIDTH
DIL_PATTERNS = ((128, 1), (512, 4), (2048, 16))
N_BUCKETS = 32
REL_MAX_DIST = 1024
N_EXPERTS = 16
EC_K = 2
D_FF = 2816
NORM_EPS = 1e-6
SUBLN_EPS = 1e-5
NEG = -1e30
LAMBDA_INIT = 0.8 - 0.6 * math.exp(0.0)

LANES = 128
SUBLANES = 8
ROW_CHUNKS = D_MODEL // LANES

DA_TQ = 256
DA_TK = 512
DA_GROUP = 4
DA_EST = 128
DA_DMIN = -1280
DA_DMAX = 1024
DA_WTOT = DA_DMAX - DA_DMIN + DA_TK

DL_BQ = 128
DL_HALF = 64
DL_WIN = DL_BQ + 2 * DL_HALF
DL_GROUP = 8

LOG2E = math.log2(math.e)

VMEM_LIMIT = 56 * 1024 * 1024


def _cparams(sem):
    return pltpu.CompilerParams(dimension_semantics=sem, vmem_limit_bytes=VMEM_LIMIT)


def _t5_bucket(rel):
    half = N_BUCKETS // 2
    max_exact = half // 2
    n = jnp.abs(rel)
    nf = jnp.maximum(n, 1).astype(F32)
    large = max_exact + (jnp.log(nf / max_exact) / math.log(REL_MAX_DIST / max_exact)
                         * (half - max_exact)).astype(I32)
    large = jnp.minimum(large, half - 1)
    return jnp.where(rel > 0, half, 0) + jnp.where(n < max_exact, n, large)


def _bias_lookup(rel, tab_ref, col):
    bucket = _t5_bucket(rel)
    acc = jnp.zeros(rel.shape, F32)
    for j in range(N_BUCKETS):
        acc = jnp.where(bucket == j, tab_ref[j, col], acc)
    return acc


def _diff_bias_body(tab_ref, e_ref):
    h = pl.program_id(0)

    def chunk(cc, _):
        c0 = pl.multiple_of(cc * LANES, LANES)
        i = lax.broadcasted_iota(I32, (DA_TQ, LANES), 0)
        c = lax.broadcasted_iota(I32, (DA_TQ, LANES), 1) + c0
        e_ref[0, :, pl.ds(c0, LANES)] = _bias_lookup(c - i + DA_DMIN, tab_ref, h) * LOG2E
        return 0

    lax.fori_loop(0, DA_WTOT // LANES, chunk, 0)


def _diff_bias(rel_bias):
    return pl.pallas_call(
        _diff_bias_body,
        grid=(N_DIFF_HEADS,),
        in_specs=[pl.BlockSpec(memory_space=pltpu.SMEM)],
        out_specs=pl.BlockSpec((1, DA_TQ, DA_WTOT), lambda h: (h, 0, 0)),
        out_shape=jax.ShapeDtypeStruct((N_DIFF_HEADS, DA_TQ, DA_WTOT), F32),
        compiler_params=_cparams(("arbitrary",)),
        name="diff_bias",
    )(rel_bias)


def _dil_bias_body(tab_ref, m_ref):
    p = pl.program_id(0)
    h = pl.program_id(1)
    dil = jnp.where(p == 0, DIL_PATTERNS[0][1], jnp.where(p == 1, DIL_PATTERNS[1][1], DIL_PATTERNS[2][1]))
    i = lax.broadcasted_iota(I32, (DL_BQ, DL_WIN), 0)
    c = lax.broadcasted_iota(I32, (DL_BQ, DL_WIN), 1)
    off = c - DL_HALF - i
    bias = _bias_lookup(off * dil, tab_ref, N_DIFF_HEADS + h) * LOG2E
    band = jnp.abs(off) <= DL_HALF
    m_ref[0, 0, 0] = jnp.where(band & (c >= DL_HALF), bias, NEG)
    m_ref[0, 0, 1] = jnp.where(band, bias, NEG)
    m_ref[0, 0, 2] = jnp.where(band & (c < DL_HALF + DL_BQ), bias, NEG)


def _dil_bias(rel_bias):
    n_pat = len(DIL_PATTERNS)
    return pl.pallas_call(
        _dil_bias_body,
        grid=(n_pat, N_DIL_HEADS),
        in_specs=[pl.BlockSpec(memory_space=pltpu.SMEM)],
        out_specs=pl.BlockSpec((1, 1, 3, DL_BQ, DL_WIN), lambda p, h: (p, h, 0, 0, 0)),
        out_shape=jax.ShapeDtypeStruct((n_pat, N_DIL_HEADS, 3, DL_BQ, DL_WIN), F32),
        compiler_params=_cparams(("arbitrary", "arbitrary")),
        name="dil_bias",
    )(rel_bias)


IP_TM = 512
IP_TN = 256
N_COLBLK = N_IN // LANES
N_DIL_COLBLK = 3 * DIL_WIDTH // LANES


def _in_proj_body(x_ref, g_ref, w_ref, p_ref, p4_ref, p16_ref, scr_ref):
    q_scale = HEAD_DIM ** -0.5 * LOG2E
    x = x_ref[0]
    ms = jnp.mean(x * x, axis=-1, keepdims=True)
    y = (x * lax.rsqrt(ms + NORM_EPS) * g_ref[...]).astype(BF16)
    for n in range(N_IN // IP_TN):
        c0 = n * IP_TN
        r = jnp.dot(y, w_ref[:, c0:c0 + IP_TN], preferred_element_type=F32)
        is_q = (c0 < DIFF_WIDTH) or (3 * DIFF_WIDTH <= c0 < 3 * DIFF_WIDTH + DIL_WIDTH)
        if is_q:
            r = r * q_scale
        for k in range(IP_TN // LANES):
            p_ref[0, 2 * n + k] = r[:, k * LANES:(k + 1) * LANES].astype(BF16)
        if c0 >= 3 * DIFF_WIDTH:
            for k in range(IP_TN // LANES):
                scr_ref[(c0 - 3 * DIFF_WIDTH) // LANES + k] = r[:, k * LANES:(k + 1) * LANES]
    for c in range(N_DIL_COLBLK):
        for d, dst_ref in ((4, p4_ref), (16, p16_ref)):
            for r_ in range(d):
                dst_ref[0, c, r_] = scr_ref[c, pl.ds(r_, IP_TM // d, stride=d), :].astype(BF16)


def _in_proj(x, g, w_bf16):
    B, S, _ = x.shape
    nt = S // IP_TM
    return pl.pallas_call(
        _in_proj_body,
        grid=(B, nt),
        in_specs=[
            pl.BlockSpec((1, IP_TM, D_MODEL), lambda b, t: (b, t, 0)),
            pl.BlockSpec((1, D_MODEL), lambda b, t: (0, 0)),
            pl.BlockSpec((D_MODEL, N_IN), lambda b, t: (0, 0)),
        ],
        out_specs=[
            pl.BlockSpec((1, N_COLBLK, IP_TM, LANES), lambda b, t: (b, 0, t, 0)),
            pl.BlockSpec((1, N_DIL_COLBLK, 4, IP_TM // 4, LANES), lambda b, t: (b, 0, 0, t, 0)),
            pl.BlockSpec((1, N_DIL_COLBLK, 16, IP_TM // 16, LANES), lambda b, t: (b, 0, 0, t, 0)),
        ],
        out_shape=[
            jax.ShapeDtypeStruct((B, N_COLBLK, S, LANES), BF16),
            jax.ShapeDtypeStruct((B, N_DIL_COLBLK, 4, S // 4, LANES), BF16),
            jax.ShapeDtypeStruct((B, N_DIL_COLBLK, 16, S // 16, LANES), BF16),
        ],
        scratch_shapes=[pltpu.VMEM((N_DIL_COLBLK, IP_TM, LANES), F32)],
        compiler_params=_cparams(("arbitrary", "arbitrary")),
        name="in_proj",
    )(x, g, w_bf16)


_NT_DIMS = (((1,), (1,)), ((), ()))


def _diff_attn_body(q_ref, k_ref, v_ref, e_ref, lq1_ref, lk1_ref, lq2_ref, lk2_ref, sg_ref, o_ref,
                    kx_ref, vx_ref, p_ref):
    S = q_ref.shape[2]
    TQ, TK = DA_TQ, DA_TK
    lam = (jnp.exp(jnp.sum(lq1_ref[...] * lk1_ref[...], axis=-1, keepdims=True))
           - jnp.exp(jnp.sum(lq2_ref[...] * lk2_ref[...], axis=-1, keepdims=True)) + LAMBDA_INIT)
    lane_q = lax.broadcasted_iota(I32, (TQ, LANES), 1)
    lo = lane_q < HEAD_DIM
    lane2 = lax.broadcasted_iota(I32, (2 * TQ, LANES), 1)

    def fill(i, _):
        r0 = pl.multiple_of(i * 512, 512)
        e0 = jnp.where(lax.broadcasted_iota(I32, (512, LANES), 1) == 0, 1.0, 0.0).astype(BF16)
        kx_ref[pl.ds(r0, 512), 0:LANES] = k_ref[0, 0, pl.ds(r0, 512), :]
        kx_ref[pl.ds(r0, 512), LANES:2 * LANES] = e0
        vx_ref[pl.ds(r0, 512), 0:LANES] = v_ref[0, 0, pl.ds(r0, 512), :]
        vx_ref[pl.ds(r0, 512), LANES:2 * LANES] = e0
        return 0

    lax.fori_loop(0, S // 512, fill, 0)

    def q_tile(q0, exact_shift, slot=0):
        ps_ref = p_ref.at[slot]
        q = q_ref[0, 0, pl.ds(q0, TQ), :]
        zero = jnp.zeros_like(q)
        qa = jnp.concatenate([jnp.where(lo, q, zero), jnp.where(lo, zero, q)], axis=0)

        def bias_tile(k0):
            off = pl.multiple_of(jnp.clip(k0 - q0, DA_DMIN, DA_DMAX) - DA_DMIN, LANES)
            return e_ref[0, :, pl.ds(off, TK)]

        def tile_max(k0, pm, width=TK):
            s = lax.dot_general(qa, k_ref[0, 0, pl.ds(k0, width), :], _NT_DIMS, preferred_element_type=F32)
            bias = bias_tile(k0)[:, 0:width]
            out = []
            for half in range(2):
                sb = s[half * TQ:(half + 1) * TQ] + bias
                t = pm[half]
                for j in range(width // LANES):
                    t = jnp.maximum(t, sb[:, j * LANES:(j + 1) * LANES])
                out.append(t)
            return tuple(out)

        neg = jnp.full((TQ, LANES), -jnp.inf, F32)
        if exact_shift:
            pm = lax.fori_loop(0, S // TK, lambda ki, pm: tile_max(pl.multiple_of(ki * TK, TK), pm), (neg, neg))
        else:
            pm = tile_max(0, (neg, neg), width=DA_EST)
        m = jnp.max(jnp.concatenate(pm, axis=0), axis=-1, keepdims=True)
        mcol = jnp.where(lane2 == 0, -m, 0.0).astype(BF16)
        qx = jnp.concatenate([qa, mcol], axis=1)

        for ki in range(S // TK):
            k0 = ki * TK
            s = lax.dot_general(qx, kx_ref[k0:k0 + TK, :], _NT_DIMS, preferred_element_type=F32)
            bias = bias_tile(k0)
            for half in range(2):
                pr = jnp.exp2(s[half * TQ:(half + 1) * TQ] + bias).astype(BF16)
                ps_ref[half * TQ:(half + 1) * TQ, k0:k0 + TK] = pr

        acc = jnp.dot(ps_ref[...], vx_ref[...], preferred_element_type=F32)
        o1 = acc[0:TQ, 0:LANES] / acc[0:TQ, LANES:LANES + 1]
        o2 = acc[TQ:2 * TQ, 0:LANES] / acc[TQ:2 * TQ, LANES:LANES + 1]
        o = o1 - lam * o2
        ms = jnp.mean(o * o, axis=-1, keepdims=True)
        o = o * lax.rsqrt(ms + SUBLN_EPS) * sg_ref[...] * (1.0 - LAMBDA_INIT)
        o_ref[0, 0, pl.ds(q0, TQ), :] = o.astype(BF16)
        return jnp.where(jnp.isfinite(o), 0.0, 1.0)

    def fast_loop(qi, bad):
        for u in range(DA_GROUP):
            q0 = pl.multiple_of((qi * DA_GROUP + u) * TQ, TQ)
            bad = jnp.maximum(bad, q_tile(q0, exact_shift=False, slot=u))
        return bad

    bad = lax.fori_loop(0, S // (TQ * DA_GROUP), fast_loop, jnp.zeros((TQ, LANES), F32))

    @pl.when(jnp.max(bad) > 0.0)
    def _():
        def exact_loop(qi, _):
            q_tile(pl.multiple_of(qi * TQ, TQ), exact_shift=True)
            return 0
        lax.fori_loop(0, S // TQ, exact_loop, 0)


def _diff_attn(p, e_bias, lq1, lk1, lq2, lk2, subln_g):
    B, _, S, _ = p.shape
    H = N_DIFF_HEADS
    vec = lambda n: pl.BlockSpec((1, n), lambda b, h: (0, 0))
    return pl.pallas_call(
        _diff_attn_body,
        grid=(B, H),
        in_specs=[
            pl.BlockSpec((1, 1, S, LANES), lambda b, h: (b, h, 0, 0)),
            pl.BlockSpec((1, 1, S, LANES), lambda b, h: (b, H + h, 0, 0)),
            pl.BlockSpec((1, 1, S, LANES), lambda b, h: (b, 2 * H + h, 0, 0)),
            pl.BlockSpec((1, DA_TQ, DA_WTOT), lambda b, h: (h, 0, 0)),
            vec(HEAD_DIM), vec(HEAD_DIM), vec(HEAD_DIM), vec(HEAD_DIM), vec(2 * HEAD_DIM),
        ],
        out_specs=pl.BlockSpec((1, 1, S, LANES), lambda b, h: (b, h, 0, 0)),
        out_shape=jax.ShapeDtypeStruct((B, H, S, LANES), BF16),
        scratch_shapes=[
            pltpu.VMEM((S, 2 * LANES), BF16),
            pltpu.VMEM((S, 2 * LANES), BF16),
            pltpu.VMEM((DA_GROUP, 2 * DA_TQ, S), BF16),
        ],
        compiler_params=_cparams(("arbitrary", "arbitrary")),
        name="diff_attn",
    )(p, p, p, e_bias, lq1, lk1, lq2, lk2, subln_g)


def _dil_attn_body(q1_ref, k1_ref, v1_ref, q4_ref, k4_ref, v4_ref, q16_ref, k16_ref, v16_ref, m_ref,
                   o_ref, kpad_ref, vlo_ref, vhi_ref, oacc_ref, lse_ref):
    S = q1_ref.shape[2]
    lane = lax.broadcasted_iota(I32, (DL_BQ, LANES), 1)
    lo = lane < HEAD_DIM
    zpad = jnp.zeros((DL_HALF, LANES), BF16)

    def ones_fill(i, _):
        r0 = pl.multiple_of(i * 512, 512)
        lo_f = lax.broadcasted_iota(I32, (512, LANES), 1) < HEAD_DIM
        vlo_ref[pl.ds(r0, 512), LANES:2 * LANES] = jnp.where(lo_f, 1.0, 0.0).astype(BF16)
        vhi_ref[pl.ds(r0, 512), LANES:2 * LANES] = jnp.where(lo_f, 0.0, 1.0).astype(BF16)
        return 0

    lax.fori_loop(0, vlo_ref.shape[0] // 512, ones_fill, 0)

    def run_pattern(p, dil, get_q, get_k, get_v):
        L = S // dil
        nblk = L // DL_BQ
        LP = L + 2 * DL_HALF

        def fill(r, _):
            base = pl.multiple_of(r * LP, DL_HALF)
            for off in (0, DL_HALF + L):
                kpad_ref[pl.ds(base + off, DL_HALF), :] = zpad
                vlo_ref[pl.ds(base + off, DL_HALF), 0:LANES] = zpad
                vhi_ref[pl.ds(base + off, DL_HALF), 0:LANES] = zpad
            kpad_ref[pl.ds(base + DL_HALF, L), :] = get_k(r)[...]
            v = get_v(r)[...]
            lo_v = lax.broadcasted_iota(I32, v.shape, 1) < HEAD_DIM
            zero = jnp.zeros_like(v)
            vlo_ref[pl.ds(base + DL_HALF, L), 0:LANES] = jnp.where(lo_v, v, zero)
            vhi_ref[pl.ds(base + DL_HALF, L), 0:LANES] = jnp.where(lo_v, zero, v)
            return 0

        lax.fori_loop(0, dil, fill, 0)

        def group(gi, _):
            for u in range(DL_GROUP):
                n = gi * DL_GROUP + u
                r = n // nblk
                blk = n % nblk
                q0 = pl.multiple_of(blk * DL_BQ, DL_BQ)
                w0 = pl.multiple_of(r * LP + q0, DL_HALF)
                q = get_q(r)[pl.ds(q0, DL_BQ), :]
                kw = kpad_ref[pl.ds(w0, DL_WIN), :]
                vw = jnp.concatenate([vlo_ref[pl.ds(w0, DL_WIN), :], vhi_ref[pl.ds(w0, DL_WIN), :]], axis=0)
                edge = jnp.where(blk == 0, 0, jnp.where(blk == nblk - 1, 2, 1))
                zero = jnp.zeros_like(q)
                probs, maxes = [], []
                for hh in range(2):
                    qm = jnp.where(lo, q, zero) if hh == 0 else jnp.where(lo, zero, q)
                    s = lax.dot_general(qm, kw, _NT_DIMS, preferred_element_type=F32) + m_ref[p, hh, edge]
                    m = jnp.max(s, axis=-1, keepdims=True)
                    probs.append(jnp.exp2(s - m).astype(BF16))
                    maxes.append(m)
                both = jnp.dot(jnp.concatenate(probs, axis=1), vw, preferred_element_type=F32)
                l = both[:, LANES:2 * LANES]
                o = both[:, 0:LANES] / l
                lse = jnp.where(lo, maxes[0], maxes[1]) + jnp.log2(l)
                if dil == 1:
                    oacc_ref[p, pl.ds(q0, DL_BQ), :] = o
                    lse_ref[p, pl.ds(q0, DL_BQ), :] = lse
                else:
                    oacc_ref[p, pl.ds(q0 * dil + r, DL_BQ, stride=dil), :] = o
                    lse_ref[p, pl.ds(q0 * dil + r, DL_BQ, stride=dil), :] = lse
            return 0

        lax.fori_loop(0, dil * nblk // DL_GROUP, group, 0)

    run_pattern(0, 1, lambda r: q1_ref.at[0, 0], lambda r: k1_ref.at[0, 0], lambda r: v1_ref.at[0, 0])
    run_pattern(1, 4, lambda r: q4_ref.at[0, 0, r], lambda r: k4_ref.at[0, 0, r], lambda r: v4_ref.at[0, 0, r])
    run_pattern(2, 16, lambda r: q16_ref.at[0, 0, r], lambda r: k16_ref.at[0, 0, r], lambda r: v16_ref.at[0, 0, r])

    MIX = 256

    def mix(i, _):
        r0 = pl.multiple_of(i * MIX, MIX)
        ls = [lse_ref[p, pl.ds(r0, MIX), :] for p in range(3)]
        mx = jnp.maximum(jnp.maximum(ls[0], ls[1]), ls[2])
        ws = [jnp.exp2(l - mx) for l in ls]
        den = ws[0] + ws[1] + ws[2]
        num = ws[0] * oacc_ref[0, pl.ds(r0, MIX), :]
        num += ws[1] * oacc_ref[1, pl.ds(r0, MIX), :]
        num += ws[2] * oacc_ref[2, pl.ds(r0, MIX), :]
        o_ref[0, 0, pl.ds(r0, MIX), :] = (num / den).astype(BF16)
        return 0

    lax.fori_loop(0, S // MIX, mix, 0)


def _dil_attn(p, p4, p16, m_bias):
    B, _, S, _ = p.shape
    NP = N_DIL_PAIRS
    base = 3 * N_DIFF_HEADS
    nat = lambda o: pl.BlockSpec((1, 1, S, LANES), lambda b, h: (b, base + o * NP + h, 0, 0))
    res = lambda o, d: pl.BlockSpec((1, 1, d, S // d, LANES), lambda b, h: (b, o * NP + h, 0, 0, 0))
    return pl.pallas_call(
        _dil_attn_body,
        grid=(B, NP),
        in_specs=[nat(0), nat(1), nat(2), res(0, 4), res(1, 4), res(2, 4), res(0, 16), res(1, 16), res(2, 16),
                  pl.BlockSpec((3, 2, 3, DL_BQ, DL_WIN), lambda b, h: (0, h, 0, 0, 0))],
        out_specs=pl.BlockSpec((1, 1, S, LANES), lambda b, h: (b, h, 0, 0)),
        out_shape=jax.ShapeDtypeStruct((B, NP, S, LANES), BF16),
        scratch_shapes=[
            pltpu.VMEM((S + 2 * DL_HALF * DIL_PATTERNS[-1][1], LANES), BF16),
            pltpu.VMEM((S + 2 * DL_HALF * DIL_PATTERNS[-1][1], 2 * LANES), BF16),
            pltpu.VMEM((S + 2 * DL_HALF * DIL_PATTERNS[-1][1], 2 * LANES), BF16),
            pltpu.VMEM((3, S, LANES), F32),
            pltpu.VMEM((3, S, LANES), F32),
        ],
        compiler_params=_cparams(("arbitrary", "arbitrary")),
        name="dil_attn",
    )(p, p, p, p4, p4, p4, p16, p16, p16, m_bias)


OP_TM = 512


def _store_token_major(ref, val):
    rows = val.shape[0]
    for j in range(ROW_CHUNKS):
        ref[0, pl.ds(j, rows, stride=ROW_CHUNKS), :] = val[:, j * LANES:(j + 1) * LANES]


def _out_proj_body(oa_ref, od_ref, x_ref, w_ref, g_ref, wr_ref, x2_ref, h2_ref, aff_ref):
    pieces = [oa_ref[0, i] for i in range(N_DIFF_HEADS)] + [od_ref[0, i] for i in range(N_DIL_PAIRS)]
    a = jnp.concatenate(pieces, axis=-1)
    x2 = x_ref[0] + jnp.dot(a, w_ref[...], preferred_element_type=F32)
    ms = jnp.mean(x2 * x2, axis=-1, keepdims=True)
    h = x2 * lax.rsqrt(ms + NORM_EPS) * g_ref[...]
    logits = lax.dot_general(wr_ref[...], h, _NT_DIMS, preferred_element_type=F32,
                             precision=lax.Precision.HIGHEST)
    mx = jnp.max(logits, axis=0, keepdims=True)
    ex = jnp.exp(logits - mx)
    aff_ref[0] = ex / jnp.sum(ex, axis=0, keepdims=True)
    _store_token_major(x2_ref, x2)
    _store_token_major(h2_ref, h)


def _out_proj(oa, od, x, w_bf16, g, wr_t):
    B, S, _ = x.shape
    nt = S // OP_TM
    attn = lambda n: pl.BlockSpec((1, n, OP_TM, LANES), lambda b, t: (b, 0, t, 0))
    tok_major = pl.BlockSpec((1, OP_TM * ROW_CHUNKS, LANES), lambda b, t: (b, t, 0))
    return pl.pallas_call(
        _out_proj_body,
        grid=(B, nt),
        in_specs=[
            attn(N_DIFF_HEADS), attn(N_DIL_PAIRS),
            pl.BlockSpec((1, OP_TM, D_MODEL), lambda b, t: (b, t, 0)),
            pl.BlockSpec((D_MODEL, D_MODEL), lambda b, t: (0, 0)),
            pl.BlockSpec((1, D_MODEL), lambda b, t: (0, 0)),
            pl.BlockSpec((N_EXPERTS, D_MODEL), lambda b, t: (0, 0)),
        ],
        out_specs=[tok_major, tok_major, pl.BlockSpec((1, N_EXPERTS, OP_TM), lambda b, t: (b, 0, t))],
        out_shape=[
            jax.ShapeDtypeStruct((B, S * ROW_CHUNKS, LANES), F32),
            jax.ShapeDtypeStruct((B, S * ROW_CHUNKS, LANES), F32),
            jax.ShapeDtypeStruct((B, N_EXPERTS, S), F32),
        ],
        compiler_params=_cparams(("arbitrary", "arbitrary")),
        name="out_proj",
    )(oa, od, x, w_bf16, g, wr_t)


RT_GROUP = 2


def _route_body(aff_ref, idx_ref, xh_ref, xl_ref, st_ref, en_ref):
    E, S = aff_ref.shape[1], aff_ref.shape[2]
    C = idx_ref.shape[2]
    n_chunks = S // LANES
    a = aff_ref[0]
    bits = pltpu.bitcast(a, I32)
    tau = jnp.zeros((E, 1), I32)
    for bit in range(30, -1, -1):
        cand = tau | (1 << bit)
        cnt = jnp.sum(jnp.where(bits >= cand, 1.0, 0.0).astype(F32), axis=1, keepdims=True)
        tau = jnp.where(cnt >= C, cand, tau)
    gt = bits > tau
    eq = bits == tau
    need = C - jnp.sum(jnp.where(gt, 1.0, 0.0).astype(F32), axis=1, keepdims=True)

    tri = (lax.broadcasted_iota(I32, (LANES, LANES), 0) <= lax.broadcasted_iota(I32, (LANES, LANES), 1)).astype(BF16)

    def chunk_cumsum(mj, carry):
        return jnp.dot(mj.astype(BF16), tri, preferred_element_type=F32) + carry

    sel_chunks = []
    carry = jnp.zeros((E, 1), F32)
    for j in range(n_chunks):
        cs = slice(j * LANES, (j + 1) * LANES)
        eq_j = jnp.where(eq[:, cs], 1.0, 0.0).astype(F32)
        inc = chunk_cumsum(eq_j, carry)
        carry = inc[:, LANES - 1:LANES]
        take_tie = jnp.where(inc - eq_j < need, eq_j, 0.0)
        sel_chunks.append(jnp.where(gt[:, cs], 1.0, take_tie))

    lane_e = lax.broadcasted_iota(I32, (E, LANES), 1)
    st = jnp.full((E, LANES), float(C), F32)
    en = jnp.full((E, LANES), float(C), F32)
    carry = jnp.zeros((E, 1), F32)
    for j in range(n_chunks):
        sel_j = sel_chunks[j]
        inc = chunk_cumsum(sel_j, carry)
        posm = jnp.where(sel_j > 0.0, inc - sel_j, -1.0)
        hi = jnp.floor(posm * 0.5)
        xh_ref[pl.ds(j * E, E), :] = hi
        xl_ref[pl.ds(j * E, E), :] = posm - 2.0 * hi
        st = jnp.where(lane_e == j, carry, st)
        carry = inc[:, LANES - 1:LANES]
        en = jnp.where(lane_e == j, carry, en)
    pad_rows = (LANES - n_chunks) * E
    xh_ref[pl.ds(n_chunks * E, pad_rows), :] = jnp.zeros((pad_rows, LANES), F32)
    xl_ref[pl.ds(n_chunks * E, pad_rows), :] = jnp.zeros((pad_rows, LANES), F32)
    st_ref[...] = st
    en_ref[...] = en

    c_iota = lax.broadcasted_iota(I32, (C, LANES), 0).astype(F32)
    lane_row = jnp.where(lax.broadcasted_iota(I32, (SUBLANES, LANES), 0) == 0,
                         lax.broadcasted_iota(I32, (SUBLANES, LANES), 1), 0).astype(F32).astype(BF16)

    def expert_group(g, _):
        for u in range(RT_GROUP):
            e = g * RT_GROUP + u
            st_e = st_ref[pl.ds(e, 1), :]
            en_e = en_ref[pl.ds(e, 1), :]
            in_chunk = jnp.where(st_e <= c_iota, jnp.where(c_iota < en_e, 1.0, 0.0), 0.0).astype(BF16)
            xh = xh_ref[pl.ds(e, LANES, stride=E), :].astype(BF16)
            xl = xl_ref[pl.ds(e, LANES, stride=E), :].astype(BF16)
            hl = jnp.dot(in_chunk, jnp.concatenate([xh, xl], axis=1), preferred_element_type=F32)
            slot_of = 2.0 * hl[:, 0:LANES] + hl[:, LANES:2 * LANES]
            hit = jnp.where(slot_of == c_iota, 1.0, 0.0).astype(BF16)
            chunk_id = lax.dot_general(lane_row, in_chunk, _NT_DIMS, preferred_element_type=F32)
            lane_id = lax.dot_general(lane_row, hit, _NT_DIMS, preferred_element_type=F32)
            idx_ref[0, pl.ds(e, 1), :] = (chunk_id[0:1, :] * float(LANES) + lane_id[0:1, :] + 0.5).astype(I32)
        return 0

    lax.fori_loop(0, E // RT_GROUP, expert_group, 0)


def _route(aff):
    B, E, S = aff.shape
    C = EC_K * S // E
    return pl.pallas_call(
        _route_body,
        grid=(B,),
        in_specs=[pl.BlockSpec((1, E, S), lambda b: (b, 0, 0))],
        out_specs=pl.BlockSpec((1, E, C), lambda b: (b, 0, 0)),
        out_shape=jax.ShapeDtypeStruct((B, E, C), I32),
        scratch_shapes=[pltpu.VMEM((LANES * E, LANES), F32), pltpu.VMEM((LANES * E, LANES), F32),
                        pltpu.VMEM((E, LANES), F32), pltpu.VMEM((E, LANES), F32)],
        compiler_params=_cparams(("arbitrary",)),
        name="route",
    )(aff)


GA_UNROLL = 16


def _gather_body(idx_ref, h_ref, xe_ref, g_ref):
    C = idx_ref.shape[2]

    def rows(i, _):
        for u in range(GA_UNROLL):
            r = i * GA_UNROLL + u
            t = idx_ref[0, 0, r]
            src = pl.multiple_of(t * ROW_CHUNKS, ROW_CHUNKS)
            dst = pl.multiple_of(r * ROW_CHUNKS, ROW_CHUNKS)
            g_ref[pl.ds(dst, ROW_CHUNKS), :] = h_ref[0, pl.ds(src, ROW_CHUNKS), :]
        return 0

    lax.fori_loop(0, C // GA_UNROLL, rows, 0)
    for j in range(ROW_CHUNKS):
        xe_ref[0, :, j * LANES:(j + 1) * LANES] = g_ref[pl.ds(j, C, stride=ROW_CHUNKS), :].astype(BF16)


def _gather(idx, h2):
    B, E, C = idx.shape
    SR = h2.shape[1]
    return pl.pallas_call(
        _gather_body,
        grid=(B, E),
        in_specs=[
            pl.BlockSpec((1, 1, C), lambda b, e: (b * E + e, 0, 0), memory_space=pltpu.SMEM),
            pl.BlockSpec((1, SR, LANES), lambda b, e: (b, 0, 0), pipeline_mode=pl.Buffered(1)),
        ],
        out_specs=pl.BlockSpec((1, C, D_MODEL), lambda b, e: (e, b, 0)),
        out_shape=jax.ShapeDtypeStruct((E, B * C, D_MODEL), BF16),
        scratch_shapes=[pltpu.VMEM((C * ROW_CHUNKS, LANES), F32)],
        compiler_params=_cparams(("arbitrary", "arbitrary")),
        name="gather",
    )(idx.reshape(B * E, 1, C), h2)


FF_TM = 512
FF_TF = 1408
FF_CAST_ROWS = 128


def _cast_weight(src_ref, dst_ref):
    rows = dst_ref.shape[0]

    def chunk(i, _):
        r0 = pl.multiple_of(i * FF_CAST_ROWS, FF_CAST_ROWS)
        dst_ref[pl.ds(r0, FF_CAST_ROWS), :] = src_ref[0, pl.ds(r0, FF_CAST_ROWS), :].astype(BF16)
        return 0

    lax.fori_loop(0, rows // FF_CAST_ROWS, chunk, 0)


def _ffn_up_body(x_ref, wg_ref, wu_ref, he_ref, wgb_ref, wub_ref):
    @pl.when(pl.program_id(2) == 0)
    def _():
        _cast_weight(wg_ref, wgb_ref)
        _cast_weight(wu_ref, wub_ref)

    x = x_ref[0]
    hg = jnp.dot(x, wgb_ref[...], preferred_element_type=F32)
    hu = jnp.dot(x, wub_ref[...], preferred_element_type=F32)
    he_ref[0] = (hg * jax.nn.sigmoid(hg) * hu).astype(BF16)


def _ffn_down_body(he_ref, wd_ref, y_ref, wdb_ref):
    @pl.when(pl.program_id(1) == 0)
    def _():
        _cast_weight(wd_ref, wdb_ref)

    _store_token_major(y_ref, jnp.dot(he_ref[0], wdb_ref[...], preferred_element_type=F32))


def _ffn(xe, wg, wu, wd):
    E, M, _ = xe.shape
    nm = M // FF_TM
    nf = D_FF // FF_TF
    he = pl.pallas_call(
        _ffn_up_body,
        grid=(E, nf, nm),
        in_specs=[
            pl.BlockSpec((1, FF_TM, D_MODEL), lambda e, f, m: (e, m, 0)),
            pl.BlockSpec((1, D_MODEL, FF_TF), lambda e, f, m: (e, 0, f)),
            pl.BlockSpec((1, D_MODEL, FF_TF), lambda e, f, m: (e, 0, f)),
        ],
        out_specs=pl.BlockSpec((1, FF_TM, FF_TF), lambda e, f, m: (e, m, f)),
        out_shape=jax.ShapeDtypeStruct((E, M, D_FF), BF16),
        scratch_shapes=[pltpu.VMEM((D_MODEL, FF_TF), BF16), pltpu.VMEM((D_MODEL, FF_TF), BF16)],
        compiler_params=_cparams(("arbitrary", "arbitrary", "arbitrary")),
        name="expert_up",
    )(xe, wg, wu)
    return pl.pallas_call(
        _ffn_down_body,
        grid=(E, nm),
        in_specs=[
            pl.BlockSpec((1, FF_TM, D_FF), lambda e, m: (e, m, 0)),
            pl.BlockSpec((1, D_FF, D_MODEL), lambda e, m: (e, 0, 0)),
        ],
        out_specs=pl.BlockSpec((1, FF_TM * ROW_CHUNKS, LANES), lambda e, m: (e, m, 0)),
        out_shape=jax.ShapeDtypeStruct((E, M * ROW_CHUNKS, LANES), F32),
        scratch_shapes=[pltpu.VMEM((D_FF, D_MODEL), BF16)],
        compiler_params=_cparams(("arbitrary", "arbitrary")),
        name="expert_down",
    )(he, wd)


CB_UNROLL = 8
CB_TM = 512


def _combine_body(idx_ref, aff_ref, y_ref, x2_ref, g_ref, o_ref, acc_ref, n_ref):
    E = N_EXPERTS
    C = idx_ref.shape[2]
    s = pl.program_id(1)

    @pl.when(s == 0)
    def _():
        def zero(i, _):
            r0 = pl.multiple_of(i * 1024, 1024)
            acc_ref[pl.ds(r0, 1024), :] = jnp.zeros((1024, LANES), F32)
            return 0
        lax.fori_loop(0, acc_ref.shape[0] // 1024, zero, 0)

    @pl.when(s < E)
    def _():
        def rows(i, _):
            dsts, sums = [], []
            for u in range(CB_UNROLL):
                r = i * CB_UNROLL + u
                t = idx_ref[0, 0, r]
                gate = aff_ref[0, 0, t]
                dst = pl.multiple_of(t * ROW_CHUNKS, ROW_CHUNKS)
                src = pl.multiple_of(r * ROW_CHUNKS, ROW_CHUNKS)
                dsts.append(dst)
                sums.append(acc_ref[pl.ds(dst, ROW_CHUNKS), :] + gate * y_ref[0, pl.ds(src, ROW_CHUNKS), :])
            for dst, v in zip(dsts, sums):
                acc_ref[pl.ds(dst, ROW_CHUNKS), :] = v
            return 0
        lax.fori_loop(0, C // CB_UNROLL, rows, 0)

    @pl.when(s >= E)
    def _():
        r0 = pl.multiple_of((s - E) * (CB_TM * ROW_CHUNKS), CB_TM * ROW_CHUNKS)
        v = x2_ref[0] + acc_ref[pl.ds(r0, CB_TM * ROW_CHUNKS), :]
        v3 = v.reshape(CB_TM, ROW_CHUNKS, LANES)
        ss = jnp.sum(jnp.sum(v3 * v3, axis=2, keepdims=True), axis=1, keepdims=True)
        n3 = v3 * lax.rsqrt(ss * (1.0 / D_MODEL) + NORM_EPS) * g_ref[...][None]
        n_ref[...] = n3.reshape(CB_TM * ROW_CHUNKS, LANES)
        for j in range(ROW_CHUNKS):
            o_ref[0, :, j * LANES:(j + 1) * LANES] = n_ref[pl.ds(j, CB_TM, stride=ROW_CHUNKS), :]


def _combine(idx, aff, ye, x2, g_f):
    B, E, C = idx.shape
    SR = x2.shape[1]
    S = SR // ROW_CHUNKS
    nfin = S // CB_TM
    return pl.pallas_call(
        _combine_body,
        grid=(B, E + nfin),
        in_specs=[
            pl.BlockSpec((1, 1, C), lambda b, s: (b * E + jnp.minimum(s, E - 1), 0, 0), memory_space=pltpu.SMEM),
            pl.BlockSpec((1, 1, S), lambda b, s: (b * E + jnp.minimum(s, E - 1), 0, 0), memory_space=pltpu.SMEM),
            pl.BlockSpec((1, C * ROW_CHUNKS, LANES), lambda b, s: (jnp.minimum(s, E - 1), b, 0)),
            pl.BlockSpec((1, CB_TM * ROW_CHUNKS, LANES), lambda b, s: (b, jnp.maximum(s - E, 0), 0)),
            pl.BlockSpec((ROW_CHUNKS, LANES), lambda b, s: (0, 0)),
        ],
        out_specs=pl.BlockSpec((1, CB_TM, D_MODEL), lambda b, s: (b, jnp.maximum(s - E, 0), 0)),
        out_shape=jax.ShapeDtypeStruct((B, S, D_MODEL), F32),
        scratch_shapes=[pltpu.VMEM((SR, LANES), F32), pltpu.VMEM((CB_TM * ROW_CHUNKS, LANES), F32)],
        compiler_params=_cparams(("arbitrary", "arbitrary")),
        name="combine",
    )(idx.reshape(B * E, 1, C), aff.reshape(B * E, 1, S), ye, x2, g_f)


def kernel(x, norm1_g, w_in, lam_q1, lam_k1, lam_q2, lam_k2, subln_g, w_out, rel_bias,
           norm2_g, w_router, w_gate, w_up, w_down, norm_f_g):
    B, S, _ = x.shape
    C = EC_K * S // N_EXPERTS
    e_bias = _diff_bias(rel_bias)
    m_bias = _dil_bias(rel_bias)
    p, p4, p16 = _in_proj(x, norm1_g, w_in[0].astype(BF16))
    oa = _diff_attn(p, e_bias, lam_q1, lam_k1, lam_q2, lam_k2, subln_g)
    od = _dil_attn(p, p4, p16, m_bias)
    x2, h2, aff = _out_proj(oa, od, x, w_out[0].astype(BF16), norm2_g, w_router[0].T)
    idx = _route(aff)
    xe = _gather(idx, h2)
    ye = _ffn(xe, w_gate[0], w_up[0], w_down[0])
    return _combine(idx, aff, ye, x2, norm_f_g.reshape(ROW_CHUNKS, LANES))
```

```python
import functools
import math

import jax
import jax.numpy as jnp
from jax import lax
from jax.experimental import pallas as pl
from jax.experimental.pallas import tpu as pltpu

F32 = jnp.float32
BF16 = jnp.bfloat16
I32 = jnp.int32

D_MODEL = 1024
HEAD_DIM = 64
N_DIFF_HEADS = 4
N_DIL_HEADS = 8
N_DIL_PAIRS = N_DIL_HEADS // 2
DIFF_WIDTH = 512
DIL_WIDTH = 512
N_IN = 3 * DIFF_WIDTH + 3 * DIL_WIDTH
DIL_PATTERNS = ((128, 1), (512, 4), (2048, 16))
N_BUCKETS = 32
REL_MAX_DIST = 1024
N_EXPERTS = 16
EC_K = 2
D_FF = 2816
NORM_EPS = 1e-6
SUBLN_EPS = 1e-5
NEG = -1e30
LAMBDA_INIT = 0.8 - 0.6 * math.exp(0.0)

LANES = 128
SUBLANES = 8
ROW_CHUNKS = D_MODEL // LANES

DA_TQ = 256
DA_TK = 512
DA_GROUP = 4
DA_EST = 128
DA_DMIN = -1280
DA_DMAX = 1024
DA_WTOT = DA_DMAX - DA_DMIN + DA_TK

DL_BQ = 128
DL_HALF = 64
DL_WIN = DL_BQ + 2 * DL_HALF
DL_GROUP = 8

LOG2E = math.log2(math.e)

VMEM_LIMIT = 56 * 1024 * 1024


def _cparams(sem):
    return pltpu.CompilerParams(dimension_semantics=sem, vmem_limit_bytes=VMEM_LIMIT)


def _t5_bucket(rel):
    half = N_BUCKETS // 2
    max_exact = half // 2
    n = jnp.abs(rel)
    nf = jnp.maximum(n, 1).astype(F32)
    large = max_exact + (jnp.log(nf / max_exact) / math.log(REL_MAX_DIST / max_exact)
                         * (half - max_exact)).astype(I32)
    large = jnp.minimum(large, half - 1)
    return jnp.where(rel > 0, half, 0) + jnp.where(n < max_exact, n, large)


def _bias_lookup(rel, tab_ref, col):
    bucket = _t5_bucket(rel)
    acc = jnp.zeros(rel.shape, F32)
    for j in range(N_BUCKETS):
        acc = jnp.where(bucket == j, tab_ref[j, col], acc)
    return acc


def _diff_bias_body(tab_ref, e_ref):
    h = pl.program_id(0)

    def chunk(cc, _):
        c0 = pl.multiple_of(cc * LANES, LANES)
        i = lax.broadcasted_iota(I32, (DA_TQ, LANES), 0)
        c = lax.broadcasted_iota(I32, (DA_TQ, LANES), 1) + c0
        e_ref[0, :, pl.ds(c0, LANES)] = _bias_lookup(c - i + DA_DMIN, tab_ref, h) * LOG2E
        return 0

    lax.fori_loop(0, DA_WTOT // LANES, chunk, 0)


def _diff_bias(rel_bias):
    return pl.pallas_call(
        _diff_bias_body,
        grid=(N_DIFF_HEADS,),
        in_specs=[pl.BlockSpec(memory_space=pltpu.SMEM)],
        out_specs=pl.BlockSpec((1, DA_TQ, DA_WTOT), lambda h: (h, 0, 0)),
        out_shape=jax.ShapeDtypeStruct((N_DIFF_HEADS, DA_TQ, DA_WTOT), F32),
        compiler_params=_cparams(("arbitrary",)),
        name="diff_bias",
    )(rel_bias)


def _dil_bias_body(tab_ref, m_ref):
    p = pl.program_id(0)
    h = pl.program_id(1)
    dil = jnp.where(p == 0, DIL_PATTERNS[0][1], jnp.where(p == 1, DIL_PATTERNS[1][1], DIL_PATTERNS[2][1]))
    i = lax.broadcasted_iota(I32, (DL_BQ, DL_WIN), 0)
    c = lax.broadcasted_iota(I32, (DL_BQ, DL_WIN), 1)
    off = c - DL_HALF - i
    bias = _bias_lookup(off * dil, tab_ref, N_DIFF_HEADS + h) * LOG2E
    band = jnp.abs(off) <= DL_HALF
    m_ref[0, 0, 0] = jnp.where(band & (c >= DL_HALF), bias, NEG)
    m_ref[0, 0, 1] = jnp.where(band, bias, NEG)
    m_ref[0, 0, 2] = jnp.where(band & (c < DL_HALF + DL_BQ), bias, NEG)


def _dil_bias(rel_bias):
    n_pat = len(DIL_PATTERNS)
    return pl.pallas_call(
        _dil_bias_body,
        grid=(n_pat, N_DIL_HEADS),
        in_specs=[pl.BlockSpec(memory_space=pltpu.SMEM)],
        out_specs=pl.BlockSpec((1, 1, 3, DL_BQ, DL_WIN), lambda p, h: (p, h, 0, 0, 0)),
        out_shape=jax.ShapeDtypeStruct((n_pat, N_DIL_HEADS, 3, DL_BQ, DL_WIN), F32),
        compiler_params=_cparams(("arbitrary", "arbitrary")),
        name="dil_bias",
    )(rel_bias)


IP_TM = 512
IP_TN = 256
N_COLBLK = N_IN // LANES
N_DIL_COLBLK = 3 * DIL_WIDTH // LANES


def _in_proj_body(x_ref, g_ref, w_ref, p_ref, p4_ref, p16_ref, scr_ref):
    q_scale = HEAD_DIM ** -0.5 * LOG2E
    x = x_ref[0]
    ms = jnp.mean(x * x, axis=-1, keepdims=True)
    y = (x * lax.rsqrt(ms + NORM_EPS) * g_ref[...]).astype(BF16)
    for n in range(N_IN // IP_TN):
        c0 = n * IP_TN
        r = jnp.dot(y, w_ref[:, c0:c0 + IP_TN], preferred_element_type=F32)
        is_q = (c0 < DIFF_WIDTH) or (3 * DIFF_WIDTH <= c0 < 3 * DIFF_WIDTH + DIL_WIDTH)
        if is_q:
            r = r * q_scale
        for k in range(IP_TN // LANES):
            p_ref[0, 2 * n + k] = r[:, k * LANES:(k + 1) * LANES].astype(BF16)
        if c0 >= 3 * DIFF_WIDTH:
            for k in range(IP_TN // LANES):
                scr_ref[(c0 - 3 * DIFF_WIDTH) // LANES + k] = r[:, k * LANES:(k + 1) * LANES]
    for c in range(N_DIL_COLBLK):
        for d, dst_ref in ((4, p4_ref), (16, p16_ref)):
            for r_ in range(d):
                dst_ref[0, c, r_] = scr_ref[c, pl.ds(r_, IP_TM // d, stride=d), :].astype(BF16)


def _in_proj(x, g, w_bf16):
    B, S, _ = x.shape
    nt = S // IP_TM
    return pl.pallas_call(
        _in_proj_body,
        grid=(B, nt),
        in_specs=[
            pl.BlockSpec((1, IP_TM, D_MODEL), lambda b, t: (b, t, 0)),
            pl.BlockSpec((1, D_MODEL), lambda b, t: (0, 0)),
            pl.BlockSpec((D_MODEL, N_IN), lambda b, t: (0, 0)),
        ],
        out_specs=[
            pl.BlockSpec((1, N_COLBLK, IP_TM, LANES), lambda b, t: (b, 0, t, 0)),
            pl.BlockSpec((1, N_DIL_COLBLK, 4, IP_TM // 4, LANES), lambda b, t: (b, 0, 0, t, 0)),
            pl.BlockSpec((1, N_DIL_COLBLK, 16, IP_TM // 16, LANES), lambda b, t: (b, 0, 0, t, 0)),
        ],
        out_shape=[
            jax.ShapeDtypeStruct((B, N_COLBLK, S, LANES), BF16),
            jax.ShapeDtypeStruct((B, N_DIL_COLBLK, 4, S // 4, LANES), BF16),
            jax.ShapeDtypeStruct((B, N_DIL_COLBLK, 16, S // 16, LANES), BF16),
        ],
        scratch_shapes=[pltpu.VMEM((N_DIL_COLBLK, IP_TM, LANES), F32)],
        compiler_params=_cparams(("arbitrary", "arbitrary")),
        name="in_proj",
    )(x, g, w_bf16)


_NT_DIMS = (((1,), (1,)), ((), ()))


def _diff_attn_body(q_ref, k_ref, v_ref, e_ref, lq1_ref, lk1_ref, lq2_ref, lk2_ref, sg_ref, o_ref,
                    kx_ref, vx_ref, p_ref):
    S = q_ref.shape[2]
    TQ, TK = DA_TQ, DA_TK
    lam = (jnp.exp(jnp.sum(lq1_ref[...] * lk1_ref[...], axis=-1, keepdims=True))
           - jnp.exp(jnp.sum(lq2_ref[...] * lk2_ref[...], axis=-1, keepdims=True)) + LAMBDA_INIT)
    lane_q = lax.broadcasted_iota(I32, (TQ, LANES), 1)
    lo = lane_q < HEAD_DIM
    lane2 = lax.broadcasted_iota(I32, (2 * TQ, LANES), 1)

    def fill(i, _):
        r0 = pl.multiple_of(i * 512, 512)
        e0 = jnp.where(lax.broadcasted_iota(I32, (512, LANES), 1) == 0, 1.0, 0.0).astype(BF16)
        kx_ref[0:LANES, pl.ds(r0, 512)] = k_ref[0, 0, pl.ds(r0, 512), :].T
        kx_ref[LANES:2 * LANES, pl.ds(r0, 512)] = jnp.where(
            lax.broadcasted_iota(I32, (LANES, 512), 0) == 0, 1.0, 0.0).astype(BF16)
        vx_ref[pl.ds(r0, 512), 0:LANES] = v_ref[0, 0, pl.ds(r0, 512), :]
        vx_ref[pl.ds(r0, 512), LANES:2 * LANES] = e0
        return 0

    lax.fori_loop(0, S // 512, fill, 0)

    def q_tile(q0, exact_shift, slot=0):
        ps_ref = p_ref.at[slot]
        q = q_ref[0, 0, pl.ds(q0, TQ), :]
        zero = jnp.zeros_like(q)
        qa = jnp.concatenate([jnp.where(lo, q, zero), jnp.where(lo, zero, q)], axis=0)

        def bias_tile(k0):
            off = pl.multiple_of(jnp.clip(k0 - q0, DA_DMIN, DA_DMAX) - DA_DMIN, LANES)
            return e_ref[0, :, pl.ds(off, TK)]

        def tile_max(k0, pm, width=TK):
            s = jnp.dot(qa, kx_ref[0:LANES, pl.ds(k0, width)], preferred_element_type=F32)
            bias = bias_tile(k0)[:, 0:width]
            out = []
            for half in range(2):
                sb = s[half * TQ:(half + 1) * TQ] + bias
                t = pm[half]
                for j in range(width // LANES):
                    t = jnp.maximum(t, sb[:, j * LANES:(j + 1) * LANES])
                out.append(t)
            return tuple(out)

        neg = jnp.full((TQ, LANES), -jnp.inf, F32)
        if exact_shift:
            pm = lax.fori_loop(0, S // TK, lambda ki, pm: tile_max(pl.multiple_of(ki * TK, TK), pm), (neg, neg))
        else:
            pm = tile_max(0, (neg, neg), width=DA_EST)
        m = jnp.max(jnp.concatenate(pm, axis=0), axis=-1, keepdims=True)
        mcol = jnp.where(lane2 == 0, -m, 0.0).astype(BF16)
        qx = jnp.concatenate([qa, mcol], axis=1)

        for ki in range(S // TK):
            k0 = ki * TK
            s = jnp.dot(qx, kx_ref[:, k0:k0 + TK], preferred_element_type=F32)
            bias = bias_tile(k0)
            for half in range(2):
                pr = jnp.exp2(s[half * TQ:(half + 1) * TQ] + bias).astype(BF16)
                ps_ref[half * TQ:(half + 1) * TQ, k0:k0 + TK] = pr

        acc = jnp.dot(ps_ref[...], vx_ref[...], preferred_element_type=F32)
        o1 = acc[0:TQ, 0:LANES] / acc[0:TQ, LANES:LANES + 1]
        o2 = acc[TQ:2 * TQ, 0:LANES] / acc[TQ:2 * TQ, LANES:LANES + 1]
        o = o1 - lam * o2
        ms = jnp.mean(o * o, axis=-1, keepdims=True)
        o = o * lax.rsqrt(ms + SUBLN_EPS) * sg_ref[...] * (1.0 - LAMBDA_INIT)
        o_ref[0, 0, pl.ds(q0, TQ), :] = o.astype(BF16)
        return jnp.where(jnp.isfinite(o), 0.0, 1.0)

    def fast_loop(qi, bad):
        for u in range(DA_GROUP):
            q0 = pl.multiple_of((qi * DA_GROUP + u) * TQ, TQ)
            bad = jnp.maximum(bad, q_tile(q0, exact_shift=False, slot=u))
        return bad

    bad = lax.fori_loop(0, S // (TQ * DA_GROUP), fast_loop, jnp.zeros((TQ, LANES), F32))

    @pl.when(jnp.max(bad) > 0.0)
    def _():
        def exact_loop(qi, _):
            q_tile(pl.multiple_of(qi * TQ, TQ), exact_shift=True)
            return 0
        lax.fori_loop(0, S // TQ, exact_loop, 0)


def _diff_attn(p, e_bias, lq1, lk1, lq2, lk2, subln_g):
    B, _, S, _ = p.shape
    H = N_DIFF_HEADS
    vec = lambda n: pl.BlockSpec((1, n), lambda b, h: (0, 0))
    return pl.pallas_call(
        _diff_attn_body,
        grid=(B, H),
        in_specs=[
            pl.BlockSpec((1, 1, S, LANES), lambda b, h: (b, h, 0, 0)),
            pl.BlockSpec((1, 1, S, LANES), lambda b, h: (b, H + h, 0, 0)),
            pl.BlockSpec((1, 1, S, LANES), lambda b, h: (b, 2 * H + h, 0, 0)),
            pl.BlockSpec((1, DA_TQ, DA_WTOT), lambda b, h: (h, 0, 0)),
            vec(HEAD_DIM), vec(HEAD_DIM), vec(HEAD_DIM), vec(HEAD_DIM), vec(2 * HEAD_DIM),
        ],
        out_specs=pl.BlockSpec((1, 1, S, LANES), lambda b, h: (b, h, 0, 0)),
        out_shape=jax.ShapeDtypeStruct((B, H, S, LANES), BF16),
        scratch_shapes=[
            pltpu.VMEM((2 * LANES, S), BF16),
            pltpu.VMEM((S, 2 * LANES), BF16),
            pltpu.VMEM((DA_GROUP, 2 * DA_TQ, S), BF16),
        ],
        compiler_params=_cparams(("arbitrary", "arbitrary")),
        name="diff_attn",
    )(p, p, p, e_bias, lq1, lk1, lq2, lk2, subln_g)


def _dil_attn_body(q1_ref, k1_ref, v1_ref, q4_ref, k4_ref, v4_ref, q16_ref, k16_ref, v16_ref, m_ref,
                   o_ref, kpad_ref, vlo_ref, vhi_ref, oacc_ref, lse_ref):
    S = q1_ref.shape[2]
    lane = lax.broadcasted_iota(I32, (DL_BQ, LANES), 1)
    lo = lane < HEAD_DIM
    zpad = jnp.zeros((DL_HALF, LANES), BF16)

    def ones_fill(i, _):
        r0 = pl.multiple_of(i * 512, 512)
        lo_f = lax.broadcasted_iota(I32, (512, LANES), 1) < HEAD_DIM
        vlo_ref[pl.ds(r0, 512), LANES:2 * LANES] = jnp.where(lo_f, 1.0, 0.0).astype(BF16)
        vhi_ref[pl.ds(r0, 512), LANES:2 * LANES] = jnp.where(lo_f, 0.0, 1.0).astype(BF16)
        return 0

    lax.fori_loop(0, vlo_ref.shape[0] // 512, ones_fill, 0)

    def run_pattern(p, dil, get_q, get_k, get_v):
        L = S // dil
        nblk = L // DL_BQ
        LP = L + 2 * DL_HALF

        def fill(r, _):
            base = pl.multiple_of(r * LP, DL_HALF)
            for off in (0, DL_HALF + L):
                kpad_ref[pl.ds(base + off, DL_HALF), :] = zpad
                vlo_ref[pl.ds(base + off, DL_HALF), 0:LANES] = zpad
                vhi_ref[pl.ds(base + off, DL_HALF), 0:LANES] = zpad
            kpad_ref[pl.ds(base + DL_HALF, L), :] = get_k(r)[...]
            v = get_v(r)[...]
            lo_v = lax.broadcasted_iota(I32, v.shape, 1) < HEAD_DIM
            zero = jnp.zeros_like(v)
            vlo_ref[pl.ds(base + DL_HALF, L), 0:LANES] = jnp.where(lo_v, v, zero)
            vhi_ref[pl.ds(base + DL_HALF, L), 0:LANES] = jnp.where(lo_v, zero, v)
            return 0

        lax.fori_loop(0, dil, fill, 0)

        def group(gi, _):
            for u in range(DL_GROUP):
                n = gi * DL_GROUP + u
                r = n // nblk
                blk = n % nblk
                q0 = pl.multiple_of(blk * DL_BQ, DL_BQ)
                w0 = pl.multiple_of(r * LP + q0, DL_HALF)
                q = get_q(r)[pl.ds(q0, DL_BQ), :]
                kw = kpad_ref[pl.ds(w0, DL_WIN), :]
                vw = jnp.concatenate([vlo_ref[pl.ds(w0, DL_WIN), :], vhi_ref[pl.ds(w0, DL_WIN), :]], axis=0)
                edge = jnp.where(blk == 0, 0, jnp.where(blk == nblk - 1, 2, 1))
                zero = jnp.zeros_like(q)
                probs, maxes = [], []
                for hh in range(2):
                    qm = jnp.where(lo, q, zero) if hh == 0 else jnp.where(lo, zero, q)
                    s = lax.dot_general(qm, kw, _NT_DIMS, preferred_element_type=F32) + m_ref[p, hh, edge]
                    m = jnp.max(s, axis=-1, keepdims=True)
                    probs.append(jnp.exp2(s - m).astype(BF16))
                    maxes.append(m)
                both = jnp.dot(jnp.concatenate(probs, axis=1), vw, preferred_element_type=F32)
                l = both[:, LANES:2 * LANES]
                o = both[:, 0:LANES] / l
                lse = jnp.where(lo, maxes[0], maxes[1]) + jnp.log2(l)
                if dil == 1:
                    oacc_ref[p, pl.ds(q0, DL_BQ), :] = o
                    lse_ref[p, pl.ds(q0, DL_BQ), :] = lse
                else:
                    oacc_ref[p, pl.ds(q0 * dil + r, DL_BQ, stride=dil), :] = o
                    lse_ref[p, pl.ds(q0 * dil + r, DL_BQ, stride=dil), :] = lse
            return 0

        lax.fori_loop(0, dil * nblk // DL_GROUP, group, 0)

    run_pattern(0, 1, lambda r: q1_ref.at[0, 0], lambda r: k1_ref.at[0, 0], lambda r: v1_ref.at[0, 0])
    run_pattern(1, 4, lambda r: q4_ref.at[0, 0, r], lambda r: k4_ref.at[0, 0, r], lambda r: v4_ref.at[0, 0, r])
    run_pattern(2, 16, lambda r: q16_ref.at[0, 0, r], lambda r: k16_ref.at[0, 0, r], lambda r: v16_ref.at[0, 0, r])

    MIX = 256

    def mix(i, _):
        r0 = pl.multiple_of(i * MIX, MIX)
        ls = [lse_ref[p, pl.ds(r0, MIX), :] for p in range(3)]
        mx = jnp.maximum(jnp.maximum(ls[0], ls[1]), ls[2])
        ws = [jnp.exp2(l - mx) for l in ls]
        den = ws[0] + ws[1] + ws[2]
        num = ws[0] * oacc_ref[0, pl.ds(r0, MIX), :]
        num += ws[1] * oacc_ref[1, pl.ds(r0, MIX), :]
        num += ws[2] * oacc_ref[2, pl.ds(r0, MIX), :]
        o_ref[0, 0, pl.ds(r0, MIX), :] = (num / den).astype(BF16)
        return 0

    lax.fori_loop(0, S // MIX, mix, 0)


def _dil_attn(p, p4, p16, m_bias):
    B, _, S, _ = p.shape
    NP = N_DIL_PAIRS
    base = 3 * N_DIFF_HEADS
    nat = lambda o: pl.BlockSpec((1, 1, S, LANES), lambda b, h: (b, base + o * NP + h, 0, 0))
    res = lambda o, d: pl.BlockSpec((1, 1, d, S // d, LANES), lambda b, h: (b, o * NP + h, 0, 0, 0))
    return pl.pallas_call(
        _dil_attn_body,
        grid=(B, NP),
        in_specs=[nat(0), nat(1), nat(2), res(0, 4), res(1, 4), res(2, 4), res(0, 16), res(1, 16), res(2, 16),
                  pl.BlockSpec((3, 2, 3, DL_BQ, DL_WIN), lambda b, h: (0, h, 0, 0, 0))],
        out_specs=pl.BlockSpec((1, 1, S, LANES), lambda b, h: (b, h, 0, 0)),
        out_shape=jax.ShapeDtypeStruct((B, NP, S, LANES), BF16),
        scratch_shapes=[
            pltpu.VMEM((S + 2 * DL_HALF * DIL_PATTERNS[-1][1], LANES), BF16),
            pltpu.VMEM((S + 2 * DL_HALF * DIL_PATTERNS[-1][1], 2 * LANES), BF16),
            pltpu.VMEM((S + 2 * DL_HALF * DIL_PATTERNS[-1][1], 2 * LANES), BF16),
            pltpu.VMEM((3, S, LANES), F32),
            pltpu.VMEM((3, S, LANES), F32),
        ],
        compiler_params=_cparams(("arbitrary", "arbitrary")),
        name="dil_attn",
    )(p, p, p, p4, p4, p4, p16, p16, p16, m_bias)


OP_TM = 512


def _store_token_major(ref, val):
    rows = val.shape[0]
    for j in range(ROW_CHUNKS):
        ref[0, pl.ds(j, rows, stride=ROW_CHUNKS), :] = val[:, j * LANES:(j + 1) * LANES]


def _out_proj_body(oa_ref, od_ref, x_ref, w_ref, g_ref, wr_ref, x2_ref, h2_ref, aff_ref):
    pieces = [oa_ref[0, i] for i in range(N_DIFF_HEADS)] + [od_ref[0, i] for i in range(N_DIL_PAIRS)]
    a = jnp.concatenate(pieces, axis=-1)
    x2 = x_ref[0] + jnp.dot(a, w_ref[...], preferred_element_type=F32)
    ms = jnp.mean(x2 * x2, axis=-1, keepdims=True)
    h = x2 * lax.rsqrt(ms + NORM_EPS) * g_ref[...]
    logits = lax.dot_general(wr_ref[...], h, _NT_DIMS, preferred_element_type=F32,
                             precision=lax.Precision.HIGHEST)
    mx = jnp.max(logits, axis=0, keepdims=True)
    ex = jnp.exp(logits - mx)
    aff_ref[0] = ex / jnp.sum(ex, axis=0, keepdims=True)
    _store_token_major(x2_ref, x2)
    _store_token_major(h2_ref, h)


def _out_proj(oa, od, x, w_bf16, g, wr_t):
    B, S, _ = x.shape
    nt = S // OP_TM
    attn = lambda n: pl.BlockSpec((1, n, OP_TM, LANES), lambda b, t: (b, 0, t, 0))
    tok_major = pl.BlockSpec((1, OP_TM * ROW_CHUNKS, LANES), lambda b, t: (b, t, 0))
    return pl.pallas_call(
        _out_proj_body,
        grid=(B, nt),
        in_specs=[
            attn(N_DIFF_HEADS), attn(N_DIL_PAIRS),
            pl.BlockSpec((1, OP_TM, D_MODEL), lambda b, t: (b, t, 0)),
            pl.BlockSpec((D_MODEL, D_MODEL), lambda b, t: (0, 0)),
            pl.BlockSpec((1, D_MODEL), lambda b, t: (0, 0)),
            pl.BlockSpec((N_EXPERTS, D_MODEL), lambda b, t: (0, 0)),
        ],
        out_specs=[tok_major, tok_major, pl.BlockSpec((1, N_EXPERTS, OP_TM), lambda b, t: (b, 0, t))],
        out_shape=[
            jax.ShapeDtypeStruct((B, S * ROW_CHUNKS, LANES), F32),
            jax.ShapeDtypeStruct((B, S * ROW_CHUNKS, LANES), F32),
            jax.ShapeDtypeStruct((B, N_EXPERTS, S), F32),
        ],
        compiler_params=_cparams(("arbitrary", "arbitrary")),
        name="out_proj",
    )(oa, od, x, w_bf16, g, wr_t)


RT_GROUP = 2


def _route_body(aff_ref, idx_ref, xh_ref, xl_ref, st_ref, en_ref):
    E, S = aff_ref.shape[1], aff_ref.shape[2]
    C = idx_ref.shape[2]
    n_chunks = S // LANES
    a = aff_ref[0]
    bits = pltpu.bitcast(a, I32)
    tau = jnp.zeros((E, 1), I32)
    for bit in range(30, -1, -1):
        cand = tau | (1 << bit)
        cnt = jnp.sum(jnp.where(bits >= cand, 1.0, 0.0).astype(F32), axis=1, keepdims=True)
        tau = jnp.where(cnt >= C, cand, tau)
    gt = bits > tau
    eq = bits == tau
    need = C - jnp.sum(jnp.where(gt, 1.0, 0.0).astype(F32), axis=1, keepdims=True)

    tri = (lax.broadcasted_iota(I32, (LANES, LANES), 0) <= lax.broadcasted_iota(I32, (LANES, LANES), 1)).astype(BF16)

    def chunk_cumsum(mj, carry):
        return jnp.dot(mj.astype(BF16), tri, preferred_element_type=F32) + carry

    sel_chunks = []
    carry = jnp.zeros((E, 1), F32)
    for j in range(n_chunks):
        cs = slice(j * LANES, (j + 1) * LANES)
        eq_j = jnp.where(eq[:, cs], 1.0, 0.0).astype(F32)
        inc = chunk_cumsum(eq_j, carry)
        carry = inc[:, LANES - 1:LANES]
        take_tie = jnp.where(inc - eq_j < need, eq_j, 0.0)
        sel_chunks.append(jnp.where(gt[:, cs], 1.0, take_tie))

    lane_e = lax.broadcasted_iota(I32, (E, LANES), 1)
    st = jnp.full((E, LANES), float(C), F32)
    en = jnp.full((E, LANES), float(C), F32)
    carry = jnp.zeros((E, 1), F32)
    for j in range(n_chunks):
        sel_j = sel_chunks[j]
        inc = chunk_cumsum(sel_j, carry)
        posm = jnp.where(sel_j > 0.0, inc - sel_j, -1.0)
        hi = jnp.floor(posm * 0.5)
        xh_ref[pl.ds(j * E, E), :] = hi
        xl_ref[pl.ds(j * E, E), :] = posm - 2.0 * hi
        st = jnp.where(lane_e == j, carry, st)
        carry = inc[:, LANES - 1:LANES]
        en = jnp.where(lane_e == j, carry, en)
    pad_rows = (LANES - n_chunks) * E
    xh_ref[pl.ds(n_chunks * E, pad_rows), :] = jnp.zeros((pad_rows, LANES), F32)
    xl_ref[pl.ds(n_chunks * E, pad_rows), :] = jnp.zeros((pad_rows, LANES), F32)
    st_ref[...] = st
    en_ref[...] = en

    c_iota = lax.broadcasted_iota(I32, (C, LANES), 0).astype(F32)
    lane_row = jnp.where(lax.broadcasted_iota(I32, (SUBLANES, LANES), 0) == 0,
                         lax.broadcasted_iota(I32, (SUBLANES, LANES), 1), 0).astype(F32).astype(BF16)

    def expert_group(g, _):
        for u in range(RT_GROUP):
            e = g * RT_GROUP + u
            st_e = st_ref[pl.ds(e, 1), :]
            en_e = en_ref[pl.ds(e, 1), :]
            in_chunk = jnp.where(st_e <= c_iota, jnp.where(c_iota < en_e, 1.0, 0.0), 0.0).astype(BF16)
            xh = xh_ref[pl.ds(e, LANES, stride=E), :].astype(BF16)
            xl = xl_ref[pl.ds(e, LANES, stride=E), :].astype(BF16)
            hl = jnp.dot(in_chunk, jnp.concatenate([xh, xl], axis=1), preferred_element_type=F32)
            slot_of = 2.0 * hl[:, 0:LANES] + hl[:, LANES:2 * LANES]
            hit = jnp.where(slot_of == c_iota, 1.0, 0.0).astype(BF16)
            chunk_id = lax.dot_general(lane_row, in_chunk, _NT_DIMS, preferred_element_type=F32)
            lane_id = lax.dot_general(lane_row, hit, _NT_DIMS, preferred_element_type=F32)
            idx_ref[0, pl.ds(e, 1), :] = (chunk_id[0:1, :] * float(LANES) + lane_id[0:1, :] + 0.5).astype(I32)
        return 0

    lax.fori_loop(0, E // RT_GROUP, expert_group, 0)


def _route(aff):
    B, E, S = aff.shape
    C = EC_K * S // E
    return pl.pallas_call(
        _route_body,
        grid=(B,),
        in_specs=[pl.BlockSpec((1, E, S), lambda b: (b, 0, 0))],
        out_specs=pl.BlockSpec((1, E, C), lambda b: (b, 0, 0)),
        out_shape=jax.ShapeDtypeStruct((B, E, C), I32),
        scratch_shapes=[pltpu.VMEM((LANES * E, LANES), F32), pltpu.VMEM((LANES * E, LANES), F32),
                        pltpu.VMEM((E, LANES), F32), pltpu.VMEM((E, LANES), F32)],
        compiler_params=_cparams(("arbitrary",)),
        name="route",
    )(aff)


GA_UNROLL = 16


def _gather_body(idx_ref, h_ref, xe_ref, g_ref):
    C = idx_ref.shape[2]

    def rows(i, _):
        for u in range(GA_UNROLL):
            r = i * GA_UNROLL + u
            t = idx_ref[0, 0, r]
            src = pl.multiple_of(t * ROW_CHUNKS, ROW_CHUNKS)
            dst = pl.multiple_of(r * ROW_CHUNKS, ROW_CHUNKS)
            g_ref[pl.ds(dst, ROW_CHUNKS), :] = h_ref[0, pl.ds(src, ROW_CHUNKS), :]
        return 0

    lax.fori_loop(0, C // GA_UNROLL, rows, 0)
    for j in range(ROW_CHUNKS):
        xe_ref[0, :, j * LANES:(j + 1) * LANES] = g_ref[pl.ds(j, C, stride=ROW_CHUNKS), :].astype(BF16)


def _gather(idx, h2):
    B, E, C = idx.shape
    SR = h2.shape[1]
    return pl.pallas_call(
        _gather_body,
        grid=(B, E),
        in_specs=[
            pl.BlockSpec((1, 1, C), lambda b, e: (b * E + e, 0, 0), memory_space=pltpu.SMEM),
            pl.BlockSpec((1, SR, LANES), lambda b, e: (b, 0, 0), pipeline_mode=pl.Buffered(1)),
        ],
        out_specs=pl.BlockSpec((1, C, D_MODEL), lambda b, e: (e, b, 0)),
        out_shape=jax.ShapeDtypeStruct((E, B * C, D_MODEL), BF16),
        scratch_shapes=[pltpu.VMEM((C * ROW_CHUNKS, LANES), F32)],
        compiler_params=_cparams(("arbitrary", "arbitrary")),
        name="gather",
    )(idx.reshape(B * E, 1, C), h2)


FF_TM = 512
FF_TF = 1408
FF_CAST_ROWS = 128


def _cast_weight(src_ref, dst_ref):
    rows = dst_ref.shape[0]

    def chunk(i, _):
        r0 = pl.multiple_of(i * FF_CAST_ROWS, FF_CAST_ROWS)
        dst_ref[pl.ds(r0, FF_CAST_ROWS), :] = src_ref[0, pl.ds(r0, FF_CAST_ROWS), :].astype(BF16)
        return 0

    lax.fori_loop(0, rows // FF_CAST_ROWS, chunk, 0)


def _ffn_up_body(x_ref, wg_ref, wu_ref, he_ref, wgb_ref, wub_ref):
    @pl.when(pl.program_id(2) == 0)
    def _():
        _cast_weight(wg_ref, wgb_ref)
        _cast_weight(wu_ref, wub_ref)

    x = x_ref[0]
    hg = jnp.dot(x, wgb_ref[...], preferred_element_type=F32)
    hu = jnp.dot(x, wub_ref[...], preferred_element_type=F32)
    he_ref[0] = (hg * jax.nn.sigmoid(hg) * hu).astype(BF16)


def _ffn_down_body(he_ref, wd_ref, y_ref, wdb_ref):
    @pl.when(pl.program_id(1) == 0)
    def _():
        _cast_weight(wd_ref, wdb_ref)

    _store_token_major(y_ref, jnp.dot(he_ref[0], wdb_ref[...], preferred_element_type=F32))


def _ffn(xe, wg, wu, wd):
    E, M, _ = xe.shape
    nm = M // FF_TM
    nf = D_FF // FF_TF
    he = pl.pallas_call(
        _ffn_up_body,
        grid=(E, nf, nm),
        in_specs=[
            pl.BlockSpec((1, FF_TM, D_MODEL), lambda e, f, m: (e, m, 0)),
            pl.BlockSpec((1, D_MODEL, FF_TF), lambda e, f, m: (e, 0, f)),
            pl.BlockSpec((1, D_MODEL, FF_TF), lambda e, f, m: (e, 0, f)),
        ],
        out_specs=pl.BlockSpec((1, FF_TM, FF_TF), lambda e, f, m: (e, m, f)),
        out_shape=jax.ShapeDtypeStruct((E, M, D_FF), BF16),
        scratch_shapes=[pltpu.VMEM((D_MODEL, FF_TF), BF16), pltpu.VMEM((D_MODEL, FF_TF), BF16)],
        compiler_params=_cparams(("arbitrary", "arbitrary", "arbitrary")),
        name="expert_up",
    )(xe, wg, wu)
    return pl.pallas_call(
        _ffn_down_body,
        grid=(E, nm),
        in_specs=[
            pl.BlockSpec((1, FF_TM, D_FF), lambda e, m: (e, m, 0)),
            pl.BlockSpec((1, D_FF, D_MODEL), lambda e, m: (e, 0, 0)),
        ],
        out_specs=pl.BlockSpec((1, FF_TM * ROW_CHUNKS, LANES), lambda e, m: (e, m, 0)),
        out_shape=jax.ShapeDtypeStruct((E, M * ROW_CHUNKS, LANES), F32),
        scratch_shapes=[pltpu.VMEM((D_FF, D_MODEL), BF16)],
        compiler_params=_cparams(("arbitrary", "arbitrary")),
        name="expert_down",
    )(he, wd)


CB_UNROLL = 8
CB_TM = 512


def _combine_body(idx_ref, aff_ref, y_ref, x2_ref, g_ref, o_ref, acc_ref, n_ref):
    E = N_EXPERTS
    C = idx_ref.shape[2]
    s = pl.program_id(1)

    @pl.when(s == 0)
    def _():
        def zero(i, _):
            r0 = pl.multiple_of(i * 1024, 1024)
            acc_ref[pl.ds(r0, 1024), :] = jnp.zeros((1024, LANES), F32)
            return 0
        lax.fori_loop(0, acc_ref.shape[0] // 1024, zero, 0)

    @pl.when(s < E)
    def _():
        def rows(i, _):
            dsts, sums = [], []
            for u in range(CB_UNROLL):
                r = i * CB_UNROLL + u
                t = idx_ref[0, 0, r]
                gate = aff_ref[0, 0, t]
                dst = pl.multiple_of(t * ROW_CHUNKS, ROW_CHUNKS)
                src = pl.multiple_of(r * ROW_CHUNKS, ROW_CHUNKS)
                dsts.append(dst)
                sums.append(acc_ref[pl.ds(dst, ROW_CHUNKS), :] + gate * y_ref[0, pl.ds(src, ROW_CHUNKS), :])
            for dst, v in zip(dsts, sums):
                acc_ref[pl.ds(dst, ROW_CHUNKS), :] = v
            return 0
        lax.fori_loop(0, C // CB_UNROLL, rows, 0)

    @pl.when(s >= E)
    def _():
        r0 = pl.multiple_of((s - E) * (CB_TM * ROW_CHUNKS), CB_TM * ROW_CHUNKS)
        v = x2_ref[0] + acc_ref[pl.ds(r0, CB_TM * ROW_CHUNKS), :]
        v3 = v.reshape(CB_TM, ROW_CHUNKS, LANES)
        ss = jnp.sum(jnp.sum(v3 * v3, axis=2, keepdims=True), axis=1, keepdims=True)
        n3 = v3 * lax.rsqrt(ss * (1.0 / D_MODEL) + NORM_EPS) * g_ref[...][None]
        n_ref[...] = n3.reshape(CB_TM * ROW_CHUNKS, LANES)
        for j in range(ROW_CHUNKS):
            o_ref[0, :, j * LANES:(j + 1) * LANES] = n_ref[pl.ds(j, CB_TM, stride=ROW_CHUNKS), :]


def _combine(idx, aff, ye, x2, g_f):
    B, E, C = idx.shape
    SR = x2.shape[1]
    S = SR // ROW_CHUNKS
    nfin = S // CB_TM
    return pl.pallas_call(
        _combine_body,
        grid=(B, E + nfin),
        in_specs=[
            pl.BlockSpec((1, 1, C), lambda b, s: (b * E + jnp.minimum(s, E - 1), 0, 0), memory_space=pltpu.SMEM),
            pl.BlockSpec((1, 1, S), lambda b, s: (b * E + jnp.minimum(s, E - 1), 0, 0), memory_space=pltpu.SMEM),
            pl.BlockSpec((1, C * ROW_CHUNKS, LANES), lambda b, s: (jnp.minimum(s, E - 1), b, 0)),
            pl.BlockSpec((1, CB_TM * ROW_CHUNKS, LANES), lambda b, s: (b, jnp.maximum(s - E, 0), 0)),
            pl.BlockSpec((ROW_CHUNKS, LANES), lambda b, s: (0, 0)),
        ],
        out_specs=pl.BlockSpec((1, CB_TM, D_MODEL), lambda b, s: (b, jnp.maximum(s - E, 0), 0)),
        out_shape=jax.ShapeDtypeStruct((B, S, D_MODEL), F32),
        scratch_shapes=[pltpu.VMEM((SR, LANES), F32), pltpu.VMEM((CB_TM * ROW_CHUNKS, LANES), F32)],
        compiler_params=_cparams(("arbitrary", "arbitrary")),
        name="combine",
    )(idx.reshape(B * E, 1, C), aff.reshape(B * E, 1, S), ye, x2, g_f)


def kernel(x, norm1_g, w_in, lam_q1, lam_k1, lam_q2, lam_k2, subln_g, w_out, rel_bias,
           norm2_g, w_router, w_gate, w_up, w_down, norm_f_g):
    B, S, _ = x.shape
    C = EC_K * S // N_EXPERTS
    e_bias = _diff_bias(rel_bias)
    m_bias = _dil_bias(rel_bias)
    p, p4, p16 = _in_proj(x, norm1_g, w_in[0].astype(BF16))
    oa = _diff_attn(p, e_bias, lam_q1, lam_k1, lam_q2, lam_k2, subln_g)
    od = _dil_attn(p, p4, p16, m_bias)
    x2, h2, aff = _out_proj(oa, od, x, w_out[0].astype(BF16), norm2_g, w_router[0].T)
    idx = _route(aff)
    xe = _gather(idx, h2)
    ye = _ffn(xe, w_gate[0], w_up[0], w_down[0])
    return _combine(idx, aff, ye, x2, norm_f_g.reshape(ROW_CHUNKS, LANES))
```

```python
import functools
import math

import jax
import jax.numpy as jnp
from jax import lax
from jax.experimental import pallas as pl
from jax.experimental.pallas import tpu as pltpu

F32 = jnp.float32
BF16 = jnp.bfloat16
I32 = jnp.int32

D_MODEL = 1024
HEAD_DIM = 64
N_DIFF_HEADS = 4
N_DIL_HEADS = 8
N_DIL_PAIRS = N_DIL_HEADS // 2
DIFF_WIDTH = 512
DIL_WIDTH = 512
N_IN = 3 * DIFF_WIDTH + 3 * DIL_WIDTH
DIL_PATTERNS = ((128, 1), (512, 4), (2048, 16))
N_BUCKETS = 32
REL_MAX_DIST = 1024
N_EXPERTS = 16
EC_K = 2
D_FF = 2816
NORM_EPS = 1e-6
SUBLN_EPS = 1e-5
NEG = -1e30
LAMBDA_INIT = 0.8 - 0.6 * math.exp(0.0)

LANES = 128
SUBLANES = 8
ROW_CHUNKS = D_MODEL // LANES

DA_TQ = 256
DA_TK = 512
DA_GROUP = 4
DA_EST = 128
DA_DMIN = -1280
DA_DMAX = 1024
DA_WTOT = DA_DMAX - DA_DMIN + DA_TK

DL_BQ = 128
DL_HALF = 64
DL_WIN = DL_BQ + 2 * DL_HALF
DL_GROUP = 8

LOG2E = math.log2(math.e)

VMEM_LIMIT = 56 * 1024 * 1024


def _cparams(sem):
    return pltpu.CompilerParams(dimension_semantics=sem, vmem_limit_bytes=VMEM_LIMIT)


def _t5_bucket(rel):
    half = N_BUCKETS // 2
    max_exact = half // 2
    n = jnp.abs(rel)
    nf = jnp.maximum(n, 1).astype(F32)
    large = max_exact + (jnp.log(nf / max_exact) / math.log(REL_MAX_DIST / max_exact)
                         * (half - max_exact)).astype(I32)
    large = jnp.minimum(large, half - 1)
    return jnp.where(rel > 0, half, 0) + jnp.where(n < max_exact, n, large)


def _bias_lookup(rel, tab_ref, col):
    bucket = _t5_bucket(rel)
    acc = jnp.zeros(rel.shape, F32)
    for j in range(N_BUCKETS):
        acc = jnp.where(bucket == j, tab_ref[j, col], acc)
    return acc


def _diff_bias_body(tab_ref, e_ref):
    h = pl.program_id(0)

    def chunk(cc, _):
        c0 = pl.multiple_of(cc * LANES, LANES)
        i = lax.broadcasted_iota(I32, (DA_TQ, LANES), 0)
        c = lax.broadcasted_iota(I32, (DA_TQ, LANES), 1) + c0
        e_ref[0, :, pl.ds(c0, LANES)] = _bias_lookup(c - i + DA_DMIN, tab_ref, h) * LOG2E
        return 0

    lax.fori_loop(0, DA_WTOT // LANES, chunk, 0)


def _diff_bias(rel_bias):
    return pl.pallas_call(
        _diff_bias_body,
        grid=(N_DIFF_HEADS,),
        in_specs=[pl.BlockSpec(memory_space=pltpu.SMEM)],
        out_specs=pl.BlockSpec((1, DA_TQ, DA_WTOT), lambda h: (h, 0, 0)),
        out_shape=jax.ShapeDtypeStruct((N_DIFF_HEADS, DA_TQ, DA_WTOT), F32),
        compiler_params=_cparams(("arbitrary",)),
        name="diff_bias",
    )(rel_bias)


def _dil_bias_body(tab_ref, m_ref):
    p = pl.program_id(0)
    h = pl.program_id(1)
    dil = jnp.where(p == 0, DIL_PATTERNS[0][1], jnp.where(p == 1, DIL_PATTERNS[1][1], DIL_PATTERNS[2][1]))
    i = lax.broadcasted_iota(I32, (DL_BQ, DL_WIN), 0)
    c = lax.broadcasted_iota(I32, (DL_BQ, DL_WIN), 1)
    off = c - DL_HALF - i
    bias = _bias_lookup(off * dil, tab_ref, N_DIFF_HEADS + h) * LOG2E
    band = jnp.abs(off) <= DL_HALF
    m_ref[0, 0, 0] = jnp.where(band & (c >= DL_HALF), bias, NEG)
    m_ref[0, 0, 1] = jnp.where(band, bias, NEG)
    m_ref[0, 0, 2] = jnp.where(band & (c < DL_HALF + DL_BQ), bias, NEG)


def _dil_bias(rel_bias):
    n_pat = len(DIL_PATTERNS)
    return pl.pallas_call(
        _dil_bias_body,
        grid=(n_pat, N_DIL_HEADS),
        in_specs=[pl.BlockSpec(memory_space=pltpu.SMEM)],
        out_specs=pl.BlockSpec((1, 1, 3, DL_BQ, DL_WIN), lambda p, h: (p, h, 0, 0, 0)),
        out_shape=jax.ShapeDtypeStruct((n_pat, N_DIL_HEADS, 3, DL_BQ, DL_WIN), F32),
        compiler_params=_cparams(("arbitrary", "arbitrary")),
        name="dil_bias",
    )(rel_bias)


IP_TM = 512
IP_TN = 256
N_COLBLK = N_IN // LANES
N_DIL_COLBLK = 3 * DIL_WIDTH // LANES


def _in_proj_body(x_ref, g_ref, w_ref, p_ref, p4_ref, p16_ref, scr_ref):
    q_scale = HEAD_DIM ** -0.5 * LOG2E
    x = x_ref[0]
    ms = jnp.mean(x * x, axis=-1, keepdims=True)
    y = (x * lax.rsqrt(ms + NORM_EPS) * g_ref[...]).astype(BF16)
    for n in range(N_IN // IP_TN):
        c0 = n * IP_TN
        r = jnp.dot(y, w_ref[:, c0:c0 + IP_TN], preferred_element_type=F32)
        is_q = (c0 < DIFF_WIDTH) or (3 * DIFF_WIDTH <= c0 < 3 * DIFF_WIDTH + DIL_WIDTH)
        if is_q:
            r = r * q_scale
        for k in range(IP_TN // LANES):
            p_ref[0, 2 * n + k] = r[:, k * LANES:(k + 1) * LANES].astype(BF16)
        if c0 >= 3 * DIFF_WIDTH:
            for k in range(IP_TN // LANES):
                scr_ref[(c0 - 3 * DIFF_WIDTH) // LANES + k] = r[:, k * LANES:(k + 1) * LANES]
    for c in range(N_DIL_COLBLK):
        for d, dst_ref in ((4, p4_ref), (16, p16_ref)):
            for r_ in range(d):
                dst_ref[0, c, r_] = scr_ref[c, pl.ds(r_, IP_TM // d, stride=d), :].astype(BF16)


def _in_proj(x, g, w_bf16):
    B, S, _ = x.shape
    nt = S // IP_TM
    return pl.pallas_call(
        _in_proj_body,
        grid=(B, nt),
        in_specs=[
            pl.BlockSpec((1, IP_TM, D_MODEL), lambda b, t: (b, t, 0)),
            pl.BlockSpec((1, D_MODEL), lambda b, t: (0, 0)),
            pl.BlockSpec((D_MODEL, N_IN), lambda b, t: (0, 0)),
        ],
        out_specs=[
            pl.BlockSpec((1, N_COLBLK, IP_TM, LANES), lambda b, t: (b, 0, t, 0)),
            pl.BlockSpec((1, N_DIL_COLBLK, 4, IP_TM // 4, LANES), lambda b, t: (b, 0, 0, t, 0)),
            pl.BlockSpec((1, N_DIL_COLBLK, 16, IP_TM // 16, LANES), lambda b, t: (b, 0, 0, t, 0)),
        ],
        out_shape=[
            jax.ShapeDtypeStruct((B, N_COLBLK, S, LANES), BF16),
            jax.ShapeDtypeStruct((B, N_DIL_COLBLK, 4, S // 4, LANES), BF16),
            jax.ShapeDtypeStruct((B, N_DIL_COLBLK, 16, S // 16, LANES), BF16),
        ],
        scratch_shapes=[pltpu.VMEM((N_DIL_COLBLK, IP_TM, LANES), F32)],
        compiler_params=_cparams(("arbitrary", "arbitrary")),
        name="in_proj",
    )(x, g, w_bf16)


_NT_DIMS = (((1,), (1,)), ((), ()))


def _diff_attn_body(q_ref, k_ref, v_ref, e_ref, lq1_ref, lk1_ref, lq2_ref, lk2_ref, sg_ref, o_ref,
                    kx_ref, vx_ref, p_ref):
    S = q_ref.shape[2]
    TQ, TK = DA_TQ, DA_TK
    lam = (jnp.exp(jnp.sum(lq1_ref[...] * lk1_ref[...], axis=-1, keepdims=True))
           - jnp.exp(jnp.sum(lq2_ref[...] * lk2_ref[...], axis=-1, keepdims=True)) + LAMBDA_INIT)
    lane_q = lax.broadcasted_iota(I32, (TQ, LANES), 1)
    lo = lane_q < HEAD_DIM
    lane2 = lax.broadcasted_iota(I32, (2 * TQ, LANES), 1)

    def fill(i, _):
        r0 = pl.multiple_of(i * 512, 512)
        e0 = jnp.where(lax.broadcasted_iota(I32, (512, LANES), 1) == 0, 1.0, 0.0).astype(BF16)
        kx_ref[pl.ds(r0, 512), 0:LANES] = k_ref[0, 0, pl.ds(r0, 512), :]
        kx_ref[pl.ds(r0, 512), LANES:2 * LANES] = e0
        vx_ref[pl.ds(r0, 512), 0:LANES] = v_ref[0, 0, pl.ds(r0, 512), :]
        vx_ref[pl.ds(r0, 512), LANES:2 * LANES] = e0
        return 0

    lax.fori_loop(0, S // 512, fill, 0)

    def q_tile(q0, exact_shift, slot=0):
        ps_ref = p_ref.at[slot]
        q = q_ref[0, 0, pl.ds(q0, TQ), :]
        zero = jnp.zeros_like(q)
        qa = jnp.concatenate([jnp.where(lo, q, zero), jnp.where(lo, zero, q)], axis=0)

        def bias_tile(k0):
            off = pl.multiple_of(jnp.clip(k0 - q0, DA_DMIN, DA_DMAX) - DA_DMIN, LANES)
            return e_ref[0, :, pl.ds(off, TK)]

        def tile_max(k0, pm, width=TK):
            s = lax.dot_general(qa, k_ref[0, 0, pl.ds(k0, width), :], _NT_DIMS, preferred_element_type=F32)
            bias = bias_tile(k0)[:, 0:width]
            out = []
            for half in range(2):
                sb = s[half * TQ:(half + 1) * TQ] + bias
                t = pm[half]
                for j in range(width // LANES):
                    t = jnp.maximum(t, sb[:, j * LANES:(j + 1) * LANES])
                out.append(t)
            return tuple(out)

        neg = jnp.full((TQ, LANES), -jnp.inf, F32)
        if exact_shift:
            pm = lax.fori_loop(0, S // TK, lambda ki, pm: tile_max(pl.multiple_of(ki * TK, TK), pm), (neg, neg))
        else:
            pm = tile_max(0, (neg, neg), width=DA_EST)
        m = jnp.max(jnp.concatenate(pm, axis=0), axis=-1, keepdims=True)
        mcol = jnp.where(lane2 == 0, -m, 0.0).astype(BF16)
        qx = jnp.concatenate([qa, mcol], axis=1)

        for ki in range(S // TK):
            k0 = ki * TK
            s = lax.dot_general(qx, kx_ref[k0:k0 + TK, :], _NT_DIMS, preferred_element_type=F32)
            bias = bias_tile(k0)
            for half in range(2):
                pr = jnp.exp2(s[half * TQ:(half + 1) * TQ] + bias).astype(BF16)
                ps_ref[half * TQ:(half + 1) * TQ, k0:k0 + TK] = pr

        acc = jnp.dot(ps_ref[...], vx_ref[...], preferred_element_type=F32)
        o1 = acc[0:TQ, 0:LANES] / acc[0:TQ, LANES:LANES + 1]
        o2 = acc[TQ:2 * TQ, 0:LANES] / acc[TQ:2 * TQ, LANES:LANES + 1]
        o = o1 - lam * o2
        ms = jnp.mean(o * o, axis=-1, keepdims=True)
        o = o * lax.rsqrt(ms + SUBLN_EPS) * sg_ref[...] * (1.0 - LAMBDA_INIT)
        o_ref[0, 0, pl.ds(q0, TQ), :] = o.astype(BF16)
        return jnp.where(jnp.isfinite(o), 0.0, 1.0)

    def fast_loop(qi, bad):
        for u in range(DA_GROUP):
            q0 = pl.multiple_of((qi * DA_GROUP + u) * TQ, TQ)
            bad = jnp.maximum(bad, q_tile(q0, exact_shift=False, slot=u))
        return bad

    bad = lax.fori_loop(0, S // (TQ * DA_GROUP), fast_loop, jnp.zeros((TQ, LANES), F32))

    @pl.when(jnp.max(bad) > 0.0)
    def _():
        def exact_loop(qi, _):
            q_tile(pl.multiple_of(qi * TQ, TQ), exact_shift=True)
            return 0
        lax.fori_loop(0, S // TQ, exact_loop, 0)


def _diff_attn(p, e_bias, lq1, lk1, lq2, lk2, subln_g):
    B, _, S, _ = p.shape
    H = N_DIFF_HEADS
    vec = lambda n: pl.BlockSpec((1, n), lambda b, h: (0, 0))
    return pl.pallas_call(
        _diff_attn_body,
        grid=(B, H),
        in_specs=[
            pl.BlockSpec((1, 1, S, LANES), lambda b, h: (b, h, 0, 0)),
            pl.BlockSpec((1, 1, S, LANES), lambda b, h: (b, H + h, 0, 0)),
            pl.BlockSpec((1, 1, S, LANES), lambda b, h: (b, 2 * H + h, 0, 0)),
            pl.BlockSpec((1, DA_TQ, DA_WTOT), lambda b, h: (h, 0, 0)),
            vec(HEAD_DIM), vec(HEAD_DIM), vec(HEAD_DIM), vec(HEAD_DIM), vec(2 * HEAD_DIM),
        ],
        out_specs=pl.BlockSpec((1, 1, S, LANES), lambda b, h: (b, h, 0, 0)),
        out_shape=jax.ShapeDtypeStruct((B, H, S, LANES), BF16),
        scratch_shapes=[
            pltpu.VMEM((S, 2 * LANES), BF16),
            pltpu.VMEM((S, 2 * LANES), BF16),
            pltpu.VMEM((DA_GROUP, 2 * DA_TQ, S), BF16),
        ],
        compiler_params=_cparams(("arbitrary", "arbitrary")),
        name="diff_attn",
    )(p, p, p, e_bias, lq1, lk1, lq2, lk2, subln_g)


def _dil_attn_body(q1_ref, k1_ref, v1_ref, q4_ref, k4_ref, v4_ref, q16_ref, k16_ref, v16_ref, m_ref,
                   o_ref, kpad_ref, vlo_ref, vhi_ref, oacc_ref, lse_ref):
    S = q1_ref.shape[2]
    lane = lax.broadcasted_iota(I32, (DL_BQ, LANES), 1)
    lo = lane < HEAD_DIM
    zpad = jnp.zeros((DL_HALF, LANES), BF16)

    def ones_fill(i, _):
        r0 = pl.multiple_of(i * 512, 512)
        lo_f = lax.broadcasted_iota(I32, (512, LANES), 1) < HEAD_DIM
        vlo_ref[pl.ds(r0, 512), LANES:2 * LANES] = jnp.where(lo_f, 1.0, 0.0).astype(BF16)
        vhi_ref[pl.ds(r0, 512), LANES:2 * LANES] = jnp.where(lo_f, 0.0, 1.0).astype(BF16)
        return 0

    lax.fori_loop(0, vlo_ref.shape[0] // 512, ones_fill, 0)

    def run_pattern(p, dil, get_q, get_k, get_v):
        L = S // dil
        nblk = L // DL_BQ
        LP = L + 2 * DL_HALF

        def fill(r, _):
            base = pl.multiple_of(r * LP, DL_HALF)
            for off in (0, DL_HALF + L):
                kpad_ref[pl.ds(base + off, DL_HALF), :] = zpad
                vlo_ref[pl.ds(base + off, DL_HALF), 0:LANES] = zpad
                vhi_ref[pl.ds(base + off, DL_HALF), 0:LANES] = zpad
            kpad_ref[pl.ds(base + DL_HALF, L), :] = get_k(r)[...]
            v = get_v(r)[...]
            lo_v = lax.broadcasted_iota(I32, v.shape, 1) < HEAD_DIM
            zero = jnp.zeros_like(v)
            vlo_ref[pl.ds(base + DL_HALF, L), 0:LANES] = jnp.where(lo_v, v, zero)
            vhi_ref[pl.ds(base + DL_HALF, L), 0:LANES] = jnp.where(lo_v, zero, v)
            return 0

        lax.fori_loop(0, dil, fill, 0)

        def group(gi, _):
            for u in range(DL_GROUP):
                n = gi * DL_GROUP + u
                r = n // nblk
                blk = n % nblk
                q0 = pl.multiple_of(blk * DL_BQ, DL_BQ)
                w0 = pl.multiple_of(r * LP + q0, DL_HALF)
                q = get_q(r)[pl.ds(q0, DL_BQ), :]
                kw = kpad_ref[pl.ds(w0, DL_WIN), :]
                vw = jnp.concatenate([vlo_ref[pl.ds(w0, DL_WIN), :], vhi_ref[pl.ds(w0, DL_WIN), :]], axis=0)
                edge = jnp.where(blk == 0, 0, jnp.where(blk == nblk - 1, 2, 1))
                zero = jnp.zeros_like(q)
                probs, maxes = [], []
                for hh in range(2):
                    qm = jnp.where(lo, q, zero) if hh == 0 else jnp.where(lo, zero, q)
                    s = lax.dot_general(qm, kw, _NT_DIMS, preferred_element_type=F32) + m_ref[p, hh, edge]
                    m = jnp.max(s, axis=-1, keepdims=True)
                    probs.append(jnp.exp2(s - m).astype(BF16))
                    maxes.append(m)
                both = jnp.dot(jnp.concatenate(probs, axis=1), vw, preferred_element_type=F32)
                l = both[:, LANES:2 * LANES]
                o = both[:, 0:LANES] / l
                lse = jnp.where(lo, maxes[0], maxes[1]) + jnp.log2(l)
                if dil == 1:
                    oacc_ref[p, pl.ds(q0, DL_BQ), :] = o
                    lse_ref[p, pl.ds(q0, DL_BQ), :] = lse
                else:
                    oacc_ref[p, pl.ds(q0 * dil + r, DL_BQ, stride=dil), :] = o
                    lse_ref[p, pl.ds(q0 * dil + r, DL_BQ, stride=dil), :] = lse
            return 0

        lax.fori_loop(0, dil * nblk // DL_GROUP, group, 0)

    run_pattern(0, 1, lambda r: q1_ref.at[0, 0], lambda r: k1_ref.at[0, 0], lambda r: v1_ref.at[0, 0])
    run_pattern(1, 4, lambda r: q4_ref.at[0, 0, r], lambda r: k4_ref.at[0, 0, r], lambda r: v4_ref.at[0, 0, r])
    run_pattern(2, 16, lambda r: q16_ref.at[0, 0, r], lambda r: k16_ref.at[0, 0, r], lambda r: v16_ref.at[0, 0, r])

    MIX = 256

    def mix(i, _):
        r0 = pl.multiple_of(i * MIX, MIX)
        ls = [lse_ref[p, pl.ds(r0, MIX), :] for p in range(3)]
        mx = jnp.maximum(jnp.maximum(ls[0], ls[1]), ls[2])
        ws = [jnp.exp2(l - mx) for l in ls]
        den = ws[0] + ws[1] + ws[2]
        num = ws[0] * oacc_ref[0, pl.ds(r0, MIX), :]
        num += ws[1] * oacc_ref[1, pl.ds(r0, MIX), :]
        num += ws[2] * oacc_ref[2, pl.ds(r0, MIX), :]
        o_ref[0, 0, pl.ds(r0, MIX), :] = (num / den).astype(BF16)
        return 0

    lax.fori_loop(0, S // MIX, mix, 0)


def _dil_attn(p, p4, p16, m_bias):
    B, _, S, _ = p.shape
    NP = N_DIL_PAIRS
    base = 3 * N_DIFF_HEADS
    nat = lambda o: pl.BlockSpec((1, 1, S, LANES), lambda b, h: (b, base + o * NP + h, 0, 0))
    res = lambda o, d: pl.BlockSpec((1, 1, d, S // d, LANES), lambda b, h: (b, o * NP + h, 0, 0, 0))
    return pl.pallas_call(
        _dil_attn_body,
        grid=(B, NP),
        in_specs=[nat(0), nat(1), nat(2), res(0, 4), res(1, 4), res(2, 4), res(0, 16), res(1, 16), res(2, 16),
                  pl.BlockSpec((3, 2, 3, DL_BQ, DL_WIN), lambda b, h: (0, h, 0, 0, 0))],
        out_specs=pl.BlockSpec((1, 1, S, LANES), lambda b, h: (b, h, 0, 0)),
        out_shape=jax.ShapeDtypeStruct((B, NP, S, LANES), BF16),
        scratch_shapes=[
            pltpu.VMEM((S + 2 * DL_HALF * DIL_PATTERNS[-1][1], LANES), BF16),
            pltpu.VMEM((S + 2 * DL_HALF * DIL_PATTERNS[-1][1], 2 * LANES), BF16),
            pltpu.VMEM((S + 2 * DL_HALF * DIL_PATTERNS[-1][1], 2 * LANES), BF16),
            pltpu.VMEM((3, S, LANES), F32),
            pltpu.VMEM((3, S, LANES), F32),
        ],
        compiler_params=_cparams(("arbitrary", "arbitrary")),
        name="dil_attn",
    )(p, p, p, p4, p4, p4, p16, p16, p16, m_bias)


OP_TM = 512


def _store_token_major(ref, val):
    rows = val.shape[0]
    for j in range(ROW_CHUNKS):
        ref[0, pl.ds(j, rows, stride=ROW_CHUNKS), :] = val[:, j * LANES:(j + 1) * LANES]


def _out_proj_body(oa_ref, od_ref, x_ref, w_ref, g_ref, wr_ref, x2_ref, h2_ref, aff_ref):
    pieces = [oa_ref[0, i] for i in range(N_DIFF_HEADS)] + [od_ref[0, i] for i in range(N_DIL_PAIRS)]
    a = jnp.concatenate(pieces, axis=-1)
    x2 = x_ref[0] + jnp.dot(a, w_ref[...], preferred_element_type=F32)
    ms = jnp.mean(x2 * x2, axis=-1, keepdims=True)
    h = x2 * lax.rsqrt(ms + NORM_EPS) * g_ref[...]
    h_hi = h.astype(BF16)
    h_lo = (h - h_hi.astype(F32)).astype(BF16)
    wr = wr_ref[...]
    w_hi = wr.astype(BF16)
    w_lo = (wr - w_hi.astype(F32)).astype(BF16)
    logits = (lax.dot_general(w_hi, h_hi, _NT_DIMS, preferred_element_type=F32)
              + lax.dot_general(w_hi, h_lo, _NT_DIMS, preferred_element_type=F32)
              + lax.dot_general(w_lo, h_hi, _NT_DIMS, preferred_element_type=F32))
    mx = jnp.max(logits, axis=0, keepdims=True)
    ex = jnp.exp(logits - mx)
    aff_ref[0] = ex / jnp.sum(ex, axis=0, keepdims=True)
    x2_ref[0] = x2
    _store_token_major(h2_ref, h)


def _out_proj(oa, od, x, w_bf16, g, wr_t):
    B, S, _ = x.shape
    nt = S // OP_TM
    attn = lambda n: pl.BlockSpec((1, n, OP_TM, LANES), lambda b, t: (b, 0, t, 0))
    tok_major = pl.BlockSpec((1, OP_TM * ROW_CHUNKS, LANES), lambda b, t: (b, t, 0))
    return pl.pallas_call(
        _out_proj_body,
        grid=(B, nt),
        in_specs=[
            attn(N_DIFF_HEADS), attn(N_DIL_PAIRS),
            pl.BlockSpec((1, OP_TM, D_MODEL), lambda b, t: (b, t, 0)),
            pl.BlockSpec((D_MODEL, D_MODEL), lambda b, t: (0, 0)),
            pl.BlockSpec((1, D_MODEL), lambda b, t: (0, 0)),
            pl.BlockSpec((N_EXPERTS, D_MODEL), lambda b, t: (0, 0)),
        ],
        out_specs=[pl.BlockSpec((1, OP_TM, D_MODEL), lambda b, t: (b, t, 0)), tok_major,
                   pl.BlockSpec((1, N_EXPERTS, OP_TM), lambda b, t: (b, 0, t))],
        out_shape=[
            jax.ShapeDtypeStruct((B, S, D_MODEL), F32),
            jax.ShapeDtypeStruct((B, S * ROW_CHUNKS, LANES), F32),
            jax.ShapeDtypeStruct((B, N_EXPERTS, S), F32),
        ],
        compiler_params=_cparams(("arbitrary", "arbitrary")),
        name="out_proj",
    )(oa, od, x, w_bf16, g, wr_t)


RT_GROUP = 2


def _route_body(aff_ref, idx_ref, xh_ref, xl_ref, st_ref, en_ref):
    E, S = aff_ref.shape[1], aff_ref.shape[2]
    C = idx_ref.shape[2]
    n_chunks = S // LANES
    a = aff_ref[0]
    bits = pltpu.bitcast(a, I32)
    tau = jnp.zeros((E, 1), I32)
    for bit in range(30, -1, -1):
        cand = tau | (1 << bit)
        cnt = jnp.sum(jnp.where(bits >= cand, 1.0, 0.0).astype(F32), axis=1, keepdims=True)
        tau = jnp.where(cnt >= C, cand, tau)
    gt = bits > tau
    eq = bits == tau
    need = C - jnp.sum(jnp.where(gt, 1.0, 0.0).astype(F32), axis=1, keepdims=True)

    tri = (lax.broadcasted_iota(I32, (LANES, LANES), 0) <= lax.broadcasted_iota(I32, (LANES, LANES), 1)).astype(BF16)

    def chunk_cumsum(mj, carry):
        return jnp.dot(mj.astype(BF16), tri, preferred_element_type=F32) + carry

    sel_chunks = []
    carry = jnp.zeros((E, 1), F32)
    for j in range(n_chunks):
        cs = slice(j * LANES, (j + 1) * LANES)
        eq_j = jnp.where(eq[:, cs], 1.0, 0.0).astype(F32)
        inc = chunk_cumsum(eq_j, carry)
        carry = inc[:, LANES - 1:LANES]
        take_tie = jnp.where(inc - eq_j < need, eq_j, 0.0)
        sel_chunks.append(jnp.where(gt[:, cs], 1.0, take_tie))

    lane_e = lax.broadcasted_iota(I32, (E, LANES), 1)
    st = jnp.full((E, LANES), float(C), F32)
    en = jnp.full((E, LANES), float(C), F32)
    carry = jnp.zeros((E, 1), F32)
    for j in range(n_chunks):
        sel_j = sel_chunks[j]
        inc = chunk_cumsum(sel_j, carry)
        posm = jnp.where(sel_j > 0.0, inc - sel_j, -1.0)
        hi = jnp.floor(posm * 0.5)
        xh_ref[pl.ds(j * E, E), :] = hi
        xl_ref[pl.ds(j * E, E), :] = posm - 2.0 * hi
        st = jnp.where(lane_e == j, carry, st)
        carry = inc[:, LANES - 1:LANES]
        en = jnp.where(lane_e == j, carry, en)
    pad_rows = (LANES - n_chunks) * E
    xh_ref[pl.ds(n_chunks * E, pad_rows), :] = jnp.zeros((pad_rows, LANES), F32)
    xl_ref[pl.ds(n_chunks * E, pad_rows), :] = jnp.zeros((pad_rows, LANES), F32)
    st_ref[...] = st
    en_ref[...] = en

    c_iota = lax.broadcasted_iota(I32, (C, LANES), 0).astype(F32)
    lane_row = jnp.where(lax.broadcasted_iota(I32, (SUBLANES, LANES), 0) == 0,
                         lax.broadcasted_iota(I32, (SUBLANES, LANES), 1), 0).astype(F32).astype(BF16)

    def expert_group(g, _):
        for u in range(RT_GROUP):
            e = g * RT_GROUP + u
            st_e = st_ref[pl.ds(e, 1), :]
            en_e = en_ref[pl.ds(e, 1), :]
            in_chunk = jnp.where(st_e <= c_iota, jnp.where(c_iota < en_e, 1.0, 0.0), 0.0).astype(BF16)
            xh = xh_ref[pl.ds(e, LANES, stride=E), :].astype(BF16)
            xl = xl_ref[pl.ds(e, LANES, stride=E), :].astype(BF16)
            hl = jnp.dot(in_chunk, jnp.concatenate([xh, xl], axis=1), preferred_element_type=F32)
            slot_of = 2.0 * hl[:, 0:LANES] + hl[:, LANES:2 * LANES]
            hit = jnp.where(slot_of == c_iota, 1.0, 0.0).astype(BF16)
            chunk_id = lax.dot_general(lane_row, in_chunk, _NT_DIMS, preferred_element_type=F32)
            lane_id = lax.dot_general(lane_row, hit, _NT_DIMS, preferred_element_type=F32)
            idx_ref[0, pl.ds(e, 1), :] = (chunk_id[0:1, :] * float(LANES) + lane_id[0:1, :] + 0.5).astype(I32)
        return 0

    lax.fori_loop(0, E // RT_GROUP, expert_group, 0)


def _route(aff):
    B, E, S = aff.shape
    C = EC_K * S // E
    return pl.pallas_call(
        _route_body,
        grid=(B,),
        in_specs=[pl.BlockSpec((1, E, S), lambda b: (b, 0, 0))],
        out_specs=pl.BlockSpec((1, E, C), lambda b: (b, 0, 0)),
        out_shape=jax.ShapeDtypeStruct((B, E, C), I32),
        scratch_shapes=[pltpu.VMEM((LANES * E, LANES), F32), pltpu.VMEM((LANES * E, LANES), F32),
                        pltpu.VMEM((E, LANES), F32), pltpu.VMEM((E, LANES), F32)],
        compiler_params=_cparams(("arbitrary",)),
        name="route",
    )(aff)


GA_UNROLL = 16


def _gather_body(idx_ref, h_ref, xe_ref, g_ref):
    C = idx_ref.shape[2]

    def rows(i, _):
        for u in range(GA_UNROLL):
            r = i * GA_UNROLL + u
            t = idx_ref[0, 0, r]
            src = pl.multiple_of(t * ROW_CHUNKS, ROW_CHUNKS)
            dst = pl.multiple_of(r * ROW_CHUNKS, ROW_CHUNKS)
            g_ref[pl.ds(dst, ROW_CHUNKS), :] = h_ref[0, pl.ds(src, ROW_CHUNKS), :]
        return 0

    lax.fori_loop(0, C // GA_UNROLL, rows, 0)
    for j in range(ROW_CHUNKS):
        xe_ref[0, :, j * LANES:(j + 1) * LANES] = g_ref[pl.ds(j, C, stride=ROW_CHUNKS), :].astype(BF16)


def _gather(idx, h2):
    B, E, C = idx.shape
    SR = h2.shape[1]
    return pl.pallas_call(
        _gather_body,
        grid=(B, E),
        in_specs=[
            pl.BlockSpec((1, 1, C), lambda b, e: (b * E + e, 0, 0), memory_space=pltpu.SMEM),
            pl.BlockSpec((1, SR, LANES), lambda b, e: (b, 0, 0), pipeline_mode=pl.Buffered(1)),
        ],
        out_specs=pl.BlockSpec((1, C, D_MODEL), lambda b, e: (e, b, 0)),
        out_shape=jax.ShapeDtypeStruct((E, B * C, D_MODEL), BF16),
        scratch_shapes=[pltpu.VMEM((C * ROW_CHUNKS, LANES), F32)],
        compiler_params=_cparams(("arbitrary", "arbitrary")),
        name="gather",
    )(idx.reshape(B * E, 1, C), h2)


FF_TM = 1024
FF_TF = 1408
FF_CAST_ROWS = 128


def _cast_weight(src_ref, dst_ref):
    rows = dst_ref.shape[0]

    def chunk(i, _):
        r0 = pl.multiple_of(i * FF_CAST_ROWS, FF_CAST_ROWS)
        dst_ref[pl.ds(r0, FF_CAST_ROWS), :] = src_ref[0, pl.ds(r0, FF_CAST_ROWS), :].astype(BF16)
        return 0

    lax.fori_loop(0, rows // FF_CAST_ROWS, chunk, 0)


def _ffn_up_body(x_ref, wg_ref, wu_ref, he_ref, wgb_ref, wub_ref):
    @pl.when(pl.program_id(2) == 0)
    def _():
        _cast_weight(wg_ref, wgb_ref)
        _cast_weight(wu_ref, wub_ref)

    x = x_ref[0]
    hg = jnp.dot(x, wgb_ref[...], preferred_element_type=F32)
    hu = jnp.dot(x, wub_ref[...], preferred_element_type=F32)
    he_ref[0] = (hg * jax.nn.sigmoid(hg) * hu).astype(BF16)


def _ffn_down_body(he_ref, wd_ref, y_ref, wdb_ref):
    @pl.when(pl.program_id(1) == 0)
    def _():
        _cast_weight(wd_ref, wdb_ref)

    _store_token_major(y_ref, jnp.dot(he_ref[0], wdb_ref[...], preferred_element_type=F32))


def _ffn(xe, wg, wu, wd):
    E, M, _ = xe.shape
    assert M % FF_TM == 0, (M, FF_TM)
    nm = M // FF_TM
    nf = D_FF // FF_TF
    he = pl.pallas_call(
        _ffn_up_body,
        grid=(E, nf, nm),
        in_specs=[
            pl.BlockSpec((1, FF_TM, D_MODEL), lambda e, f, m: (e, m, 0)),
            pl.BlockSpec((1, D_MODEL, FF_TF), lambda e, f, m: (e, 0, f)),
            pl.BlockSpec((1, D_MODEL, FF_TF), lambda e, f, m: (e, 0, f)),
        ],
        out_specs=pl.BlockSpec((1, FF_TM, FF_TF), lambda e, f, m: (e, m, f)),
        out_shape=jax.ShapeDtypeStruct((E, M, D_FF), BF16),
        scratch_shapes=[pltpu.VMEM((D_MODEL, FF_TF), BF16), pltpu.VMEM((D_MODEL, FF_TF), BF16)],
        compiler_params=_cparams(("arbitrary", "arbitrary", "arbitrary")),
        name="expert_up",
    )(xe, wg, wu)
    return pl.pallas_call(
        _ffn_down_body,
        grid=(E, nm),
        in_specs=[
            pl.BlockSpec((1, FF_TM, D_FF), lambda e, m: (e, m, 0)),
            pl.BlockSpec((1, D_FF, D_MODEL), lambda e, m: (e, 0, 0)),
        ],
        out_specs=pl.BlockSpec((1, FF_TM * ROW_CHUNKS, LANES), lambda e, m: (e, m, 0)),
        out_shape=jax.ShapeDtypeStruct((E, M * ROW_CHUNKS, LANES), F32),
        scratch_shapes=[pltpu.VMEM((D_FF, D_MODEL), BF16)],
        compiler_params=_cparams(("arbitrary", "arbitrary")),
        name="expert_down",
    )(he, wd)


CB_UNROLL = 8
CB_TM = 512


def _combine_body(idx_ref, aff_ref, y_ref, x2_ref, g_ref, o_ref, acc_ref):
    E = N_EXPERTS
    C = idx_ref.shape[2]
    s = pl.program_id(1)

    @pl.when(s == 0)
    def _():
        def zero(i, _):
            r0 = pl.multiple_of(i * 1024, 1024)
            acc_ref[pl.ds(r0, 1024), :] = jnp.zeros((1024, LANES), F32)
            return 0
        lax.fori_loop(0, acc_ref.shape[0] // 1024, zero, 0)

    @pl.when(s < E)
    def _():
        def rows(i, _):
            dsts, sums = [], []
            for u in range(CB_UNROLL):
                r = i * CB_UNROLL + u
                t = idx_ref[0, 0, r]
                gate = aff_ref[0, 0, t]
                dst = pl.multiple_of(t * ROW_CHUNKS, ROW_CHUNKS)
                src = pl.multiple_of(r * ROW_CHUNKS, ROW_CHUNKS)
                dsts.append(dst)
                sums.append(acc_ref[pl.ds(dst, ROW_CHUNKS), :] + gate * y_ref[0, pl.ds(src, ROW_CHUNKS), :])
            for dst, v in zip(dsts, sums):
                acc_ref[pl.ds(dst, ROW_CHUNKS), :] = v
            return 0
        lax.fori_loop(0, C // CB_UNROLL, rows, 0)

    @pl.when(s >= E)
    def _():
        r0 = pl.multiple_of((s - E) * (CB_TM * ROW_CHUNKS), CB_TM * ROW_CHUNKS)
        moe = jnp.concatenate([acc_ref[pl.ds(r0 + j, CB_TM, stride=ROW_CHUNKS), :] for j in range(ROW_CHUNKS)],
                              axis=1)
        v = x2_ref[0] + moe
        ms = jnp.mean(v * v, axis=-1, keepdims=True)
        o_ref[0] = v * lax.rsqrt(ms + NORM_EPS) * g_ref[...]


def _combine(idx, aff, ye, x2, g_f):
    B, E, C = idx.shape
    S = x2.shape[1]
    SR = S * ROW_CHUNKS
    nfin = S // CB_TM
    return pl.pallas_call(
        _combine_body,
        grid=(B, E + nfin),
        in_specs=[
            pl.BlockSpec((1, 1, C), lambda b, s: (b * E + jnp.minimum(s, E - 1), 0, 0), memory_space=pltpu.SMEM),
            pl.BlockSpec((1, 1, S), lambda b, s: (b * E + jnp.minimum(s, E - 1), 0, 0), memory_space=pltpu.SMEM),
            pl.BlockSpec((1, C * ROW_CHUNKS, LANES), lambda b, s: (jnp.minimum(s, E - 1), b, 0)),
            pl.BlockSpec((1, CB_TM, D_MODEL), lambda b, s: (b, jnp.maximum(s - E, 0), 0)),
            pl.BlockSpec((1, D_MODEL), lambda b, s: (0, 0)),
        ],
        out_specs=pl.BlockSpec((1, CB_TM, D_MODEL), lambda b, s: (b, jnp.maximum(s - E, 0), 0)),
        out_shape=jax.ShapeDtypeStruct((B, S, D_MODEL), F32),
        scratch_shapes=[pltpu.VMEM((SR, LANES), F32)],
        compiler_params=_cparams(("arbitrary", "arbitrary")),
        name="combine",
    )(idx.reshape(B * E, 1, C), aff.reshape(B * E, 1, S), ye, x2, g_f)


def kernel(x, norm1_g, w_in, lam_q1, lam_k1, lam_q2, lam_k2, subln_g, w_out, rel_bias,
           norm2_g, w_router, w_gate, w_up, w_down, norm_f_g):
    B, S, _ = x.shape
    C = EC_K * S // N_EXPERTS
    e_bias = _diff_bias(rel_bias)
    m_bias = _dil_bias(rel_bias)
    p, p4, p16 = _in_proj(x, norm1_g, w_in[0].astype(BF16))
    oa = _diff_attn(p, e_bias, lam_q1, lam_k1, lam_q2, lam_k2, subln_g)
    od = _dil_attn(p, p4, p16, m_bias)
    x2, h2, aff = _out_proj(oa, od, x, w_out[0].astype(BF16), norm2_g, w_router[0].T)
    idx = _route(aff)
    xe = _gather(idx, h2)
    ye = _ffn(xe, w_gate[0], w_up[0], w_down[0])
    return _combine(idx, aff, ye, x2, norm_f_g.reshape(1, D_MODEL))
```

```python
import functools
import math

import jax
import jax.numpy as jnp
from jax import lax
from jax.experimental import pallas as pl
from jax.experimental.pallas import tpu as pltpu

F32 = jnp.float32
BF16 = jnp.bfloat16
I32 = jnp.int32

D_MODEL = 1024
HEAD_DIM = 64
N_DIFF_HEADS = 4
N_DIL_HEADS = 8
N_DIL_PAIRS = N_DIL_HEADS // 2
DIFF_WIDTH = 512
DIL_WIDTH = 512
N_IN = 3 * DIFF_WIDTH + 3 * DIL_WIDTH
DIL_PATTERNS = ((128, 1), (512, 4), (2048, 16))
N_BUCKETS = 32
REL_MAX_DIST = 1024
N_EXPERTS = 16
EC_K = 2
D_FF = 2816
NORM_EPS = 1e-6
SUBLN_EPS = 1e-5
NEG = -1e30
LAMBDA_INIT = 0.8 - 0.6 * math.exp(0.0)

LANES = 128
SUBLANES = 8
ROW_CHUNKS = D_MODEL // LANES

DA_TQ = 256
DA_TK = 512
DA_GROUP = 4
DA_EST = 128
DA_DMIN = -1280
DA_DMAX = 1024
DA_WTOT = DA_DMAX - DA_DMIN + DA_TK

DL_BQ = 128
DL_HALF = 64
DL_WIN = DL_BQ + 2 * DL_HALF
DL_GROUP = 16

LOG2E = math.log2(math.e)

VMEM_LIMIT = 56 * 1024 * 1024


def _cparams(sem):
    return pltpu.CompilerParams(dimension_semantics=sem, vmem_limit_bytes=VMEM_LIMIT)


def _t5_bucket(rel):
    half = N_BUCKETS // 2
    max_exact = half // 2
    n = jnp.abs(rel)
    nf = jnp.maximum(n, 1).astype(F32)
    large = max_exact + (jnp.log(nf / max_exact) / math.log(REL_MAX_DIST / max_exact)
                         * (half - max_exact)).astype(I32)
    large = jnp.minimum(large, half - 1)
    return jnp.where(rel > 0, half, 0) + jnp.where(n < max_exact, n, large)


def _bias_lookup(rel, tab_ref, col):
    bucket = _t5_bucket(rel)
    acc = jnp.zeros(rel.shape, F32)
    for j in range(N_BUCKETS):
        acc = jnp.where(bucket == j, tab_ref[j, col], acc)
    return acc


def _diff_bias_body(tab_ref, e_ref):
    h = pl.program_id(0)

    def chunk(cc, _):
        c0 = pl.multiple_of(cc * LANES, LANES)
        i = lax.broadcasted_iota(I32, (DA_TQ, LANES), 0)
        c = lax.broadcasted_iota(I32, (DA_TQ, LANES), 1) + c0
        e_ref[0, :, pl.ds(c0, LANES)] = _bias_lookup(c - i + DA_DMIN, tab_ref, h) * LOG2E
        return 0

    lax.fori_loop(0, DA_WTOT // LANES, chunk, 0)


def _diff_bias(rel_bias):
    return pl.pallas_call(
        _diff_bias_body,
        grid=(N_DIFF_HEADS,),
        in_specs=[pl.BlockSpec(memory_space=pltpu.SMEM)],
        out_specs=pl.BlockSpec((1, DA_TQ, DA_WTOT), lambda h: (h, 0, 0)),
        out_shape=jax.ShapeDtypeStruct((N_DIFF_HEADS, DA_TQ, DA_WTOT), F32),
        compiler_params=_cparams(("arbitrary",)),
        name="diff_bias",
    )(rel_bias)


def _dil_bias_body(tab_ref, m_ref):
    p = pl.program_id(0)
    h = pl.program_id(1)
    dil = jnp.where(p == 0, DIL_PATTERNS[0][1], jnp.where(p == 1, DIL_PATTERNS[1][1], DIL_PATTERNS[2][1]))
    i = lax.broadcasted_iota(I32, (DL_BQ, DL_WIN), 0)
    c = lax.broadcasted_iota(I32, (DL_BQ, DL_WIN), 1)
    off = c - DL_HALF - i
    bias = _bias_lookup(off * dil, tab_ref, N_DIFF_HEADS + h) * LOG2E
    band = jnp.abs(off) <= DL_HALF
    m_ref[0, 0, 0] = jnp.where(band & (c >= DL_HALF), bias, NEG)
    m_ref[0, 0, 1] = jnp.where(band, bias, NEG)
    m_ref[0, 0, 2] = jnp.where(band & (c < DL_HALF + DL_BQ), bias, NEG)


def _dil_bias(rel_bias):
    n_pat = len(DIL_PATTERNS)
    return pl.pallas_call(
        _dil_bias_body,
        grid=(n_pat, N_DIL_HEADS),
        in_specs=[pl.BlockSpec(memory_space=pltpu.SMEM)],
        out_specs=pl.BlockSpec((1, 1, 3, DL_BQ, DL_WIN), lambda p, h: (p, h, 0, 0, 0)),
        out_shape=jax.ShapeDtypeStruct((n_pat, N_DIL_HEADS, 3, DL_BQ, DL_WIN), F32),
        compiler_params=_cparams(("arbitrary", "arbitrary")),
        name="dil_bias",
    )(rel_bias)


IP_TM = 512
IP_TN = 256
N_COLBLK = N_IN // LANES
N_DIL_COLBLK = 3 * DIL_WIDTH // LANES


def _in_proj_body(x_ref, g_ref, w_ref, p_ref, p4_ref, p16_ref, scr_ref):
    q_scale = HEAD_DIM ** -0.5 * LOG2E
    x = x_ref[0]
    ms = jnp.mean(x * x, axis=-1, keepdims=True)
    y = (x * lax.rsqrt(ms + NORM_EPS) * g_ref[...]).astype(BF16)
    for n in range(N_IN // IP_TN):
        c0 = n * IP_TN
        r = jnp.dot(y, w_ref[:, c0:c0 + IP_TN], preferred_element_type=F32)
        is_q = (c0 < DIFF_WIDTH) or (3 * DIFF_WIDTH <= c0 < 3 * DIFF_WIDTH + DIL_WIDTH)
        if is_q:
            r = r * q_scale
        for k in range(IP_TN // LANES):
            p_ref[0, 2 * n + k] = r[:, k * LANES:(k + 1) * LANES].astype(BF16)
        if c0 >= 3 * DIFF_WIDTH:
            for k in range(IP_TN // LANES):
                scr_ref[(c0 - 3 * DIFF_WIDTH) // LANES + k] = r[:, k * LANES:(k + 1) * LANES]
    for c in range(N_DIL_COLBLK):
        for d, dst_ref in ((4, p4_ref), (16, p16_ref)):
            for r_ in range(d):
                dst_ref[0, c, r_] = scr_ref[c, pl.ds(r_, IP_TM // d, stride=d), :].astype(BF16)


def _in_proj(x, g, w_bf16):
    B, S, _ = x.shape
    nt = S // IP_TM
    return pl.pallas_call(
        _in_proj_body,
        grid=(B, nt),
        in_specs=[
            pl.BlockSpec((1, IP_TM, D_MODEL), lambda b, t: (b, t, 0)),
            pl.BlockSpec((1, D_MODEL), lambda b, t: (0, 0)),
            pl.BlockSpec((D_MODEL, N_IN), lambda b, t: (0, 0)),
        ],
        out_specs=[
            pl.BlockSpec((1, N_COLBLK, IP_TM, LANES), lambda b, t: (b, 0, t, 0)),
            pl.BlockSpec((1, N_DIL_COLBLK, 4, IP_TM // 4, LANES), lambda b, t: (b, 0, 0, t, 0)),
            pl.BlockSpec((1, N_DIL_COLBLK, 16, IP_TM // 16, LANES), lambda b, t: (b, 0, 0, t, 0)),
        ],
        out_shape=[
            jax.ShapeDtypeStruct((B, N_COLBLK, S, LANES), BF16),
            jax.ShapeDtypeStruct((B, N_DIL_COLBLK, 4, S // 4, LANES), BF16),
            jax.ShapeDtypeStruct((B, N_DIL_COLBLK, 16, S // 16, LANES), BF16),
        ],
        scratch_shapes=[pltpu.VMEM((N_DIL_COLBLK, IP_TM, LANES), F32)],
        compiler_params=_cparams(("arbitrary", "arbitrary")),
        name="in_proj",
    )(x, g, w_bf16)


_NT_DIMS = (((1,), (1,)), ((), ()))


def _diff_attn_body(q_ref, k_ref, v_ref, e_ref, lq1_ref, lk1_ref, lq2_ref, lk2_ref, sg_ref, o_ref,
                    kx_ref, vx_ref, p_ref):
    S = q_ref.shape[2]
    TQ, TK = DA_TQ, DA_TK
    lam = (jnp.exp(jnp.sum(lq1_ref[...] * lk1_ref[...], axis=-1, keepdims=True))
           - jnp.exp(jnp.sum(lq2_ref[...] * lk2_ref[...], axis=-1, keepdims=True)) + LAMBDA_INIT)
    lane_q = lax.broadcasted_iota(I32, (TQ, LANES), 1)
    lo = lane_q < HEAD_DIM
    lane2 = lax.broadcasted_iota(I32, (2 * TQ, LANES), 1)

    def fill(i, _):
        r0 = pl.multiple_of(i * 512, 512)
        e0 = jnp.where(lax.broadcasted_iota(I32, (512, LANES), 1) == 0, 1.0, 0.0).astype(BF16)
        kx_ref[pl.ds(r0, 512), 0:LANES] = k_ref[0, 0, pl.ds(r0, 512), :]
        kx_ref[pl.ds(r0, 512), LANES:2 * LANES] = e0
        vx_ref[pl.ds(r0, 512), 0:LANES] = v_ref[0, 0, pl.ds(r0, 512), :]
        vx_ref[pl.ds(r0, 512), LANES:2 * LANES] = e0
        return 0

    lax.fori_loop(0, S // 512, fill, 0)

    def q_tile(q0, exact_shift, slot=0):
        ps_ref = p_ref.at[slot]
        q = q_ref[0, 0, pl.ds(q0, TQ), :]
        zero = jnp.zeros_like(q)
        qa = jnp.concatenate([jnp.where(lo, q, zero), jnp.where(lo, zero, q)], axis=0)

        def bias_tile(k0):
            off = pl.multiple_of(jnp.clip(k0 - q0, DA_DMIN, DA_DMAX) - DA_DMIN, LANES)
            return e_ref[0, :, pl.ds(off, TK)]

        def tile_max(k0, pm, width=TK):
            s = lax.dot_general(qa, k_ref[0, 0, pl.ds(k0, width), :], _NT_DIMS, preferred_element_type=F32)
            bias = bias_tile(k0)[:, 0:width]
            out = []
            for half in range(2):
                sb = s[half * TQ:(half + 1) * TQ] + bias
                t = pm[half]
                for j in range(width // LANES):
                    t = jnp.maximum(t, sb[:, j * LANES:(j + 1) * LANES])
                out.append(t)
            return tuple(out)

        neg = jnp.full((TQ, LANES), -jnp.inf, F32)
        if exact_shift:
            pm = lax.fori_loop(0, S // TK, lambda ki, pm: tile_max(pl.multiple_of(ki * TK, TK), pm), (neg, neg))
        else:
            pm = tile_max(0, (neg, neg), width=DA_EST)
        m = jnp.max(jnp.concatenate(pm, axis=0), axis=-1, keepdims=True)
        mcol = jnp.where(lane2 == 0, -m, 0.0).astype(BF16)
        qx = jnp.concatenate([qa, mcol], axis=1)

        for ki in range(S // TK):
            k0 = ki * TK
            s = lax.dot_general(qx, kx_ref[k0:k0 + TK, :], _NT_DIMS, preferred_element_type=F32)
            bias = bias_tile(k0)
            for half in range(2):
                pr = jnp.exp2(s[half * TQ:(half + 1) * TQ] + bias).astype(BF16)
                ps_ref[half * TQ:(half + 1) * TQ, k0:k0 + TK] = pr

        acc = jnp.dot(ps_ref[...], vx_ref[...], preferred_element_type=F32)
        o1 = acc[0:TQ, 0:LANES] / acc[0:TQ, LANES:LANES + 1]
        o2 = acc[TQ:2 * TQ, 0:LANES] / acc[TQ:2 * TQ, LANES:LANES + 1]
        o = o1 - lam * o2
        ms = jnp.mean(o * o, axis=-1, keepdims=True)
        o = o * lax.rsqrt(ms + SUBLN_EPS) * sg_ref[...] * (1.0 - LAMBDA_INIT)
        o_ref[0, 0, pl.ds(q0, TQ), :] = o.astype(BF16)
        return jnp.where(jnp.isfinite(o), 0.0, 1.0)

    def fast_loop(qi, bad):
        for u in range(DA_GROUP):
            q0 = pl.multiple_of((qi * DA_GROUP + u) * TQ, TQ)
            bad = jnp.maximum(bad, q_tile(q0, exact_shift=False, slot=u))
        return bad

    bad = lax.fori_loop(0, S // (TQ * DA_GROUP), fast_loop, jnp.zeros((TQ, LANES), F32))

    @pl.when(jnp.max(bad) > 0.0)
    def _():
        def exact_loop(qi, _):
            q_tile(pl.multiple_of(qi * TQ, TQ), exact_shift=True)
            return 0
        lax.fori_loop(0, S // TQ, exact_loop, 0)


def _diff_attn(p, e_bias, lq1, lk1, lq2, lk2, subln_g):
    B, _, S, _ = p.shape
    H = N_DIFF_HEADS
    vec = lambda n: pl.BlockSpec((1, n), lambda b, h: (0, 0))
    return pl.pallas_call(
        _diff_attn_body,
        grid=(B, H),
        in_specs=[
            pl.BlockSpec((1, 1, S, LANES), lambda b, h: (b, h, 0, 0)),
            pl.BlockSpec((1, 1, S, LANES), lambda b, h: (b, H + h, 0, 0)),
            pl.BlockSpec((1, 1, S, LANES), lambda b, h: (b, 2 * H + h, 0, 0)),
            pl.BlockSpec((1, DA_TQ, DA_WTOT), lambda b, h: (h, 0, 0)),
            vec(HEAD_DIM), vec(HEAD_DIM), vec(HEAD_DIM), vec(HEAD_DIM), vec(2 * HEAD_DIM),
        ],
        out_specs=pl.BlockSpec((1, 1, S, LANES), lambda b, h: (b, h, 0, 0)),
        out_shape=jax.ShapeDtypeStruct((B, H, S, LANES), BF16),
        scratch_shapes=[
            pltpu.VMEM((S, 2 * LANES), BF16),
            pltpu.VMEM((S, 2 * LANES), BF16),
            pltpu.VMEM((DA_GROUP, 2 * DA_TQ, S), BF16),
        ],
        compiler_params=_cparams(("arbitrary", "arbitrary")),
        name="diff_attn",
    )(p, p, p, e_bias, lq1, lk1, lq2, lk2, subln_g)


def _dil_attn_body(q1_ref, k1_ref, v1_ref, q4_ref, k4_ref, v4_ref, q16_ref, k16_ref, v16_ref, m_ref,
                   o_ref, kpad_ref, vlo_ref, vhi_ref, oacc_ref, lse_ref):
    S = q1_ref.shape[2]
    lane = lax.broadcasted_iota(I32, (DL_BQ, LANES), 1)
    lo = lane < HEAD_DIM
    zpad = jnp.zeros((DL_HALF, LANES), BF16)

    def ones_fill(i, _):
        r0 = pl.multiple_of(i * 512, 512)
        lo_f = lax.broadcasted_iota(I32, (512, LANES), 1) < HEAD_DIM
        vlo_ref[pl.ds(r0, 512), LANES:2 * LANES] = jnp.where(lo_f, 1.0, 0.0).astype(BF16)
        vhi_ref[pl.ds(r0, 512), LANES:2 * LANES] = jnp.where(lo_f, 0.0, 1.0).astype(BF16)
        return 0

    lax.fori_loop(0, vlo_ref.shape[0] // 512, ones_fill, 0)

    def run_pattern(p, dil, get_q, get_k, get_v):
        L = S // dil
        nblk = L // DL_BQ
        LP = L + 2 * DL_HALF

        def fill(r, _):
            base = pl.multiple_of(r * LP, DL_HALF)
            for off in (0, DL_HALF + L):
                kpad_ref[pl.ds(base + off, DL_HALF), :] = zpad
                vlo_ref[pl.ds(base + off, DL_HALF), 0:LANES] = zpad
                vhi_ref[pl.ds(base + off, DL_HALF), 0:LANES] = zpad
            kpad_ref[pl.ds(base + DL_HALF, L), :] = get_k(r)[...]
            v = get_v(r)[...]
            lo_v = lax.broadcasted_iota(I32, v.shape, 1) < HEAD_DIM
            zero = jnp.zeros_like(v)
            vlo_ref[pl.ds(base + DL_HALF, L), 0:LANES] = jnp.where(lo_v, v, zero)
            vhi_ref[pl.ds(base + DL_HALF, L), 0:LANES] = jnp.where(lo_v, zero, v)
            return 0

        lax.fori_loop(0, dil, fill, 0)

        def group(gi, _):
            for u in range(DL_GROUP):
                n = gi * DL_GROUP + u
                r = n // nblk
                blk = n % nblk
                q0 = pl.multiple_of(blk * DL_BQ, DL_BQ)
                w0 = pl.multiple_of(r * LP + q0, DL_HALF)
                q = get_q(r)[pl.ds(q0, DL_BQ), :]
                kw = kpad_ref[pl.ds(w0, DL_WIN), :]
                vw = jnp.concatenate([vlo_ref[pl.ds(w0, DL_WIN), :], vhi_ref[pl.ds(w0, DL_WIN), :]], axis=0)
                edge = jnp.where(blk == 0, 0, jnp.where(blk == nblk - 1, 2, 1))
                zero = jnp.zeros_like(q)
                probs, maxes = [], []
                for hh in range(2):
                    qm = jnp.where(lo, q, zero) if hh == 0 else jnp.where(lo, zero, q)
                    s = lax.dot_general(qm, kw, _NT_DIMS, preferred_element_type=F32) + m_ref[p, hh, edge]
                    m = jnp.max(s, axis=-1, keepdims=True)
                    probs.append(jnp.exp2(s - m).astype(BF16))
                    maxes.append(m)
                both = jnp.dot(jnp.concatenate(probs, axis=1), vw, preferred_element_type=F32)
                l = both[:, LANES:2 * LANES]
                o = both[:, 0:LANES] / l
                lse = jnp.where(lo, maxes[0], maxes[1]) + jnp.log2(l)
                if dil == 1:
                    oacc_ref[p, pl.ds(q0, DL_BQ), :] = o
                    lse_ref[p, pl.ds(q0, DL_BQ), :] = lse
                else:
                    oacc_ref[p, pl.ds(q0 * dil + r, DL_BQ, stride=dil), :] = o
                    lse_ref[p, pl.ds(q0 * dil + r, DL_BQ, stride=dil), :] = lse
            return 0

        lax.fori_loop(0, dil * nblk // DL_GROUP, group, 0)

    run_pattern(0, 1, lambda r: q1_ref.at[0, 0], lambda r: k1_ref.at[0, 0], lambda r: v1_ref.at[0, 0])
    run_pattern(1, 4, lambda r: q4_ref.at[0, 0, r], lambda r: k4_ref.at[0, 0, r], lambda r: v4_ref.at[0, 0, r])
    run_pattern(2, 16, lambda r: q16_ref.at[0, 0, r], lambda r: k16_ref.at[0, 0, r], lambda r: v16_ref.at[0, 0, r])

    MIX = 256

    def mix(i, _):
        r0 = pl.multiple_of(i * MIX, MIX)
        ls = [lse_ref[p, pl.ds(r0, MIX), :] for p in range(3)]
        mx = jnp.maximum(jnp.maximum(ls[0], ls[1]), ls[2])
        ws = [jnp.exp2(l - mx) for l in ls]
        den = ws[0] + ws[1] + ws[2]
        num = ws[0] * oacc_ref[0, pl.ds(r0, MIX), :]
        num += ws[1] * oacc_ref[1, pl.ds(r0, MIX), :]
        num += ws[2] * oacc_ref[2, pl.ds(r0, MIX), :]
        o_ref[0, 0, pl.ds(r0, MIX), :] = (num / den).astype(BF16)
        return 0

    lax.fori_loop(0, S // MIX, mix, 0)


def _dil_attn(p, p4, p16, m_bias):
    B, _, S, _ = p.shape
    NP = N_DIL_PAIRS
    base = 3 * N_DIFF_HEADS
    nat = lambda o: pl.BlockSpec((1, 1, S, LANES), lambda b, h: (b, base + o * NP + h, 0, 0))
    res = lambda o, d: pl.BlockSpec((1, 1, d, S // d, LANES), lambda b, h: (b, o * NP + h, 0, 0, 0))
    return pl.pallas_call(
        _dil_attn_body,
        grid=(B, NP),
        in_specs=[nat(0), nat(1), nat(2), res(0, 4), res(1, 4), res(2, 4), res(0, 16), res(1, 16), res(2, 16),
                  pl.BlockSpec((3, 2, 3, DL_BQ, DL_WIN), lambda b, h: (0, h, 0, 0, 0))],
        out_specs=pl.BlockSpec((1, 1, S, LANES), lambda b, h: (b, h, 0, 0)),
        out_shape=jax.ShapeDtypeStruct((B, NP, S, LANES), BF16),
        scratch_shapes=[
            pltpu.VMEM((S + 2 * DL_HALF * DIL_PATTERNS[-1][1], LANES), BF16),
            pltpu.VMEM((S + 2 * DL_HALF * DIL_PATTERNS[-1][1], 2 * LANES), BF16),
            pltpu.VMEM((S + 2 * DL_HALF * DIL_PATTERNS[-1][1], 2 * LANES), BF16),
            pltpu.VMEM((3, S, LANES), F32),
            pltpu.VMEM((3, S, LANES), F32),
        ],
        compiler_params=_cparams(("arbitrary", "arbitrary")),
        name="dil_attn",
    )(p, p, p, p4, p4, p4, p16, p16, p16, m_bias)


OP_TM = 512


def _store_token_major(ref, val):
    rows = val.shape[0]
    for j in range(ROW_CHUNKS):
        ref[0, pl.ds(j, rows, stride=ROW_CHUNKS), :] = val[:, j * LANES:(j + 1) * LANES]


def _out_proj_body(oa_ref, od_ref, x_ref, w_ref, g_ref, wr_ref, x2_ref, h2_ref, aff_ref):
    pieces = [oa_ref[0, i] for i in range(N_DIFF_HEADS)] + [od_ref[0, i] for i in range(N_DIL_PAIRS)]
    a = jnp.concatenate(pieces, axis=-1)
    x2 = x_ref[0] + jnp.dot(a, w_ref[...], preferred_element_type=F32)
    ms = jnp.mean(x2 * x2, axis=-1, keepdims=True)
    h = x2 * lax.rsqrt(ms + NORM_EPS) * g_ref[...]
    h_hi = h.astype(BF16)
    h_lo = (h - h_hi.astype(F32)).astype(BF16)
    wr = wr_ref[...]
    w_hi = wr.astype(BF16)
    w_lo = (wr - w_hi.astype(F32)).astype(BF16)
    logits = (lax.dot_general(w_hi, h_hi, _NT_DIMS, preferred_element_type=F32)
              + lax.dot_general(w_hi, h_lo, _NT_DIMS, preferred_element_type=F32)
              + lax.dot_general(w_lo, h_hi, _NT_DIMS, preferred_element_type=F32))
    mx = jnp.max(logits, axis=0, keepdims=True)
    ex = jnp.exp(logits - mx)
    aff_ref[0] = ex / jnp.sum(ex, axis=0, keepdims=True)
    x2_ref[0] = x2
    _store_token_major(h2_ref, h)


def _out_proj(oa, od, x, w_bf16, g, wr_t):
    B, S, _ = x.shape
    nt = S // OP_TM
    attn = lambda n: pl.BlockSpec((1, n, OP_TM, LANES), lambda b, t: (b, 0, t, 0))
    tok_major = pl.BlockSpec((1, OP_TM * ROW_CHUNKS, LANES), lambda b, t: (b, t, 0))
    return pl.pallas_call(
        _out_proj_body,
        grid=(B, nt),
        in_specs=[
            attn(N_DIFF_HEADS), attn(N_DIL_PAIRS),
            pl.BlockSpec((1, OP_TM, D_MODEL), lambda b, t: (b, t, 0)),
            pl.BlockSpec((D_MODEL, D_MODEL), lambda b, t: (0, 0)),
            pl.BlockSpec((1, D_MODEL), lambda b, t: (0, 0)),
            pl.BlockSpec((N_EXPERTS, D_MODEL), lambda b, t: (0, 0)),
        ],
        out_specs=[pl.BlockSpec((1, OP_TM, D_MODEL), lambda b, t: (b, t, 0)), tok_major,
                   pl.BlockSpec((1, N_EXPERTS, OP_TM), lambda b, t: (b, 0, t))],
        out_shape=[
            jax.ShapeDtypeStruct((B, S, D_MODEL), F32),
            jax.ShapeDtypeStruct((B, S * ROW_CHUNKS, LANES), F32),
            jax.ShapeDtypeStruct((B, N_EXPERTS, S), F32),
        ],
        compiler_params=_cparams(("arbitrary", "arbitrary")),
        name="out_proj",
    )(oa, od, x, w_bf16, g, wr_t)


RT_GROUP = 2


def _route_body(aff_ref, idx_ref, xh_ref, xl_ref, st_ref, en_ref):
    E, S = aff_ref.shape[1], aff_ref.shape[2]
    C = idx_ref.shape[2]
    n_chunks = S // LANES
    a = aff_ref[0]
    bits = pltpu.bitcast(a, I32)
    tau = jnp.zeros((E, 1), I32)
    for bit in range(30, -1, -1):
        cand = tau | (1 << bit)
        cnt = jnp.sum(jnp.where(bits >= cand, 1.0, 0.0).astype(F32), axis=1, keepdims=True)
        tau = jnp.where(cnt >= C, cand, tau)
    gt = bits > tau
    eq = bits == tau
    need = C - jnp.sum(jnp.where(gt, 1.0, 0.0).astype(F32), axis=1, keepdims=True)

    tri = (lax.broadcasted_iota(I32, (LANES, LANES), 0) <= lax.broadcasted_iota(I32, (LANES, LANES), 1)).astype(BF16)

    def chunk_cumsum(mj, carry):
        return jnp.dot(mj.astype(BF16), tri, preferred_element_type=F32) + carry

    sel_chunks = []
    carry = jnp.zeros((E, 1), F32)
    for j in range(n_chunks):
        cs = slice(j * LANES, (j + 1) * LANES)
        eq_j = jnp.where(eq[:, cs], 1.0, 0.0).astype(F32)
        inc = chunk_cumsum(eq_j, carry)
        carry = inc[:, LANES - 1:LANES]
        take_tie = jnp.where(inc - eq_j < need, eq_j, 0.0)
        sel_chunks.append(jnp.where(gt[:, cs], 1.0, take_tie))

    lane_e = lax.broadcasted_iota(I32, (E, LANES), 1)
    st = jnp.full((E, LANES), float(C), F32)
    en = jnp.full((E, LANES), float(C), F32)
    carry = jnp.zeros((E, 1), F32)
    for j in range(n_chunks):
        sel_j = sel_chunks[j]
        inc = chunk_cumsum(sel_j, carry)
        posm = jnp.where(sel_j > 0.0, inc - sel_j, -1.0)
        hi = jnp.floor(posm * 0.5)
        xh_ref[pl.ds(j * E, E), :] = hi
        xl_ref[pl.ds(j * E, E), :] = posm - 2.0 * hi
        st = jnp.where(lane_e == j, carry, st)
        carry = inc[:, LANES - 1:LANES]
        en = jnp.where(lane_e == j, carry, en)
    pad_rows = (LANES - n_chunks) * E
    xh_ref[pl.ds(n_chunks * E, pad_rows), :] = jnp.zeros((pad_rows, LANES), F32)
    xl_ref[pl.ds(n_chunks * E, pad_rows), :] = jnp.zeros((pad_rows, LANES), F32)
    st_ref[...] = st
    en_ref[...] = en

    c_iota = lax.broadcasted_iota(I32, (C, LANES), 0).astype(F32)
    lane_row = jnp.where(lax.broadcasted_iota(I32, (SUBLANES, LANES), 0) == 0,
                         lax.broadcasted_iota(I32, (SUBLANES, LANES), 1), 0).astype(F32).astype(BF16)

    def expert_group(g, _):
        for u in range(RT_GROUP):
            e = g * RT_GROUP + u
            st_e = st_ref[pl.ds(e, 1), :]
            en_e = en_ref[pl.ds(e, 1), :]
            in_chunk = jnp.where(st_e <= c_iota, jnp.where(c_iota < en_e, 1.0, 0.0), 0.0).astype(BF16)
            xh = xh_ref[pl.ds(e, LANES, stride=E), :].astype(BF16)
            xl = xl_ref[pl.ds(e, LANES, stride=E), :].astype(BF16)
            hl = jnp.dot(in_chunk, jnp.concatenate([xh, xl], axis=1), preferred_element_type=F32)
            slot_of = 2.0 * hl[:, 0:LANES] + hl[:, LANES:2 * LANES]
            hit = jnp.where(slot_of == c_iota, 1.0, 0.0).astype(BF16)
            chunk_id = lax.dot_general(lane_row, in_chunk, _NT_DIMS, preferred_element_type=F32)
            lane_id = lax.dot_general(lane_row, hit, _NT_DIMS, preferred_element_type=F32)
            idx_ref[0, pl.ds(e, 1), :] = (chunk_id[0:1, :] * float(LANES) + lane_id[0:1, :] + 0.5).astype(I32)
        return 0

    lax.fori_loop(0, E // RT_GROUP, expert_group, 0)


def _route(aff):
    B, E, S = aff.shape
    C = EC_K * S // E
    return pl.pallas_call(
        _route_body,
        grid=(B,),
        in_specs=[pl.BlockSpec((1, E, S), lambda b: (b, 0, 0))],
        out_specs=pl.BlockSpec((1, E, C), lambda b: (b, 0, 0)),
        out_shape=jax.ShapeDtypeStruct((B, E, C), I32),
        scratch_shapes=[pltpu.VMEM((LANES * E, LANES), F32), pltpu.VMEM((LANES * E, LANES), F32),
                        pltpu.VMEM((E, LANES), F32), pltpu.VMEM((E, LANES), F32)],
        compiler_params=_cparams(("arbitrary",)),
        name="route",
    )(aff)


GA_UNROLL = 16


def _gather_body(idx_ref, h_ref, xe_ref, g_ref):
    C = idx_ref.shape[2]
    pitch = g_ref.shape[0] // ROW_CHUNKS

    def rows(i, _):
        for u in range(GA_UNROLL):
            r = i * GA_UNROLL + u
            t = idx_ref[0, 0, r]
            src = pl.multiple_of(t * ROW_CHUNKS, ROW_CHUNKS)
            g_ref[pl.ds(r, ROW_CHUNKS, stride=pitch), :] = h_ref[0, pl.ds(src, ROW_CHUNKS), :]
        return 0

    lax.fori_loop(0, C // GA_UNROLL, rows, 0)
    for j in range(ROW_CHUNKS):
        xe_ref[0, :, j * LANES:(j + 1) * LANES] = g_ref[j * pitch:j * pitch + C, :].astype(BF16)


def _gather(idx, h2):
    B, E, C = idx.shape
    SR = h2.shape[1]
    return pl.pallas_call(
        _gather_body,
        grid=(B, E),
        in_specs=[
            pl.BlockSpec((1, 1, C), lambda b, e: (b * E + e, 0, 0), memory_space=pltpu.SMEM),
            pl.BlockSpec((1, SR, LANES), lambda b, e: (b, 0, 0), pipeline_mode=pl.Buffered(1)),
        ],
        out_specs=pl.BlockSpec((1, C, D_MODEL), lambda b, e: (e, b, 0)),
        out_shape=jax.ShapeDtypeStruct((E, B * C, D_MODEL), BF16),
        scratch_shapes=[pltpu.VMEM(((C + SUBLANES) * ROW_CHUNKS, LANES), F32)],
        compiler_params=_cparams(("arbitrary", "arbitrary")),
        name="gather",
    )(idx.reshape(B * E, 1, C), h2)


FF_TM = 1024
FF_TF = 1408
FF_CAST_ROWS = 128


def _cast_weight(src_ref, dst_ref):
    rows = dst_ref.shape[0]

    def chunk(i, _):
        r0 = pl.multiple_of(i * FF_CAST_ROWS, FF_CAST_ROWS)
        dst_ref[pl.ds(r0, FF_CAST_ROWS), :] = src_ref[0, pl.ds(r0, FF_CAST_ROWS), :].astype(BF16)
        return 0

    lax.fori_loop(0, rows // FF_CAST_ROWS, chunk, 0)


def _ffn_up_body(x_ref, wg_ref, wu_ref, he_ref, wgb_ref, wub_ref):
    @pl.when(pl.program_id(2) == 0)
    def _():
        _cast_weight(wg_ref, wgb_ref)
        _cast_weight(wu_ref, wub_ref)

    x = x_ref[0]
    hg = jnp.dot(x, wgb_ref[...], preferred_element_type=F32)
    hu = jnp.dot(x, wub_ref[...], preferred_element_type=F32)
    he_ref[0] = (hg * jax.nn.sigmoid(hg) * hu).astype(BF16)


def _ffn_down_body(he_ref, wd_ref, y_ref, wdb_ref):
    @pl.when(pl.program_id(1) == 0)
    def _():
        _cast_weight(wd_ref, wdb_ref)

    _store_token_major(y_ref, jnp.dot(he_ref[0], wdb_ref[...], preferred_element_type=F32))


def _ffn(xe, wg, wu, wd):
    E, M, _ = xe.shape
    assert M % FF_TM == 0, (M, FF_TM)
    nm = M // FF_TM
    nf = D_FF // FF_TF
    he = pl.pallas_call(
        _ffn_up_body,
        grid=(E, nf, nm),
        in_specs=[
            pl.BlockSpec((1, FF_TM, D_MODEL), lambda e, f, m: (e, m, 0)),
            pl.BlockSpec((1, D_MODEL, FF_TF), lambda e, f, m: (e, 0, f)),
            pl.BlockSpec((1, D_MODEL, FF_TF), lambda e, f, m: (e, 0, f)),
        ],
        out_specs=pl.BlockSpec((1, FF_TM, FF_TF), lambda e, f, m: (e, m, f)),
        out_shape=jax.ShapeDtypeStruct((E, M, D_FF), BF16),
        scratch_shapes=[pltpu.VMEM((D_MODEL, FF_TF), BF16), pltpu.VMEM((D_MODEL, FF_TF), BF16)],
        compiler_params=_cparams(("arbitrary", "arbitrary", "arbitrary")),
        name="expert_up",
    )(xe, wg, wu)
    return pl.pallas_call(
        _ffn_down_body,
        grid=(E, nm),
        in_specs=[
            pl.BlockSpec((1, FF_TM, D_FF), lambda e, m: (e, m, 0)),
            pl.BlockSpec((1, D_FF, D_MODEL), lambda e, m: (e, 0, 0)),
        ],
        out_specs=pl.BlockSpec((1, FF_TM * ROW_CHUNKS, LANES), lambda e, m: (e, m, 0)),
        out_shape=jax.ShapeDtypeStruct((E, M * ROW_CHUNKS, LANES), F32),
        scratch_shapes=[pltpu.VMEM((D_FF, D_MODEL), BF16)],
        compiler_params=_cparams(("arbitrary", "arbitrary")),
        name="expert_down",
    )(he, wd)


CB_UNROLL = 8
CB_TM = 512


def _combine_body(idx_ref, aff_ref, y_ref, x2_ref, g_ref, o_ref, acc_ref):
    E = N_EXPERTS
    C = idx_ref.shape[2]
    s = pl.program_id(1)

    @pl.when(s == 0)
    def _():
        def zero(i, _):
            r0 = pl.multiple_of(i * 1024, 1024)
            acc_ref[pl.ds(r0, 1024), :] = jnp.zeros((1024, LANES), F32)
            return 0
        lax.fori_loop(0, acc_ref.shape[0] // 1024, zero, 0)

    @pl.when(s < E)
    def _():
        def rows(i, _):
            dsts, sums = [], []
            for u in range(CB_UNROLL):
                r = i * CB_UNROLL + u
                t = idx_ref[0, 0, r]
                gate = aff_ref[0, 0, t]
                dst = pl.multiple_of(t * ROW_CHUNKS, ROW_CHUNKS)
                src = pl.multiple_of(r * ROW_CHUNKS, ROW_CHUNKS)
                dsts.append(dst)
                sums.append(acc_ref[pl.ds(dst, ROW_CHUNKS), :] + gate * y_ref[0, pl.ds(src, ROW_CHUNKS), :])
            for dst, v in zip(dsts, sums):
                acc_ref[pl.ds(dst, ROW_CHUNKS), :] = v
            return 0
        lax.fori_loop(0, C // CB_UNROLL, rows, 0)

    @pl.when(s >= E)
    def _():
        r0 = pl.multiple_of((s - E) * (CB_TM * ROW_CHUNKS), CB_TM * ROW_CHUNKS)
        moe = jnp.concatenate([acc_ref[pl.ds(r0 + j, CB_TM, stride=ROW_CHUNKS), :] for j in range(ROW_CHUNKS)],
                              axis=1)
        v = x2_ref[0] + moe
        ms = jnp.mean(v * v, axis=-1, keepdims=True)
        o_ref[0] = v * lax.rsqrt(ms + NORM_EPS) * g_ref[...]


def _combine(idx, aff, ye, x2, g_f):
    B, E, C = idx.shape
    S = x2.shape[1]
    SR = S * ROW_CHUNKS
    nfin = S // CB_TM
    return pl.pallas_call(
        _combine_body,
        grid=(B, E + nfin),
        in_specs=[
            pl.BlockSpec((1, 1, C), lambda b, s: (b * E + jnp.minimum(s, E - 1), 0, 0), memory_space=pltpu.SMEM),
            pl.BlockSpec((1, 1, S), lambda b, s: (b * E + jnp.minimum(s, E - 1), 0, 0), memory_space=pltpu.SMEM),
            pl.BlockSpec((1, C * ROW_CHUNKS, LANES), lambda b, s: (jnp.minimum(s, E - 1), b, 0)),
            pl.BlockSpec((1, CB_TM, D_MODEL), lambda b, s: (b, jnp.maximum(s - E, 0), 0)),
            pl.BlockSpec((1, D_MODEL), lambda b, s: (0, 0)),
        ],
        out_specs=pl.BlockSpec((1, CB_TM, D_MODEL), lambda b, s: (b, jnp.maximum(s - E, 0), 0)),
        out_shape=jax.ShapeDtypeStruct((B, S, D_MODEL), F32),
        scratch_shapes=[pltpu.VMEM((SR, LANES), F32)],
        compiler_params=_cparams(("arbitrary", "arbitrary")),
        name="combine",
    )(idx.reshape(B * E, 1, C), aff.reshape(B * E, 1, S), ye, x2, g_f)


def kernel(x, norm1_g, w_in, lam_q1, lam_k1, lam_q2, lam_k2, subln_g, w_out, rel_bias,
           norm2_g, w_router, w_gate, w_up, w_down, norm_f_g):
    B, S, _ = x.shape
    C = EC_K * S // N_EXPERTS
    e_bias = _diff_bias(rel_bias)
    m_bias = _dil_bias(rel_bias)
    p, p4, p16 = _in_proj(x, norm1_g, w_in[0].astype(BF16))
    oa = _diff_attn(p, e_bias, lam_q1, lam_k1, lam_q2, lam_k2, subln_g)
    od = _dil_attn(p, p4, p16, m_bias)
    x2, h2, aff = _out_proj(oa, od, x, w_out[0].astype(BF16), norm2_g, w_router[0].T)
    idx = _route(aff)
    xe = _gather(idx, h2)
    ye = _ffn(xe, w_gate[0], w_up[0], w_down[0])
    return _combine(idx, aff, ye, x2, norm_f_g.reshape(1, D_MODEL))
```

```python
import functools
import math

import jax
import jax.numpy as jnp
from jax import lax
from jax.experimental import pallas as pl
from jax.experimental.pallas import tpu as pltpu

F32 = jnp.float32
BF16 = jnp.bfloat16
I32 = jnp.int32

D_MODEL = 1024
HEAD_DIM = 64
N_DIFF_HEADS = 4
N_DIL_HEADS = 8
N_DIL_PAIRS = N_DIL_HEADS // 2
DIFF_WIDTH = 512
DIL_WIDTH = 512
N_IN = 3 * DIFF_WIDTH + 3 * DIL_WIDTH
DIL_PATTERNS = ((128, 1), (512, 4), (2048, 16))
N_BUCKETS = 32
REL_MAX_DIST = 1024
N_EXPERTS = 16
EC_K = 2
D_FF = 2816
NORM_EPS = 1e-6
SUBLN_EPS = 1e-5
NEG = -1e30
LAMBDA_INIT = 0.8 - 0.6 * math.exp(0.0)

LANES = 128
SUBLANES = 8
ROW_CHUNKS = D_MODEL // LANES

DA_TQ = 256
DA_TK = 512
DA_GROUP = 4
DA_EST = 128
DA_DMIN = -1280
DA_DMAX = 1024
DA_WTOT = DA_DMAX - DA_DMIN + DA_TK

DL_BQ = 128
DL_HALF = 64
DL_WIN = DL_BQ + 2 * DL_HALF
DL_GROUP = 16

LOG2E = math.log2(math.e)

VMEM_LIMIT = 56 * 1024 * 1024


def _cparams(sem):
    return pltpu.CompilerParams(dimension_semantics=sem, vmem_limit_bytes=VMEM_LIMIT)


def _t5_bucket(rel):
    half = N_BUCKETS // 2
    max_exact = half // 2
    n = jnp.abs(rel)
    nf = jnp.maximum(n, 1).astype(F32)
    large = max_exact + (jnp.log(nf / max_exact) / math.log(REL_MAX_DIST / max_exact)
                         * (half - max_exact)).astype(I32)
    large = jnp.minimum(large, half - 1)
    return jnp.where(rel > 0, half, 0) + jnp.where(n < max_exact, n, large)


def _bias_lookup(rel, tab_ref, col):
    bucket = _t5_bucket(rel)
    acc = jnp.zeros(rel.shape, F32)
    for j in range(N_BUCKETS):
        acc = jnp.where(bucket == j, tab_ref[j, col], acc)
    return acc


def _diff_bias_body(tab_ref, e_ref):
    h = pl.program_id(0)

    def chunk(cc, _):
        c0 = pl.multiple_of(cc * LANES, LANES)
        i = lax.broadcasted_iota(I32, (DA_TQ, LANES), 0)
        c = lax.broadcasted_iota(I32, (DA_TQ, LANES), 1) + c0
        e_ref[0, :, pl.ds(c0, LANES)] = _bias_lookup(c - i + DA_DMIN, tab_ref, h) * LOG2E
        return 0

    lax.fori_loop(0, DA_WTOT // LANES, chunk, 0)


def _diff_bias(rel_bias):
    return pl.pallas_call(
        _diff_bias_body,
        grid=(N_DIFF_HEADS,),
        in_specs=[pl.BlockSpec(memory_space=pltpu.SMEM)],
        out_specs=pl.BlockSpec((1, DA_TQ, DA_WTOT), lambda h: (h, 0, 0)),
        out_shape=jax.ShapeDtypeStruct((N_DIFF_HEADS, DA_TQ, DA_WTOT), F32),
        compiler_params=_cparams(("arbitrary",)),
        name="diff_bias",
    )(rel_bias)


def _dil_bias_body(tab_ref, m_ref):
    p = pl.program_id(0)
    h = pl.program_id(1)
    dil = jnp.where(p == 0, DIL_PATTERNS[0][1], jnp.where(p == 1, DIL_PATTERNS[1][1], DIL_PATTERNS[2][1]))
    i = lax.broadcasted_iota(I32, (DL_BQ, DL_WIN), 0)
    c = lax.broadcasted_iota(I32, (DL_BQ, DL_WIN), 1)
    off = c - DL_HALF - i
    bias = _bias_lookup(off * dil, tab_ref, N_DIFF_HEADS + h) * LOG2E
    band = jnp.abs(off) <= DL_HALF
    m_ref[0, 0, 0] = jnp.where(band & (c >= DL_HALF), bias, NEG)
    m_ref[0, 0, 1] = jnp.where(band, bias, NEG)
    m_ref[0, 0, 2] = jnp.where(band & (c < DL_HALF + DL_BQ), bias, NEG)


def _dil_bias(rel_bias):
    n_pat = len(DIL_PATTERNS)
    return pl.pallas_call(
        _dil_bias_body,
        grid=(n_pat, N_DIL_HEADS),
        in_specs=[pl.BlockSpec(memory_space=pltpu.SMEM)],
        out_specs=pl.BlockSpec((1, 1, 3, DL_BQ, DL_WIN), lambda p, h: (p, h, 0, 0, 0)),
        out_shape=jax.ShapeDtypeStruct((n_pat, N_DIL_HEADS, 3, DL_BQ, DL_WIN), F32),
        compiler_params=_cparams(("arbitrary", "arbitrary")),
        name="dil_bias",
    )(rel_bias)


IP_TM = 512
IP_TN = 256
N_COLBLK = N_IN // LANES
N_DIL_COLBLK = 3 * DIL_WIDTH // LANES


def _in_proj_body(x_ref, g_ref, w_ref, p_ref, p4_ref, p16_ref, scr_ref):
    q_scale = HEAD_DIM ** -0.5 * LOG2E
    x = x_ref[0]
    ms = jnp.mean(x * x, axis=-1, keepdims=True)
    y = (x * lax.rsqrt(ms + NORM_EPS) * g_ref[...]).astype(BF16)
    for n in range(N_IN // IP_TN):
        c0 = n * IP_TN
        r = jnp.dot(y, w_ref[:, c0:c0 + IP_TN], preferred_element_type=F32)
        is_q = (c0 < DIFF_WIDTH) or (3 * DIFF_WIDTH <= c0 < 3 * DIFF_WIDTH + DIL_WIDTH)
        if is_q:
            r = r * q_scale
        for k in range(IP_TN // LANES):
            p_ref[0, 2 * n + k] = r[:, k * LANES:(k + 1) * LANES].astype(BF16)
        if c0 >= 3 * DIFF_WIDTH:
            for k in range(IP_TN // LANES):
                scr_ref[(c0 - 3 * DIFF_WIDTH) // LANES + k] = r[:, k * LANES:(k + 1) * LANES]
    for c in range(N_DIL_COLBLK):
        for d, dst_ref in ((4, p4_ref), (16, p16_ref)):
            for r_ in range(d):
                dst_ref[0, c, r_] = scr_ref[c, pl.ds(r_, IP_TM // d, stride=d), :].astype(BF16)


def _in_proj(x, g, w_bf16):
    B, S, _ = x.shape
    nt = S // IP_TM
    return pl.pallas_call(
        _in_proj_body,
        grid=(B, nt),
        in_specs=[
            pl.BlockSpec((1, IP_TM, D_MODEL), lambda b, t: (b, t, 0)),
            pl.BlockSpec((1, D_MODEL), lambda b, t: (0, 0)),
            pl.BlockSpec((D_MODEL, N_IN), lambda b, t: (0, 0)),
        ],
        out_specs=[
            pl.BlockSpec((1, N_COLBLK, IP_TM, LANES), lambda b, t: (b, 0, t, 0)),
            pl.BlockSpec((1, N_DIL_COLBLK, 4, IP_TM // 4, LANES), lambda b, t: (b, 0, 0, t, 0)),
            pl.BlockSpec((1, N_DIL_COLBLK, 16, IP_TM // 16, LANES), lambda b, t: (b, 0, 0, t, 0)),
        ],
        out_shape=[
            jax.ShapeDtypeStruct((B, N_COLBLK, S, LANES), BF16),
            jax.ShapeDtypeStruct((B, N_DIL_COLBLK, 4, S // 4, LANES), BF16),
            jax.ShapeDtypeStruct((B, N_DIL_COLBLK, 16, S // 16, LANES), BF16),
        ],
        scratch_shapes=[pltpu.VMEM((N_DIL_COLBLK, IP_TM, LANES), F32)],
        compiler_params=_cparams(("arbitrary", "arbitrary")),
        name="in_proj",
    )(x, g, w_bf16)


_NT_DIMS = (((1,), (1,)), ((), ()))


def _diff_attn_body(q_ref, k_ref, v_ref, e_ref, lq1_ref, lk1_ref, lq2_ref, lk2_ref, sg_ref, o_ref,
                    kx_ref, vx_ref, p_ref):
    S = q_ref.shape[2]
    TQ, TK = DA_TQ, DA_TK
    lam = (jnp.exp(jnp.sum(lq1_ref[...] * lk1_ref[...], axis=-1, keepdims=True))
           - jnp.exp(jnp.sum(lq2_ref[...] * lk2_ref[...], axis=-1, keepdims=True)) + LAMBDA_INIT)
    lane_q = lax.broadcasted_iota(I32, (TQ, LANES), 1)
    lo = lane_q < HEAD_DIM
    lane2 = lax.broadcasted_iota(I32, (2 * TQ, LANES), 1)

    def fill(i, _):
        r0 = pl.multiple_of(i * 512, 512)
        e0 = jnp.where(lax.broadcasted_iota(I32, (512, LANES), 1) == 0, 1.0, 0.0).astype(BF16)
        kx_ref[pl.ds(r0, 512), 0:LANES] = k_ref[0, 0, pl.ds(r0, 512), :]
        kx_ref[pl.ds(r0, 512), LANES:2 * LANES] = e0
        vx_ref[pl.ds(r0, 512), 0:LANES] = v_ref[0, 0, pl.ds(r0, 512), :]
        vx_ref[pl.ds(r0, 512), LANES:2 * LANES] = e0
        return 0

    lax.fori_loop(0, S // 512, fill, 0)

    def q_tile(q0, exact_shift, slot=0):
        ps_ref = p_ref.at[slot]
        q = q_ref[0, 0, pl.ds(q0, TQ), :]
        zero = jnp.zeros_like(q)
        qa = jnp.concatenate([jnp.where(lo, q, zero), jnp.where(lo, zero, q)], axis=0)

        def bias_tile(k0):
            off = pl.multiple_of(jnp.clip(k0 - q0, DA_DMIN, DA_DMAX) - DA_DMIN, LANES)
            return e_ref[0, :, pl.ds(off, TK)]

        def tile_max(k0, pm, width=TK):
            s = lax.dot_general(qa, k_ref[0, 0, pl.ds(k0, width), :], _NT_DIMS, preferred_element_type=F32)
            bias = bias_tile(k0)[:, 0:width]
            out = []
            for half in range(2):
                sb = s[half * TQ:(half + 1) * TQ] + bias
                t = pm[half]
                for j in range(width // LANES):
                    t = jnp.maximum(t, sb[:, j * LANES:(j + 1) * LANES])
                out.append(t)
            return tuple(out)

        neg = jnp.full((TQ, LANES), -jnp.inf, F32)
        if exact_shift:
            pm = lax.fori_loop(0, S // TK, lambda ki, pm: tile_max(pl.multiple_of(ki * TK, TK), pm), (neg, neg))
        else:
            pm = tile_max(0, (neg, neg), width=DA_EST)
        m = jnp.max(jnp.concatenate(pm, axis=0), axis=-1, keepdims=True)
        mcol = jnp.where(lane2 == 0, -m, 0.0).astype(BF16)
        qx = jnp.concatenate([qa, mcol], axis=1)

        for ki in range(S // TK):
            k0 = ki * TK
            s = lax.dot_general(qx, kx_ref[k0:k0 + TK, :], _NT_DIMS, preferred_element_type=F32)
            bias = bias_tile(k0)
            for half in range(2):
                pr = jnp.exp2(s[half * TQ:(half + 1) * TQ] + bias).astype(BF16)
                ps_ref[half * TQ:(half + 1) * TQ, k0:k0 + TK] = pr

        acc = jnp.dot(ps_ref[...], vx_ref[...], preferred_element_type=F32)
        o1 = acc[0:TQ, 0:LANES] / acc[0:TQ, LANES:LANES + 1]
        o2 = acc[TQ:2 * TQ, 0:LANES] / acc[TQ:2 * TQ, LANES:LANES + 1]
        o = o1 - lam * o2
        ms = jnp.mean(o * o, axis=-1, keepdims=True)
        o = o * lax.rsqrt(ms + SUBLN_EPS) * sg_ref[...] * (1.0 - LAMBDA_INIT)
        o_ref[0, 0, pl.ds(q0, TQ), :] = o.astype(BF16)
        return jnp.where(jnp.isfinite(o), 0.0, 1.0)

    def fast_loop(qi, bad):
        for u in range(DA_GROUP):
            q0 = pl.multiple_of((qi * DA_GROUP + u) * TQ, TQ)
            bad = jnp.maximum(bad, q_tile(q0, exact_shift=False, slot=u))
        return bad

    bad = lax.fori_loop(0, S // (TQ * DA_GROUP), fast_loop, jnp.zeros((TQ, LANES), F32))

    @pl.when(jnp.max(bad) > 0.0)
    def _():
        def exact_loop(qi, _):
            q_tile(pl.multiple_of(qi * TQ, TQ), exact_shift=True)
            return 0
        lax.fori_loop(0, S // TQ, exact_loop, 0)


def _diff_attn(p, e_bias, lq1, lk1, lq2, lk2, subln_g):
    B, _, S, _ = p.shape
    H = N_DIFF_HEADS
    vec = lambda n: pl.BlockSpec((1, n), lambda b, h: (0, 0))
    return pl.pallas_call(
        _diff_attn_body,
        grid=(B, H),
        in_specs=[
            pl.BlockSpec((1, 1, S, LANES), lambda b, h: (b, h, 0, 0)),
            pl.BlockSpec((1, 1, S, LANES), lambda b, h: (b, H + h, 0, 0)),
            pl.BlockSpec((1, 1, S, LANES), lambda b, h: (b, 2 * H + h, 0, 0)),
            pl.BlockSpec((1, DA_TQ, DA_WTOT), lambda b, h: (h, 0, 0)),
            vec(HEAD_DIM), vec(HEAD_DIM), vec(HEAD_DIM), vec(HEAD_DIM), vec(2 * HEAD_DIM),
        ],
        out_specs=pl.BlockSpec((1, 1, S, LANES), lambda b, h: (b, h, 0, 0)),
        out_shape=jax.ShapeDtypeStruct((B, H, S, LANES), BF16),
        scratch_shapes=[
            pltpu.VMEM((S, 2 * LANES), BF16),
            pltpu.VMEM((S, 2 * LANES), BF16),
            pltpu.VMEM((DA_GROUP, 2 * DA_TQ, S), BF16),
        ],
        compiler_params=_cparams(("arbitrary", "arbitrary")),
        name="diff_attn",
    )(p, p, p, e_bias, lq1, lk1, lq2, lk2, subln_g)


def _dil_attn_body(q1_ref, k1_ref, v1_ref, q4_ref, k4_ref, v4_ref, q16_ref, k16_ref, v16_ref, m_ref,
                   o_ref, kpad_ref, vlo_ref, vhi_ref, oacc_ref, lse_ref):
    S = q1_ref.shape[2]
    lane = lax.broadcasted_iota(I32, (DL_BQ, LANES), 1)
    lo = lane < HEAD_DIM
    zpad = jnp.zeros((DL_HALF, LANES), BF16)

    def ones_fill(i, _):
        r0 = pl.multiple_of(i * 512, 512)
        lo_f = lax.broadcasted_iota(I32, (512, LANES), 1) < HEAD_DIM
        vlo_ref[pl.ds(r0, 512), LANES:2 * LANES] = jnp.where(lo_f, 1.0, 0.0).astype(BF16)
        vhi_ref[pl.ds(r0, 512), LANES:2 * LANES] = jnp.where(lo_f, 0.0, 1.0).astype(BF16)
        return 0

    lax.fori_loop(0, vlo_ref.shape[0] // 512, ones_fill, 0)

    def run_pattern(p, dil, get_q, get_k, get_v):
        L = S // dil
        nblk = L // DL_BQ
        LP = L + 2 * DL_HALF

        def fill(r, _):
            base = pl.multiple_of(r * LP, DL_HALF)
            for off in (0, DL_HALF + L):
                kpad_ref[pl.ds(base + off, DL_HALF), :] = zpad
                vlo_ref[pl.ds(base + off, DL_HALF), 0:LANES] = zpad
                vhi_ref[pl.ds(base + off, DL_HALF), 0:LANES] = zpad
            kpad_ref[pl.ds(base + DL_HALF, L), :] = get_k(r)[...]
            v = get_v(r)[...]
            lo_v = lax.broadcasted_iota(I32, v.shape, 1) < HEAD_DIM
            zero = jnp.zeros_like(v)
            vlo_ref[pl.ds(base + DL_HALF, L), 0:LANES] = jnp.where(lo_v, v, zero)
            vhi_ref[pl.ds(base + DL_HALF, L), 0:LANES] = jnp.where(lo_v, zero, v)
            return 0

        lax.fori_loop(0, dil, fill, 0)

        def group(gi, _):
            for u in range(DL_GROUP):
                n = gi * DL_GROUP + u
                r = n // nblk
                blk = n % nblk
                q0 = pl.multiple_of(blk * DL_BQ, DL_BQ)
                w0 = pl.multiple_of(r * LP + q0, DL_HALF)
                q = get_q(r)[pl.ds(q0, DL_BQ), :]
                kw = kpad_ref[pl.ds(w0, DL_WIN), :]
                vw = jnp.concatenate([vlo_ref[pl.ds(w0, DL_WIN), :], vhi_ref[pl.ds(w0, DL_WIN), :]], axis=0)
                edge = jnp.where(blk == 0, 0, jnp.where(blk == nblk - 1, 2, 1))
                zero = jnp.zeros_like(q)
                probs, maxes = [], []
                for hh in range(2):
                    qm = jnp.where(lo, q, zero) if hh == 0 else jnp.where(lo, zero, q)
                    s = lax.dot_general(qm, kw, _NT_DIMS, preferred_element_type=F32) + m_ref[p, hh, edge]
                    m = jnp.max(s, axis=-1, keepdims=True)
                    probs.append(jnp.exp2(s - m).astype(BF16))
                    maxes.append(m)
                both = jnp.dot(jnp.concatenate(probs, axis=1), vw, preferred_element_type=F32)
                l = both[:, LANES:2 * LANES]
                o = both[:, 0:LANES] / l
                lse = jnp.where(lo, maxes[0], maxes[1]) + jnp.log2(l)
                if dil == 1:
                    oacc_ref[p, pl.ds(q0, DL_BQ), :] = o
                    lse_ref[p, pl.ds(q0, DL_BQ), :] = lse
                else:
                    oacc_ref[p, pl.ds(q0 * dil + r, DL_BQ, stride=dil), :] = o
                    lse_ref[p, pl.ds(q0 * dil + r, DL_BQ, stride=dil), :] = lse
            return 0

        lax.fori_loop(0, dil * nblk // DL_GROUP, group, 0)

    run_pattern(0, 1, lambda r: q1_ref.at[0, 0], lambda r: k1_ref.at[0, 0], lambda r: v1_ref.at[0, 0])
    run_pattern(1, 4, lambda r: q4_ref.at[0, 0, r], lambda r: k4_ref.at[0, 0, r], lambda r: v4_ref.at[0, 0, r])
    run_pattern(2, 16, lambda r: q16_ref.at[0, 0, r], lambda r: k16_ref.at[0, 0, r], lambda r: v16_ref.at[0, 0, r])

    MIX = 256

    def mix(i, _):
        r0 = pl.multiple_of(i * MIX, MIX)
        ls = [lse_ref[p, pl.ds(r0, MIX), :] for p in range(3)]
        mx = jnp.maximum(jnp.maximum(ls[0], ls[1]), ls[2])
        ws = [jnp.exp2(l - mx) for l in ls]
        den = ws[0] + ws[1] + ws[2]
        num = ws[0] * oacc_ref[0, pl.ds(r0, MIX), :]
        num += ws[1] * oacc_ref[1, pl.ds(r0, MIX), :]
        num += ws[2] * oacc_ref[2, pl.ds(r0, MIX), :]
        o_ref[0, 0, pl.ds(r0, MIX), :] = (num / den).astype(BF16)
        return 0

    lax.fori_loop(0, S // MIX, mix, 0)


def _dil_attn(p, p4, p16, m_bias):
    B, _, S, _ = p.shape
    NP = N_DIL_PAIRS
    base = 3 * N_DIFF_HEADS
    nat = lambda o: pl.BlockSpec((1, 1, S, LANES), lambda b, h: (b, base + o * NP + h, 0, 0))
    res = lambda o, d: pl.BlockSpec((1, 1, d, S // d, LANES), lambda b, h: (b, o * NP + h, 0, 0, 0))
    return pl.pallas_call(
        _dil_attn_body,
        grid=(B, NP),
        in_specs=[nat(0), nat(1), nat(2), res(0, 4), res(1, 4), res(2, 4), res(0, 16), res(1, 16), res(2, 16),
                  pl.BlockSpec((3, 2, 3, DL_BQ, DL_WIN), lambda b, h: (0, h, 0, 0, 0))],
        out_specs=pl.BlockSpec((1, 1, S, LANES), lambda b, h: (b, h, 0, 0)),
        out_shape=jax.ShapeDtypeStruct((B, NP, S, LANES), BF16),
        scratch_shapes=[
            pltpu.VMEM((S + 2 * DL_HALF * DIL_PATTERNS[-1][1], LANES), BF16),
            pltpu.VMEM((S + 2 * DL_HALF * DIL_PATTERNS[-1][1], 2 * LANES), BF16),
            pltpu.VMEM((S + 2 * DL_HALF * DIL_PATTERNS[-1][1], 2 * LANES), BF16),
            pltpu.VMEM((3, S, LANES), F32),
            pltpu.VMEM((3, S, LANES), F32),
        ],
        compiler_params=_cparams(("arbitrary", "arbitrary")),
        name="dil_attn",
    )(p, p, p, p4, p4, p4, p16, p16, p16, m_bias)


OP_TM = 512


def _store_token_major(ref, val):
    rows = val.shape[0]
    for j in range(ROW_CHUNKS):
        ref[0, pl.ds(j, rows, stride=ROW_CHUNKS), :] = val[:, j * LANES:(j + 1) * LANES]


def _out_proj_body(oa_ref, od_ref, x_ref, w_ref, g_ref, wr_ref, x2_ref, h2_ref, aff_ref):
    pieces = [oa_ref[0, i] for i in range(N_DIFF_HEADS)] + [od_ref[0, i] for i in range(N_DIL_PAIRS)]
    a = jnp.concatenate(pieces, axis=-1)
    x2 = x_ref[0] + jnp.dot(a, w_ref[...], preferred_element_type=F32)
    ms = jnp.mean(x2 * x2, axis=-1, keepdims=True)
    h = x2 * lax.rsqrt(ms + NORM_EPS) * g_ref[...]
    h_hi = h.astype(BF16)
    h_lo = (h - h_hi.astype(F32)).astype(BF16)
    wr = wr_ref[...]
    w_hi = wr.astype(BF16)
    w_lo = (wr - w_hi.astype(F32)).astype(BF16)
    logits = (lax.dot_general(w_hi, h_hi, _NT_DIMS, preferred_element_type=F32)
              + lax.dot_general(w_hi, h_lo, _NT_DIMS, preferred_element_type=F32)
              + lax.dot_general(w_lo, h_hi, _NT_DIMS, preferred_element_type=F32))
    mx = jnp.max(logits, axis=0, keepdims=True)
    ex = jnp.exp(logits - mx)
    aff_ref[0] = ex / jnp.sum(ex, axis=0, keepdims=True)
    x2_ref[0] = x2
    _store_token_major(h2_ref, h)


def _out_proj(oa, od, x, w_bf16, g, wr_t):
    B, S, _ = x.shape
    nt = S // OP_TM
    attn = lambda n: pl.BlockSpec((1, n, OP_TM, LANES), lambda b, t: (b, 0, t, 0))
    tok_major = pl.BlockSpec((1, OP_TM * ROW_CHUNKS, LANES), lambda b, t: (b, t, 0))
    return pl.pallas_call(
        _out_proj_body,
        grid=(B, nt),
        in_specs=[
            attn(N_DIFF_HEADS), attn(N_DIL_PAIRS),
            pl.BlockSpec((1, OP_TM, D_MODEL), lambda b, t: (b, t, 0)),
            pl.BlockSpec((D_MODEL, D_MODEL), lambda b, t: (0, 0)),
            pl.BlockSpec((1, D_MODEL), lambda b, t: (0, 0)),
            pl.BlockSpec((N_EXPERTS, D_MODEL), lambda b, t: (0, 0)),
        ],
        out_specs=[pl.BlockSpec((1, OP_TM, D_MODEL), lambda b, t: (b, t, 0)), tok_major,
                   pl.BlockSpec((1, N_EXPERTS, OP_TM), lambda b, t: (b, 0, t))],
        out_shape=[
            jax.ShapeDtypeStruct((B, S, D_MODEL), F32),
            jax.ShapeDtypeStruct((B, S * ROW_CHUNKS, LANES), F32),
            jax.ShapeDtypeStruct((B, N_EXPERTS, S), F32),
        ],
        compiler_params=_cparams(("arbitrary", "arbitrary")),
        name="out_proj",
    )(oa, od, x, w_bf16, g, wr_t)


RT_GROUP = 2


def _route_body(aff_ref, idx_ref, xh_ref, xl_ref, st_ref, en_ref):
    E, S = aff_ref.shape[1], aff_ref.shape[2]
    C = idx_ref.shape[2]
    n_chunks = S // LANES
    a = aff_ref[0]
    bits = pltpu.bitcast(a, I32)
    tau = jnp.zeros((E, 1), I32)
    for bit in range(30, -1, -1):
        cand = tau | (1 << bit)
        cnt = jnp.sum(jnp.where(bits >= cand, 1.0, 0.0).astype(F32), axis=1, keepdims=True)
        tau = jnp.where(cnt >= C, cand, tau)
    gt = bits > tau
    eq = bits == tau
    need = C - jnp.sum(jnp.where(gt, 1.0, 0.0).astype(F32), axis=1, keepdims=True)

    tri = (lax.broadcasted_iota(I32, (LANES, LANES), 0) <= lax.broadcasted_iota(I32, (LANES, LANES), 1)).astype(BF16)

    def chunk_cumsum(mj, carry):
        return jnp.dot(mj.astype(BF16), tri, preferred_element_type=F32) + carry

    sel_chunks = []
    carry = jnp.zeros((E, 1), F32)
    for j in range(n_chunks):
        cs = slice(j * LANES, (j + 1) * LANES)
        eq_j = jnp.where(eq[:, cs], 1.0, 0.0).astype(F32)
        inc = chunk_cumsum(eq_j, carry)
        carry = inc[:, LANES - 1:LANES]
        take_tie = jnp.where(inc - eq_j < need, eq_j, 0.0)
        sel_chunks.append(jnp.where(gt[:, cs], 1.0, take_tie))

    lane_e = lax.broadcasted_iota(I32, (E, LANES), 1)
    st = jnp.full((E, LANES), float(C), F32)
    en = jnp.full((E, LANES), float(C), F32)
    carry = jnp.zeros((E, 1), F32)
    for j in range(n_chunks):
        sel_j = sel_chunks[j]
        inc = chunk_cumsum(sel_j, carry)
        posm = jnp.where(sel_j > 0.0, inc - sel_j, -1.0)
        hi = jnp.floor(posm * 0.5)
        xh_ref[pl.ds(j * E, E), :] = hi
        xl_ref[pl.ds(j * E, E), :] = posm - 2.0 * hi
        st = jnp.where(lane_e == j, carry, st)
        carry = inc[:, LANES - 1:LANES]
        en = jnp.where(lane_e == j, carry, en)
    pad_rows = (LANES - n_chunks) * E
    xh_ref[pl.ds(n_chunks * E, pad_rows), :] = jnp.zeros((pad_rows, LANES), F32)
    xl_ref[pl.ds(n_chunks * E, pad_rows), :] = jnp.zeros((pad_rows, LANES), F32)
    st_ref[...] = st
    en_ref[...] = en

    c_iota = lax.broadcasted_iota(I32, (C, LANES), 0).astype(F32)
    lane_row = jnp.where(lax.broadcasted_iota(I32, (SUBLANES, LANES), 0) == 0,
                         lax.broadcasted_iota(I32, (SUBLANES, LANES), 1), 0).astype(F32).astype(BF16)

    def expert_group(g, _):
        for u in range(RT_GROUP):
            e = g * RT_GROUP + u
            st_e = st_ref[pl.ds(e, 1), :]
            en_e = en_ref[pl.ds(e, 1), :]
            in_chunk = jnp.where(st_e <= c_iota, jnp.where(c_iota < en_e, 1.0, 0.0), 0.0).astype(BF16)
            xh = xh_ref[pl.ds(e, LANES, stride=E), :].astype(BF16)
            xl = xl_ref[pl.ds(e, LANES, stride=E), :].astype(BF16)
            hl = jnp.dot(in_chunk, jnp.concatenate([xh, xl], axis=1), preferred_element_type=F32)
            slot_of = 2.0 * hl[:, 0:LANES] + hl[:, LANES:2 * LANES]
            hit = jnp.where(slot_of == c_iota, 1.0, 0.0).astype(BF16)
            chunk_id = lax.dot_general(lane_row, in_chunk, _NT_DIMS, preferred_element_type=F32)
            lane_id = lax.dot_general(lane_row, hit, _NT_DIMS, preferred_element_type=F32)
            idx_ref[0, pl.ds(e, 1), :] = (chunk_id[0:1, :] * float(LANES) + lane_id[0:1, :] + 0.5).astype(I32)
        return 0

    lax.fori_loop(0, E // RT_GROUP, expert_group, 0)


def _route(aff):
    B, E, S = aff.shape
    C = EC_K * S // E
    return pl.pallas_call(
        _route_body,
        grid=(B,),
        in_specs=[pl.BlockSpec((1, E, S), lambda b: (b, 0, 0))],
        out_specs=pl.BlockSpec((1, E, C), lambda b: (b, 0, 0)),
        out_shape=jax.ShapeDtypeStruct((B, E, C), I32),
        scratch_shapes=[pltpu.VMEM((LANES * E, LANES), F32), pltpu.VMEM((LANES * E, LANES), F32),
                        pltpu.VMEM((E, LANES), F32), pltpu.VMEM((E, LANES), F32)],
        compiler_params=_cparams(("arbitrary",)),
        name="route",
    )(aff)


GA_UNROLL = 16
GA_EXPERTS = 4


def _gather_body(idx_ref, h_ref, xe_ref, g_ref):
    C = idx_ref.shape[2] // GA_EXPERTS
    pitch = g_ref.shape[0] // ROW_CHUNKS

    for ex in range(GA_EXPERTS):
        def rows(i, _):
            for u in range(GA_UNROLL):
                r = i * GA_UNROLL + u
                t = idx_ref[0, 0, ex * C + r]
                src = pl.multiple_of(t * ROW_CHUNKS, ROW_CHUNKS)
                g_ref[pl.ds(r, ROW_CHUNKS, stride=pitch), :] = h_ref[0, pl.ds(src, ROW_CHUNKS), :]
            return 0

        lax.fori_loop(0, C // GA_UNROLL, rows, 0)
        for j in range(ROW_CHUNKS):
            xe_ref[ex, :, j * LANES:(j + 1) * LANES] = g_ref[j * pitch:j * pitch + C, :].astype(BF16)


def _gather(idx, h2):
    B, E, C = idx.shape
    SR = h2.shape[1]
    G = GA_EXPERTS
    return pl.pallas_call(
        _gather_body,
        grid=(B, E // G),
        in_specs=[
            pl.BlockSpec((1, 1, G * C), lambda b, e: (b * (E // G) + e, 0, 0), memory_space=pltpu.SMEM),
            pl.BlockSpec((1, SR, LANES), lambda b, e: (b, 0, 0), pipeline_mode=pl.Buffered(1)),
        ],
        out_specs=pl.BlockSpec((G, C, D_MODEL), lambda b, e: (e, b, 0)),
        out_shape=jax.ShapeDtypeStruct((E, B * C, D_MODEL), BF16),
        scratch_shapes=[pltpu.VMEM(((C + SUBLANES) * ROW_CHUNKS, LANES), F32)],
        compiler_params=_cparams(("arbitrary", "arbitrary")),
        name="gather",
    )(idx.reshape(B * E // G, 1, G * C), h2)


FF_TM = 1024
FF_TF = 1408
FF_CAST_ROWS = 128


def _cast_weight(src_ref, dst_ref):
    rows = dst_ref.shape[0]

    def chunk(i, _):
        r0 = pl.multiple_of(i * FF_CAST_ROWS, FF_CAST_ROWS)
        dst_ref[pl.ds(r0, FF_CAST_ROWS), :] = src_ref[0, pl.ds(r0, FF_CAST_ROWS), :].astype(BF16)
        return 0

    lax.fori_loop(0, rows // FF_CAST_ROWS, chunk, 0)


def _ffn_up_body(x_ref, wg_ref, wu_ref, he_ref, wgb_ref, wub_ref):
    @pl.when(pl.program_id(2) == 0)
    def _():
        _cast_weight(wg_ref, wgb_ref)
        _cast_weight(wu_ref, wub_ref)

    x = x_ref[0]
    hg = jnp.dot(x, wgb_ref[...], preferred_element_type=F32)
    hu = jnp.dot(x, wub_ref[...], preferred_element_type=F32)
    he_ref[0] = (hg * jax.nn.sigmoid(hg) * hu).astype(BF16)


def _ffn_down_body(he_ref, wd_ref, y_ref, wdb_ref):
    @pl.when(pl.program_id(1) == 0)
    def _():
        _cast_weight(wd_ref, wdb_ref)

    _store_token_major(y_ref, jnp.dot(he_ref[0], wdb_ref[...], preferred_element_type=F32))


def _ffn(xe, wg, wu, wd):
    E, M, _ = xe.shape
    assert M % FF_TM == 0, (M, FF_TM)
    nm = M // FF_TM
    nf = D_FF // FF_TF
    he = pl.pallas_call(
        _ffn_up_body,
        grid=(E, nf, nm),
        in_specs=[
            pl.BlockSpec((1, FF_TM, D_MODEL), lambda e, f, m: (e, m, 0)),
            pl.BlockSpec((1, D_MODEL, FF_TF), lambda e, f, m: (e, 0, f)),
            pl.BlockSpec((1, D_MODEL, FF_TF), lambda e, f, m: (e, 0, f)),
        ],
        out_specs=pl.BlockSpec((1, FF_TM, FF_TF), lambda e, f, m: (e, m, f)),
        out_shape=jax.ShapeDtypeStruct((E, M, D_FF), BF16),
        scratch_shapes=[pltpu.VMEM((D_MODEL, FF_TF), BF16), pltpu.VMEM((D_MODEL, FF_TF), BF16)],
        compiler_params=_cparams(("arbitrary", "arbitrary", "arbitrary")),
        name="expert_up",
    )(xe, wg, wu)
    return pl.pallas_call(
        _ffn_down_body,
        grid=(E, nm),
        in_specs=[
            pl.BlockSpec((1, FF_TM, D_FF), lambda e, m: (e, m, 0)),
            pl.BlockSpec((1, D_FF, D_MODEL), lambda e, m: (e, 0, 0)),
        ],
        out_specs=pl.BlockSpec((1, FF_TM * ROW_CHUNKS, LANES), lambda e, m: (e, m, 0)),
        out_shape=jax.ShapeDtypeStruct((E, M * ROW_CHUNKS, LANES), F32),
        scratch_shapes=[pltpu.VMEM((D_FF, D_MODEL), BF16)],
        compiler_params=_cparams(("arbitrary", "arbitrary")),
        name="expert_down",
    )(he, wd)


CB_UNROLL = 8
CB_TM = 1024
CB_EXPERTS = 2


def _combine_body(idx_ref, aff_ref, y_ref, x2_ref, g_ref, o_ref, acc_ref):
    E = N_EXPERTS // CB_EXPERTS
    C = idx_ref.shape[2] // CB_EXPERTS
    S = aff_ref.shape[2] // CB_EXPERTS
    s = pl.program_id(1)

    @pl.when(s == 0)
    def _():
        def zero(i, _):
            r0 = pl.multiple_of(i * 1024, 1024)
            acc_ref[pl.ds(r0, 1024), :] = jnp.zeros((1024, LANES), F32)
            return 0
        lax.fori_loop(0, acc_ref.shape[0] // 1024, zero, 0)

    @pl.when(s < E)
    def _():
        for ex in range(CB_EXPERTS):
            def rows(i, _):
                dsts, sums = [], []
                for u in range(CB_UNROLL):
                    r = i * CB_UNROLL + u
                    t = idx_ref[0, 0, ex * C + r]
                    gate = aff_ref[0, 0, ex * S + t]
                    dst = pl.multiple_of(t * ROW_CHUNKS, ROW_CHUNKS)
                    src = pl.multiple_of(r * ROW_CHUNKS, ROW_CHUNKS)
                    dsts.append(dst)
                    sums.append(acc_ref[pl.ds(dst, ROW_CHUNKS), :] + gate * y_ref[ex, pl.ds(src, ROW_CHUNKS), :])
                for dst, v in zip(dsts, sums):
                    acc_ref[pl.ds(dst, ROW_CHUNKS), :] = v
                return 0
            lax.fori_loop(0, C // CB_UNROLL, rows, 0)

    @pl.when(s >= E)
    def _():
        r0 = pl.multiple_of((s - E) * (CB_TM * ROW_CHUNKS), CB_TM * ROW_CHUNKS)
        moe = jnp.concatenate([acc_ref[pl.ds(r0 + j, CB_TM, stride=ROW_CHUNKS), :] for j in range(ROW_CHUNKS)],
                              axis=1)
        v = x2_ref[0] + moe
        ms = jnp.mean(v * v, axis=-1, keepdims=True)
        o_ref[0] = v * lax.rsqrt(ms + NORM_EPS) * g_ref[...]


def _combine(idx, aff, ye, x2, g_f):
    B, E, C = idx.shape
    S = x2.shape[1]
    SR = S * ROW_CHUNKS
    nfin = S // CB_TM
    G = CB_EXPERTS
    ns = E // G
    return pl.pallas_call(
        _combine_body,
        grid=(B, ns + nfin),
        in_specs=[
            pl.BlockSpec((1, 1, G * C), lambda b, s: (b * ns + jnp.minimum(s, ns - 1), 0, 0), memory_space=pltpu.SMEM),
            pl.BlockSpec((1, 1, G * S), lambda b, s: (b * ns + jnp.minimum(s, ns - 1), 0, 0), memory_space=pltpu.SMEM),
            pl.BlockSpec((G, C * ROW_CHUNKS, LANES), lambda b, s: (jnp.minimum(s, ns - 1), b, 0)),
            pl.BlockSpec((1, CB_TM, D_MODEL), lambda b, s: (b, jnp.maximum(s - ns, 0), 0)),
            pl.BlockSpec((1, D_MODEL), lambda b, s: (0, 0)),
        ],
        out_specs=pl.BlockSpec((1, CB_TM, D_MODEL), lambda b, s: (b, jnp.maximum(s - ns, 0), 0)),
        out_shape=jax.ShapeDtypeStruct((B, S, D_MODEL), F32),
        scratch_shapes=[pltpu.VMEM((SR, LANES), F32)],
        compiler_params=_cparams(("arbitrary", "arbitrary")),
        name="combine",
    )(idx.reshape(B * ns, 1, G * C), aff.reshape(B * ns, 1, G * S), ye, x2, g_f)


def kernel(x, norm1_g, w_in, lam_q1, lam_k1, lam_q2, lam_k2, subln_g, w_out, rel_bias,
           norm2_g, w_router, w_gate, w_up, w_down, norm_f_g):
    B, S, _ = x.shape
    C = EC_K * S // N_EXPERTS
    e_bias = _diff_bias(rel_bias)
    m_bias = _dil_bias(rel_bias)
    p, p4, p16 = _in_proj(x, norm1_g, w_in[0].astype(BF16))
    oa = _diff_attn(p, e_bias, lam_q1, lam_k1, lam_q2, lam_k2, subln_g)
    od = _dil_attn(p, p4, p16, m_bias)
    x2, h2, aff = _out_proj(oa, od, x, w_out[0].astype(BF16), norm2_g, w_router[0].T)
    idx = _route(aff)
    xe = _gather(idx, h2)
    ye = _ffn(xe, w_gate[0], w_up[0], w_down[0])
    return _combine(idx, aff, ye, x2, norm_f_g.reshape(1, D_MODEL))
```

```python
import functools
import math

import jax
import jax.numpy as jnp
from jax import lax
from jax.experimental import pallas as pl
from jax.experimental.pallas import tpu as pltpu

F32 = jnp.float32
BF16 = jnp.bfloat16
I32 = jnp.int32

D_MODEL = 1024
HEAD_DIM = 64
N_DIFF_HEADS = 4
N_DIL_HEADS = 8
N_DIL_PAIRS = N_DIL_HEADS // 2
DIFF_WIDTH = 512
DIL_WIDTH = 512
N_IN = 3 * DIFF_WIDTH + 3 * DIL_WIDTH
DIL_PATTERNS = ((128, 1), (512, 4), (2048, 16))
N_BUCKETS = 32
REL_MAX_DIST = 1024
N_EXPERTS = 16
EC_K = 2
D_FF = 2816
NORM_EPS = 1e-6
SUBLN_EPS = 1e-5
NEG = -1e30
LAMBDA_INIT = 0.8 - 0.6 * math.exp(0.0)

LANES = 128
SUBLANES = 8
ROW_CHUNKS = D_MODEL // LANES

DA_TQ = 256
DA_TK = 512
DA_GROUP = 4
DA_EST = 128
DA_DMIN = -1280
DA_DMAX = 1024
DA_WTOT = DA_DMAX - DA_DMIN + DA_TK

DL_BQ = 128
DL_HALF = 64
DL_WIN = DL_BQ + 2 * DL_HALF
DL_GROUP = 16

LOG2E = math.log2(math.e)

VMEM_LIMIT = 56 * 1024 * 1024


def _cparams(sem):
    return pltpu.CompilerParams(dimension_semantics=sem, vmem_limit_bytes=VMEM_LIMIT)


def _t5_bucket(rel):
    half = N_BUCKETS // 2
    max_exact = half // 2
    n = jnp.abs(rel)
    nf = jnp.maximum(n, 1).astype(F32)
    large = max_exact + (jnp.log(nf / max_exact) / math.log(REL_MAX_DIST / max_exact)
                         * (half - max_exact)).astype(I32)
    large = jnp.minimum(large, half - 1)
    return jnp.where(rel > 0, half, 0) + jnp.where(n < max_exact, n, large)


def _bias_lookup(rel, tab_ref, col):
    bucket = _t5_bucket(rel)
    acc = jnp.zeros(rel.shape, F32)
    for j in range(N_BUCKETS):
        acc = jnp.where(bucket == j, tab_ref[j, col], acc)
    return acc


def _diff_bias_body(tab_ref, e_ref):
    h = pl.program_id(0)

    def chunk(cc, _):
        c0 = pl.multiple_of(cc * LANES, LANES)
        i = lax.broadcasted_iota(I32, (DA_TQ, LANES), 0)
        c = lax.broadcasted_iota(I32, (DA_TQ, LANES), 1) + c0
        e_ref[0, :, pl.ds(c0, LANES)] = _bias_lookup(c - i + DA_DMIN, tab_ref, h) * LOG2E
        return 0

    lax.fori_loop(0, DA_WTOT // LANES, chunk, 0)


def _diff_bias(rel_bias):
    return pl.pallas_call(
        _diff_bias_body,
        grid=(N_DIFF_HEADS,),
        in_specs=[pl.BlockSpec(memory_space=pltpu.SMEM)],
        out_specs=pl.BlockSpec((1, DA_TQ, DA_WTOT), lambda h: (h, 0, 0)),
        out_shape=jax.ShapeDtypeStruct((N_DIFF_HEADS, DA_TQ, DA_WTOT), F32),
        compiler_params=_cparams(("arbitrary",)),
        name="diff_bias",
    )(rel_bias)


def _dil_bias_body(tab_ref, m_ref):
    p = pl.program_id(0)
    h = pl.program_id(1)
    dil = jnp.where(p == 0, DIL_PATTERNS[0][1], jnp.where(p == 1, DIL_PATTERNS[1][1], DIL_PATTERNS[2][1]))
    i = lax.broadcasted_iota(I32, (DL_BQ, DL_WIN), 0)
    c = lax.broadcasted_iota(I32, (DL_BQ, DL_WIN), 1)
    off = c - DL_HALF - i
    bias = _bias_lookup(off * dil, tab_ref, N_DIFF_HEADS + h) * LOG2E
    band = jnp.abs(off) <= DL_HALF
    m_ref[0, 0, 0] = jnp.where(band & (c >= DL_HALF), bias, NEG)
    m_ref[0, 0, 1] = jnp.where(band, bias, NEG)
    m_ref[0, 0, 2] = jnp.where(band & (c < DL_HALF + DL_BQ), bias, NEG)


def _dil_bias(rel_bias):
    n_pat = len(DIL_PATTERNS)
    return pl.pallas_call(
        _dil_bias_body,
        grid=(n_pat, N_DIL_HEADS),
        in_specs=[pl.BlockSpec(memory_space=pltpu.SMEM)],
        out_specs=pl.BlockSpec((1, 1, 3, DL_BQ, DL_WIN), lambda p, h: (p, h, 0, 0, 0)),
        out_shape=jax.ShapeDtypeStruct((n_pat, N_DIL_HEADS, 3, DL_BQ, DL_WIN), F32),
        compiler_params=_cparams(("arbitrary", "arbitrary")),
        name="dil_bias",
    )(rel_bias)


IP_TM = 512
IP_TN = 256
N_COLBLK = N_IN // LANES
N_DIL_COLBLK = 3 * DIL_WIDTH // LANES


def _in_proj_body(x_ref, g_ref, w_ref, p_ref, p4_ref, p16_ref, scr_ref):
    q_scale = HEAD_DIM ** -0.5 * LOG2E
    x = x_ref[0]
    ms = jnp.mean(x * x, axis=-1, keepdims=True)
    y = (x * lax.rsqrt(ms + NORM_EPS) * g_ref[...]).astype(BF16)
    for n in range(N_IN // IP_TN):
        c0 = n * IP_TN
        r = jnp.dot(y, w_ref[:, c0:c0 + IP_TN], preferred_element_type=F32)
        is_q = (c0 < DIFF_WIDTH) or (3 * DIFF_WIDTH <= c0 < 3 * DIFF_WIDTH + DIL_WIDTH)
        if is_q:
            r = r * q_scale
        for k in range(IP_TN // LANES):
            p_ref[0, 2 * n + k] = r[:, k * LANES:(k + 1) * LANES].astype(BF16)
        if c0 >= 3 * DIFF_WIDTH:
            for k in range(IP_TN // LANES):
                scr_ref[(c0 - 3 * DIFF_WIDTH) // LANES + k] = r[:, k * LANES:(k + 1) * LANES]
    for c in range(N_DIL_COLBLK):
        for d, dst_ref in ((4, p4_ref), (16, p16_ref)):
            for r_ in range(d):
                dst_ref[0, c, r_] = scr_ref[c, pl.ds(r_, IP_TM // d, stride=d), :].astype(BF16)


def _in_proj(x, g, w_bf16):
    B, S, _ = x.shape
    nt = S // IP_TM
    return pl.pallas_call(
        _in_proj_body,
        grid=(B, nt),
        in_specs=[
            pl.BlockSpec((1, IP_TM, D_MODEL), lambda b, t: (b, t, 0)),
            pl.BlockSpec((1, D_MODEL), lambda b, t: (0, 0)),
            pl.BlockSpec((D_MODEL, N_IN), lambda b, t: (0, 0)),
        ],
        out_specs=[
            pl.BlockSpec((1, N_COLBLK, IP_TM, LANES), lambda b, t: (b, 0, t, 0)),
            pl.BlockSpec((1, N_DIL_COLBLK, 4, IP_TM // 4, LANES), lambda b, t: (b, 0, 0, t, 0)),
            pl.BlockSpec((1, N_DIL_COLBLK, 16, IP_TM // 16, LANES), lambda b, t: (b, 0, 0, t, 0)),
        ],
        out_shape=[
            jax.ShapeDtypeStruct((B, N_COLBLK, S, LANES), BF16),
            jax.ShapeDtypeStruct((B, N_DIL_COLBLK, 4, S // 4, LANES), BF16),
            jax.ShapeDtypeStruct((B, N_DIL_COLBLK, 16, S // 16, LANES), BF16),
        ],
        scratch_shapes=[pltpu.VMEM((N_DIL_COLBLK, IP_TM, LANES), F32)],
        compiler_params=_cparams(("arbitrary", "arbitrary")),
        name="in_proj",
    )(x, g, w_bf16)


_NT_DIMS = (((1,), (1,)), ((), ()))


def _diff_attn_body(q_ref, k_ref, v_ref, e_ref, lq1_ref, lk1_ref, lq2_ref, lk2_ref, sg_ref, o_ref,
                    kx_ref, vx_ref, p_ref):
    S = q_ref.shape[2]
    TQ, TK = DA_TQ, DA_TK
    lam = (jnp.exp(jnp.sum(lq1_ref[...] * lk1_ref[...], axis=-1, keepdims=True))
           - jnp.exp(jnp.sum(lq2_ref[...] * lk2_ref[...], axis=-1, keepdims=True)) + LAMBDA_INIT)
    lane_q = lax.broadcasted_iota(I32, (TQ, LANES), 1)
    lo = lane_q < HEAD_DIM
    lane2 = lax.broadcasted_iota(I32, (2 * TQ, LANES), 1)

    def fill(i, _):
        r0 = pl.multiple_of(i * 512, 512)
        e0 = jnp.where(lax.broadcasted_iota(I32, (512, LANES), 1) == 0, 1.0, 0.0).astype(BF16)
        kx_ref[pl.ds(r0, 512), 0:LANES] = k_ref[0, 0, pl.ds(r0, 512), :]
        kx_ref[pl.ds(r0, 512), LANES:2 * LANES] = e0
        vx_ref[pl.ds(r0, 512), 0:LANES] = v_ref[0, 0, pl.ds(r0, 512), :]
        vx_ref[pl.ds(r0, 512), LANES:2 * LANES] = e0
        return 0

    lax.fori_loop(0, S // 512, fill, 0)

    def q_tile(q0, exact_shift, slot=0):
        ps_ref = p_ref.at[slot]
        q = q_ref[0, 0, pl.ds(q0, TQ), :]
        zero = jnp.zeros_like(q)
        qa = jnp.concatenate([jnp.where(lo, q, zero), jnp.where(lo, zero, q)], axis=0)

        def bias_tile(k0):
            off = pl.multiple_of(jnp.clip(k0 - q0, DA_DMIN, DA_DMAX) - DA_DMIN, LANES)
            return e_ref[0, :, pl.ds(off, TK)]

        def tile_max(k0, pm, width=TK):
            s = lax.dot_general(qa, k_ref[0, 0, pl.ds(k0, width), :], _NT_DIMS, preferred_element_type=F32)
            bias = bias_tile(k0)[:, 0:width]
            out = []
            for half in range(2):
                sb = s[half * TQ:(half + 1) * TQ] + bias
                t = pm[half]
                for j in range(width // LANES):
                    t = jnp.maximum(t, sb[:, j * LANES:(j + 1) * LANES])
                out.append(t)
            return tuple(out)

        neg = jnp.full((TQ, LANES), -jnp.inf, F32)
        if exact_shift:
            pm = lax.fori_loop(0, S // TK, lambda ki, pm: tile_max(pl.multiple_of(ki * TK, TK), pm), (neg, neg))
        else:
            pm = tile_max(0, (neg, neg), width=DA_EST)
        m = jnp.max(jnp.concatenate(pm, axis=0), axis=-1, keepdims=True)
        mcol = jnp.where(lane2 == 0, -m, 0.0).astype(BF16)
        qx = jnp.concatenate([qa, mcol], axis=1)

        for ki in range(S // TK):
            k0 = ki * TK
            s = lax.dot_general(qx, kx_ref[k0:k0 + TK, :], _NT_DIMS, preferred_element_type=F32)
            bias = bias_tile(k0)
            for half in range(2):
                pr = jnp.exp2(s[half * TQ:(half + 1) * TQ] + bias).astype(BF16)
                ps_ref[half * TQ:(half + 1) * TQ, k0:k0 + TK] = pr

        acc = jnp.dot(ps_ref[...], vx_ref[...], preferred_element_type=F32)
        o1 = acc[0:TQ, 0:LANES] / acc[0:TQ, LANES:LANES + 1]
        o2 = acc[TQ:2 * TQ, 0:LANES] / acc[TQ:2 * TQ, LANES:LANES + 1]
        o = o1 - lam * o2
        ms = jnp.mean(o * o, axis=-1, keepdims=True)
        o = o * lax.rsqrt(ms + SUBLN_EPS) * sg_ref[...] * (1.0 - LAMBDA_INIT)
        o_ref[0, 0, pl.ds(q0, TQ), :] = o.astype(BF16)
        return jnp.where(jnp.isfinite(o), 0.0, 1.0)

    def fast_loop(qi, bad):
        for u in range(DA_GROUP):
            q0 = pl.multiple_of((qi * DA_GROUP + u) * TQ, TQ)
            bad = jnp.maximum(bad, q_tile(q0, exact_shift=False, slot=u))
        return bad

    bad = lax.fori_loop(0, S // (TQ * DA_GROUP), fast_loop, jnp.zeros((TQ, LANES), F32))

    @pl.when(jnp.max(bad) > 0.0)
    def _():
        def exact_loop(qi, _):
            q_tile(pl.multiple_of(qi * TQ, TQ), exact_shift=True)
            return 0
        lax.fori_loop(0, S // TQ, exact_loop, 0)


def _diff_attn(p, e_bias, lq1, lk1, lq2, lk2, subln_g):
    B, _, S, _ = p.shape
    H = N_DIFF_HEADS
    vec = lambda n: pl.BlockSpec((1, n), lambda b, h: (0, 0))
    return pl.pallas_call(
        _diff_attn_body,
        grid=(B, H),
        in_specs=[
            pl.BlockSpec((1, 1, S, LANES), lambda b, h: (b, h, 0, 0)),
            pl.BlockSpec((1, 1, S, LANES), lambda b, h: (b, H + h, 0, 0)),
            pl.BlockSpec((1, 1, S, LANES), lambda b, h: (b, 2 * H + h, 0, 0)),
            pl.BlockSpec((1, DA_TQ, DA_WTOT), lambda b, h: (h, 0, 0)),
            vec(HEAD_DIM), vec(HEAD_DIM), vec(HEAD_DIM), vec(HEAD_DIM), vec(2 * HEAD_DIM),
        ],
        out_specs=pl.BlockSpec((1, 1, S, LANES), lambda b, h: (b, h, 0, 0)),
        out_shape=jax.ShapeDtypeStruct((B, H, S, LANES), BF16),
        scratch_shapes=[
            pltpu.VMEM((S, 2 * LANES), BF16),
            pltpu.VMEM((S, 2 * LANES), BF16),
            pltpu.VMEM((DA_GROUP, 2 * DA_TQ, S), BF16),
        ],
        compiler_params=_cparams(("arbitrary", "arbitrary")),
        name="diff_attn",
    )(p, p, p, e_bias, lq1, lk1, lq2, lk2, subln_g)


def _dil_attn_body(q1_ref, k1_ref, v1_ref, q4_ref, k4_ref, v4_ref, q16_ref, k16_ref, v16_ref, m_ref,
                   o_ref, kpad_ref, vlo_ref, vhi_ref, oacc_ref, lse_ref):
    S = q1_ref.shape[2]
    lane = lax.broadcasted_iota(I32, (DL_BQ, LANES), 1)
    lo = lane < HEAD_DIM
    zpad = jnp.zeros((DL_HALF, LANES), BF16)

    def ones_fill(i, _):
        r0 = pl.multiple_of(i * 512, 512)
        lo_f = lax.broadcasted_iota(I32, (512, LANES), 1) < HEAD_DIM
        vlo_ref[pl.ds(r0, 512), LANES:2 * LANES] = jnp.where(lo_f, 1.0, 0.0).astype(BF16)
        vhi_ref[pl.ds(r0, 512), LANES:2 * LANES] = jnp.where(lo_f, 0.0, 1.0).astype(BF16)
        return 0

    lax.fori_loop(0, vlo_ref.shape[0] // 512, ones_fill, 0)

    def run_pattern(p, dil, get_q, get_k, get_v):
        L = S // dil
        nblk = L // DL_BQ
        LP = L + 2 * DL_HALF

        def fill(r, _):
            base = pl.multiple_of(r * LP, DL_HALF)
            for off in (0, DL_HALF + L):
                kpad_ref[pl.ds(base + off, DL_HALF), :] = zpad
                vlo_ref[pl.ds(base + off, DL_HALF), 0:LANES] = zpad
                vhi_ref[pl.ds(base + off, DL_HALF), 0:LANES] = zpad
            kpad_ref[pl.ds(base + DL_HALF, L), :] = get_k(r)[...]
            v = get_v(r)[...]
            lo_v = lax.broadcasted_iota(I32, v.shape, 1) < HEAD_DIM
            zero = jnp.zeros_like(v)
            vlo_ref[pl.ds(base + DL_HALF, L), 0:LANES] = jnp.where(lo_v, v, zero)
            vhi_ref[pl.ds(base + DL_HALF, L), 0:LANES] = jnp.where(lo_v, zero, v)
            return 0

        lax.fori_loop(0, dil, fill, 0)

        def group(gi, _):
            for u in range(DL_GROUP):
                n = gi * DL_GROUP + u
                r = n // nblk
                blk = n % nblk
                q0 = pl.multiple_of(blk * DL_BQ, DL_BQ)
                w0 = pl.multiple_of(r * LP + q0, DL_HALF)
                q = get_q(r)[pl.ds(q0, DL_BQ), :]
                kw = kpad_ref[pl.ds(w0, DL_WIN), :]
                vw = jnp.concatenate([vlo_ref[pl.ds(w0, DL_WIN), :], vhi_ref[pl.ds(w0, DL_WIN), :]], axis=0)
                edge = jnp.where(blk == 0, 0, jnp.where(blk == nblk - 1, 2, 1))
                zero = jnp.zeros_like(q)
                probs, maxes = [], []
                for hh in range(2):
                    qm = jnp.where(lo, q, zero) if hh == 0 else jnp.where(lo, zero, q)
                    s = lax.dot_general(qm, kw, _NT_DIMS, preferred_element_type=F32) + m_ref[p, hh, edge]
                    m = jnp.max(s, axis=-1, keepdims=True)
                    probs.append(jnp.exp2(s - m).astype(BF16))
                    maxes.append(m)
                both = jnp.dot(jnp.concatenate(probs, axis=1), vw, preferred_element_type=F32)
                l = both[:, LANES:2 * LANES]
                o = both[:, 0:LANES] / l
                lse = jnp.where(lo, maxes[0], maxes[1]) + jnp.log2(l)
                if dil == 1:
                    oacc_ref[p, pl.ds(q0, DL_BQ), :] = o
                    lse_ref[p, pl.ds(q0, DL_BQ), :] = lse
                else:
                    oacc_ref[p, pl.ds(q0 * dil + r, DL_BQ, stride=dil), :] = o
                    lse_ref[p, pl.ds(q0 * dil + r, DL_BQ, stride=dil), :] = lse
            return 0

        lax.fori_loop(0, dil * nblk // DL_GROUP, group, 0)

    run_pattern(0, 1, lambda r: q1_ref.at[0, 0], lambda r: k1_ref.at[0, 0], lambda r: v1_ref.at[0, 0])
    run_pattern(1, 4, lambda r: q4_ref.at[0, 0, r], lambda r: k4_ref.at[0, 0, r], lambda r: v4_ref.at[0, 0, r])
    run_pattern(2, 16, lambda r: q16_ref.at[0, 0, r], lambda r: k16_ref.at[0, 0, r], lambda r: v16_ref.at[0, 0, r])

    MIX = 256

    def mix(i, _):
        r0 = pl.multiple_of(i * MIX, MIX)
        ls = [lse_ref[p, pl.ds(r0, MIX), :] for p in range(3)]
        mx = jnp.maximum(jnp.maximum(ls[0], ls[1]), ls[2])
        ws = [jnp.exp2(l - mx) for l in ls]
        den = ws[0] + ws[1] + ws[2]
        num = ws[0] * oacc_ref[0, pl.ds(r0, MIX), :]
        num += ws[1] * oacc_ref[1, pl.ds(r0, MIX), :]
        num += ws[2] * oacc_ref[2, pl.ds(r0, MIX), :]
        o_ref[0, 0, pl.ds(r0, MIX), :] = (num / den).astype(BF16)
        return 0

    lax.fori_loop(0, S // MIX, mix, 0)


def _dil_attn(p, p4, p16, m_bias):
    B, _, S, _ = p.shape
    NP = N_DIL_PAIRS
    base = 3 * N_DIFF_HEADS
    nat = lambda o: pl.BlockSpec((1, 1, S, LANES), lambda b, h: (b, base + o * NP + h, 0, 0))
    res = lambda o, d: pl.BlockSpec((1, 1, d, S // d, LANES), lambda b, h: (b, o * NP + h, 0, 0, 0))
    return pl.pallas_call(
        _dil_attn_body,
        grid=(B, NP),
        in_specs=[nat(0), nat(1), nat(2), res(0, 4), res(1, 4), res(2, 4), res(0, 16), res(1, 16), res(2, 16),
                  pl.BlockSpec((3, 2, 3, DL_BQ, DL_WIN), lambda b, h: (0, h, 0, 0, 0))],
        out_specs=pl.BlockSpec((1, 1, S, LANES), lambda b, h: (b, h, 0, 0)),
        out_shape=jax.ShapeDtypeStruct((B, NP, S, LANES), BF16),
        scratch_shapes=[
            pltpu.VMEM((S + 2 * DL_HALF * DIL_PATTERNS[-1][1], LANES), BF16),
            pltpu.VMEM((S + 2 * DL_HALF * DIL_PATTERNS[-1][1], 2 * LANES), BF16),
            pltpu.VMEM((S + 2 * DL_HALF * DIL_PATTERNS[-1][1], 2 * LANES), BF16),
            pltpu.VMEM((3, S, LANES), F32),
            pltpu.VMEM((3, S, LANES), F32),
        ],
        compiler_params=_cparams(("arbitrary", "arbitrary")),
        name="dil_attn",
    )(p, p, p, p4, p4, p4, p16, p16, p16, m_bias)


OP_TM = 512


def _store_token_major(ref, val):
    rows = val.shape[0]
    for j in range(ROW_CHUNKS):
        ref[0, pl.ds(j, rows, stride=ROW_CHUNKS), :] = val[:, j * LANES:(j + 1) * LANES]


def _out_proj_body(oa_ref, od_ref, x_ref, w_ref, g_ref, wr_ref, x2_ref, h2_ref, aff_ref):
    pieces = [oa_ref[0, i] for i in range(N_DIFF_HEADS)] + [od_ref[0, i] for i in range(N_DIL_PAIRS)]
    a = jnp.concatenate(pieces, axis=-1)
    x2 = x_ref[0] + jnp.dot(a, w_ref[...], preferred_element_type=F32)
    ms = jnp.mean(x2 * x2, axis=-1, keepdims=True)
    h = x2 * lax.rsqrt(ms + NORM_EPS) * g_ref[...]
    h_hi = h.astype(BF16)
    h_lo = (h - h_hi.astype(F32)).astype(BF16)
    wr = wr_ref[...]
    w_hi = wr.astype(BF16)
    w_lo = (wr - w_hi.astype(F32)).astype(BF16)
    logits = (lax.dot_general(w_hi, h_hi, _NT_DIMS, preferred_element_type=F32)
              + lax.dot_general(w_hi, h_lo, _NT_DIMS, preferred_element_type=F32)
              + lax.dot_general(w_lo, h_hi, _NT_DIMS, preferred_element_type=F32))
    mx = jnp.max(logits, axis=0, keepdims=True)
    ex = jnp.exp(logits - mx)
    aff_ref[0] = ex / jnp.sum(ex, axis=0, keepdims=True)
    x2_ref[0] = x2
    _store_token_major(h2_ref, h)


def _out_proj(oa, od, x, w_bf16, g, wr_t):
    B, S, _ = x.shape
    nt = S // OP_TM
    attn = lambda n: pl.BlockSpec((1, n, OP_TM, LANES), lambda b, t: (b, 0, t, 0))
    tok_major = pl.BlockSpec((1, OP_TM * ROW_CHUNKS, LANES), lambda b, t: (b, t, 0))
    return pl.pallas_call(
        _out_proj_body,
        grid=(B, nt),
        in_specs=[
            attn(N_DIFF_HEADS), attn(N_DIL_PAIRS),
            pl.BlockSpec((1, OP_TM, D_MODEL), lambda b, t: (b, t, 0)),
            pl.BlockSpec((D_MODEL, D_MODEL), lambda b, t: (0, 0)),
            pl.BlockSpec((1, D_MODEL), lambda b, t: (0, 0)),
            pl.BlockSpec((N_EXPERTS, D_MODEL), lambda b, t: (0, 0)),
        ],
        out_specs=[pl.BlockSpec((1, OP_TM, D_MODEL), lambda b, t: (b, t, 0)), tok_major,
                   pl.BlockSpec((1, N_EXPERTS, OP_TM), lambda b, t: (b, 0, t))],
        out_shape=[
            jax.ShapeDtypeStruct((B, S, D_MODEL), F32),
            jax.ShapeDtypeStruct((B, S * ROW_CHUNKS, LANES), F32),
            jax.ShapeDtypeStruct((B, N_EXPERTS, S), F32),
        ],
        compiler_params=_cparams(("arbitrary", "arbitrary")),
        name="out_proj",
    )(oa, od, x, w_bf16, g, wr_t)


RT_GROUP = 4


def _route_body(aff_ref, idx_ref, xh_ref, xl_ref, st_ref, en_ref):
    E, S = aff_ref.shape[1], aff_ref.shape[2]
    C = idx_ref.shape[2]
    n_chunks = S // LANES
    a = aff_ref[0]
    bits = pltpu.bitcast(a, I32)
    tau = jnp.zeros((E, 1), I32)
    for bit in range(30, -1, -1):
        cand = tau | (1 << bit)
        cnt = jnp.sum(jnp.where(bits >= cand, 1.0, 0.0).astype(F32), axis=1, keepdims=True)
        tau = jnp.where(cnt >= C, cand, tau)
    gt = bits > tau
    eq = bits == tau
    need = C - jnp.sum(jnp.where(gt, 1.0, 0.0).astype(F32), axis=1, keepdims=True)

    tri = (lax.broadcasted_iota(I32, (LANES, LANES), 0) <= lax.broadcasted_iota(I32, (LANES, LANES), 1)).astype(BF16)

    def chunk_cumsum(mj, carry):
        return jnp.dot(mj.astype(BF16), tri, preferred_element_type=F32) + carry

    sel_chunks = []
    carry = jnp.zeros((E, 1), F32)
    for j in range(n_chunks):
        cs = slice(j * LANES, (j + 1) * LANES)
        eq_j = jnp.where(eq[:, cs], 1.0, 0.0).astype(F32)
        inc = chunk_cumsum(eq_j, carry)
        carry = inc[:, LANES - 1:LANES]
        take_tie = jnp.where(inc - eq_j < need, eq_j, 0.0)
        sel_chunks.append(jnp.where(gt[:, cs], 1.0, take_tie))

    lane_e = lax.broadcasted_iota(I32, (E, LANES), 1)
    st = jnp.full((E, LANES), float(C), F32)
    en = jnp.full((E, LANES), float(C), F32)
    carry = jnp.zeros((E, 1), F32)
    for j in range(n_chunks):
        sel_j = sel_chunks[j]
        inc = chunk_cumsum(sel_j, carry)
        posm = jnp.where(sel_j > 0.0, inc - sel_j, -1.0)
        hi = jnp.floor(posm * 0.5)
        xh_ref[pl.ds(j * E, E), :] = hi
        xl_ref[pl.ds(j * E, E), :] = posm - 2.0 * hi
        st = jnp.where(lane_e == j, carry, st)
        carry = inc[:, LANES - 1:LANES]
        en = jnp.where(lane_e == j, carry, en)
    pad_rows = (LANES - n_chunks) * E
    xh_ref[pl.ds(n_chunks * E, pad_rows), :] = jnp.zeros((pad_rows, LANES), F32)
    xl_ref[pl.ds(n_chunks * E, pad_rows), :] = jnp.zeros((pad_rows, LANES), F32)
    st_ref[...] = st
    en_ref[...] = en

    c_iota = lax.broadcasted_iota(I32, (C, LANES), 0).astype(F32)
    lane_row = jnp.where(lax.broadcasted_iota(I32, (SUBLANES, LANES), 0) == 0,
                         lax.broadcasted_iota(I32, (SUBLANES, LANES), 1), 0).astype(F32).astype(BF16)

    def expert_group(g, _):
        for u in range(RT_GROUP):
            e = g * RT_GROUP + u
            st_e = st_ref[pl.ds(e, 1), :]
            en_e = en_ref[pl.ds(e, 1), :]
            in_chunk = jnp.where(st_e <= c_iota, jnp.where(c_iota < en_e, 1.0, 0.0), 0.0).astype(BF16)
            xh = xh_ref[pl.ds(e, LANES, stride=E), :].astype(BF16)
            xl = xl_ref[pl.ds(e, LANES, stride=E), :].astype(BF16)
            hl = jnp.dot(in_chunk, jnp.concatenate([xh, xl], axis=1), preferred_element_type=F32)
            slot_of = 2.0 * hl[:, 0:LANES] + hl[:, LANES:2 * LANES]
            hit = jnp.where(slot_of == c_iota, 1.0, 0.0).astype(BF16)
            chunk_id = lax.dot_general(lane_row, in_chunk, _NT_DIMS, preferred_element_type=F32)
            lane_id = lax.dot_general(lane_row, hit, _NT_DIMS, preferred_element_type=F32)
            idx_ref[0, pl.ds(e, 1), :] = (chunk_id[0:1, :] * float(LANES) + lane_id[0:1, :] + 0.5).astype(I32)
        return 0

    lax.fori_loop(0, E // RT_GROUP, expert_group, 0)


def _route(aff):
    B, E, S = aff.shape
    C = EC_K * S // E
    return pl.pallas_call(
        _route_body,
        grid=(B,),
        in_specs=[pl.BlockSpec((1, E, S), lambda b: (b, 0, 0))],
        out_specs=pl.BlockSpec((1, E, C), lambda b: (b, 0, 0)),
        out_shape=jax.ShapeDtypeStruct((B, E, C), I32),
        scratch_shapes=[pltpu.VMEM((LANES * E, LANES), F32), pltpu.VMEM((LANES * E, LANES), F32),
                        pltpu.VMEM((E, LANES), F32), pltpu.VMEM((E, LANES), F32)],
        compiler_params=_cparams(("arbitrary",)),
        name="route",
    )(aff)


GA_UNROLL = 16
GA_EXPERTS = 1


def _gather_body(idx_ref, h_ref, xe_ref, g_ref):
    C = idx_ref.shape[2] // GA_EXPERTS
    pitch = g_ref.shape[0] // ROW_CHUNKS

    for ex in range(GA_EXPERTS):
        def rows(i, _):
            for u in range(GA_UNROLL):
                r = i * GA_UNROLL + u
                t = idx_ref[0, 0, ex * C + r]
                src = pl.multiple_of(t * ROW_CHUNKS, ROW_CHUNKS)
                g_ref[pl.ds(r, ROW_CHUNKS, stride=pitch), :] = h_ref[0, pl.ds(src, ROW_CHUNKS), :]
            return 0

        lax.fori_loop(0, C // GA_UNROLL, rows, 0)
        for j in range(ROW_CHUNKS):
            xe_ref[ex, :, j * LANES:(j + 1) * LANES] = g_ref[j * pitch:j * pitch + C, :].astype(BF16)


def _gather(idx, h2):
    B, E, C = idx.shape
    SR = h2.shape[1]
    G = GA_EXPERTS
    return pl.pallas_call(
        _gather_body,
        grid=(B, E // G),
        in_specs=[
            pl.BlockSpec((1, 1, G * C), lambda b, e: (b * (E // G) + e, 0, 0), memory_space=pltpu.SMEM),
            pl.BlockSpec((1, SR, LANES), lambda b, e: (b, 0, 0), pipeline_mode=pl.Buffered(1)),
        ],
        out_specs=pl.BlockSpec((G, C, D_MODEL), lambda b, e: (e, b, 0)),
        out_shape=jax.ShapeDtypeStruct((E, B * C, D_MODEL), BF16),
        scratch_shapes=[pltpu.VMEM(((C + SUBLANES) * ROW_CHUNKS, LANES), F32)],
        compiler_params=_cparams(("arbitrary", "arbitrary")),
        name="gather",
    )(idx.reshape(B * E // G, 1, G * C), h2)


FF_TM = 1024
FF_TF = 1408
FF_CAST_ROWS = 128


def _cast_weight(src_ref, dst_ref):
    rows = dst_ref.shape[0]

    def chunk(i, _):
        r0 = pl.multiple_of(i * FF_CAST_ROWS, FF_CAST_ROWS)
        dst_ref[pl.ds(r0, FF_CAST_ROWS), :] = src_ref[0, pl.ds(r0, FF_CAST_ROWS), :].astype(BF16)
        return 0

    lax.fori_loop(0, rows // FF_CAST_ROWS, chunk, 0)


def _ffn_up_body(x_ref, wg_ref, wu_ref, he_ref, wgb_ref, wub_ref):
    @pl.when(pl.program_id(2) == 0)
    def _():
        _cast_weight(wg_ref, wgb_ref)
        _cast_weight(wu_ref, wub_ref)

    x = x_ref[0]
    hg = jnp.dot(x, wgb_ref[...], preferred_element_type=F32)
    hu = jnp.dot(x, wub_ref[...], preferred_element_type=F32)
    he_ref[0] = (hg * jax.nn.sigmoid(hg) * hu).astype(BF16)


def _ffn_down_body(he_ref, wd_ref, y_ref, wdb_ref):
    @pl.when(pl.program_id(1) == 0)
    def _():
        _cast_weight(wd_ref, wdb_ref)

    _store_token_major(y_ref, jnp.dot(he_ref[0], wdb_ref[...], preferred_element_type=F32))


def _ffn(xe, wg, wu, wd):
    E, M, _ = xe.shape
    assert M % FF_TM == 0, (M, FF_TM)
    nm = M // FF_TM
    nf = D_FF // FF_TF
    he = pl.pallas_call(
        _ffn_up_body,
        grid=(E, nf, nm),
        in_specs=[
            pl.BlockSpec((1, FF_TM, D_MODEL), lambda e, f, m: (e, m, 0)),
            pl.BlockSpec((1, D_MODEL, FF_TF), lambda e, f, m: (e, 0, f)),
            pl.BlockSpec((1, D_MODEL, FF_TF), lambda e, f, m: (e, 0, f)),
        ],
        out_specs=pl.BlockSpec((1, FF_TM, FF_TF), lambda e, f, m: (e, m, f)),
        out_shape=jax.ShapeDtypeStruct((E, M, D_FF), BF16),
        scratch_shapes=[pltpu.VMEM((D_MODEL, FF_TF), BF16), pltpu.VMEM((D_MODEL, FF_TF), BF16)],
        compiler_params=_cparams(("arbitrary", "arbitrary", "arbitrary")),
        name="expert_up",
    )(xe, wg, wu)
    return pl.pallas_call(
        _ffn_down_body,
        grid=(E, nm),
        in_specs=[
            pl.BlockSpec((1, FF_TM, D_FF), lambda e, m: (e, m, 0)),
            pl.BlockSpec((1, D_FF, D_MODEL), lambda e, m: (e, 0, 0)),
        ],
        out_specs=pl.BlockSpec((1, FF_TM * ROW_CHUNKS, LANES), lambda e, m: (e, m, 0)),
        out_shape=jax.ShapeDtypeStruct((E, M * ROW_CHUNKS, LANES), F32),
        scratch_shapes=[pltpu.VMEM((D_FF, D_MODEL), BF16)],
        compiler_params=_cparams(("arbitrary", "arbitrary")),
        name="expert_down",
    )(he, wd)


CB_UNROLL = 8
CB_TM = 1024
CB_EXPERTS = 4


def _combine_body(idx_ref, aff_ref, y_ref, x2_ref, g_ref, o_ref, acc_ref):
    E = N_EXPERTS // CB_EXPERTS
    C = idx_ref.shape[2] // CB_EXPERTS
    S = aff_ref.shape[2] // CB_EXPERTS
    s = pl.program_id(1)

    @pl.when(s == 0)
    def _():
        def zero(i, _):
            r0 = pl.multiple_of(i * 1024, 1024)
            acc_ref[pl.ds(r0, 1024), :] = jnp.zeros((1024, LANES), F32)
            return 0
        lax.fori_loop(0, acc_ref.shape[0] // 1024, zero, 0)

    @pl.when(s < E)
    def _():
        for ex in range(CB_EXPERTS):
            def rows(i, _):
                dsts, sums = [], []
                for u in range(CB_UNROLL):
                    r = i * CB_UNROLL + u
                    t = idx_ref[0, 0, ex * C + r]
                    gate = aff_ref[0, 0, ex * S + t]
                    dst = pl.multiple_of(t * ROW_CHUNKS, ROW_CHUNKS)
                    src = pl.multiple_of(r * ROW_CHUNKS, ROW_CHUNKS)
                    dsts.append(dst)
                    sums.append(acc_ref[pl.ds(dst, ROW_CHUNKS), :] + gate * y_ref[ex, pl.ds(src, ROW_CHUNKS), :])
                for dst, v in zip(dsts, sums):
                    acc_ref[pl.ds(dst, ROW_CHUNKS), :] = v
                return 0
            lax.fori_loop(0, C // CB_UNROLL, rows, 0)

    @pl.when(s >= E)
    def _():
        r0 = pl.multiple_of((s - E) * (CB_TM * ROW_CHUNKS), CB_TM * ROW_CHUNKS)
        moe = jnp.concatenate([acc_ref[pl.ds(r0 + j, CB_TM, stride=ROW_CHUNKS), :] for j in range(ROW_CHUNKS)],
                              axis=1)
        v = x2_ref[0] + moe
        ms = jnp.mean(v * v, axis=-1, keepdims=True)
        o_ref[0] = v * lax.rsqrt(ms + NORM_EPS) * g_ref[...]


def _combine(idx, aff, ye, x2, g_f):
    B, E, C = idx.shape
    S = x2.shape[1]
    SR = S * ROW_CHUNKS
    nfin = S // CB_TM
    G = CB_EXPERTS
    ns = E // G
    return pl.pallas_call(
        _combine_body,
        grid=(B, ns + nfin),
        in_specs=[
            pl.BlockSpec((1, 1, G * C), lambda b, s: (b * ns + jnp.minimum(s, ns - 1), 0, 0), memory_space=pltpu.SMEM),
            pl.BlockSpec((1, 1, G * S), lambda b, s: (b * ns + jnp.minimum(s, ns - 1), 0, 0), memory_space=pltpu.SMEM),
            pl.BlockSpec((G, C * ROW_CHUNKS, LANES), lambda b, s: (jnp.minimum(s, ns - 1), b, 0)),
            pl.BlockSpec((1, CB_TM, D_MODEL), lambda b, s: (b, jnp.maximum(s - ns, 0), 0)),
            pl.BlockSpec((1, D_MODEL), lambda b, s: (0, 0)),
        ],
        out_specs=pl.BlockSpec((1, CB_TM, D_MODEL), lambda b, s: (b, jnp.maximum(s - ns, 0), 0)),
        out_shape=jax.ShapeDtypeStruct((B, S, D_MODEL), F32),
        scratch_shapes=[pltpu.VMEM((SR, LANES), F32)],
        compiler_params=_cparams(("arbitrary", "arbitrary")),
        name="combine",
    )(idx.reshape(B * ns, 1, G * C), aff.reshape(B * ns, 1, G * S), ye, x2, g_f)


def kernel(x, norm1_g, w_in, lam_q1, lam_k1, lam_q2, lam_k2, subln_g, w_out, rel_bias,
           norm2_g, w_router, w_gate, w_up, w_down, norm_f_g):
    B, S, _ = x.shape
    C = EC_K * S // N_EXPERTS
    e_bias = _diff_bias(rel_bias)
    m_bias = _dil_bias(rel_bias)
    p, p4, p16 = _in_proj(x, norm1_g, w_in[0].astype(BF16))
    oa = _diff_attn(p, e_bias, lam_q1, lam_k1, lam_q2, lam_k2, subln_g)
    od = _dil_attn(p, p4, p16, m_bias)
    x2, h2, aff = _out_proj(oa, od, x, w_out[0].astype(BF16), norm2_g, w_router[0].T)
    idx = _route(aff)
    xe = _gather(idx, h2)
    ye = _ffn(xe, w_gate[0], w_up[0], w_down[0])
    return _combine(idx, aff, ye, x2, norm_f_g.reshape(1, D_MODEL))
```

```python
import functools
import math

import jax
import jax.numpy as jnp
from jax import lax
from jax.experimental import pallas as pl
from jax.experimental.pallas import tpu as pltpu

F32 = jnp.float32
BF16 = jnp.bfloat16
I32 = jnp.int32

D_MODEL = 1024
HEAD_DIM = 64
N_DIFF_HEADS = 4
N_DIL_HEADS = 8
N_DIL_PAIRS = N_DIL_HEADS // 2
DIFF_WIDTH = 512
DIL_WIDTH = 512
N_IN = 3 * DIFF_WIDTH + 3 * DIL_WIDTH
DIL_PATTERNS = ((128, 1), (512, 4), (2048, 16))
N_BUCKETS = 32
REL_MAX_DIST = 1024
N_EXPERTS = 16
EC_K = 2
D_FF = 2816
NORM_EPS = 1e-6
SUBLN_EPS = 1e-5
NEG = -1e30
LAMBDA_INIT = 0.8 - 0.6 * math.exp(0.0)

LANES = 128
SUBLANES = 8
ROW_CHUNKS = D_MODEL // LANES

DA_TQ = 256
DA_TK = 512
DA_GROUP = 4
DA_EST = 128
DA_DMIN = -1280
DA_DMAX = 1024
DA_WTOT = DA_DMAX - DA_DMIN + DA_TK

DL_BQ = 128
DL_HALF = 64
DL_WIN = DL_BQ + 2 * DL_HALF
DL_GROUP = 16

LOG2E = math.log2(math.e)

VMEM_LIMIT = 56 * 1024 * 1024


def _cparams(sem):
    return pltpu.CompilerParams(dimension_semantics=sem, vmem_limit_bytes=VMEM_LIMIT)


def _t5_bucket(rel):
    half = N_BUCKETS // 2
    max_exact = half // 2
    n = jnp.abs(rel)
    nf = jnp.maximum(n, 1).astype(F32)
    large = max_exact + (jnp.log(nf / max_exact) / math.log(REL_MAX_DIST / max_exact)
                         * (half - max_exact)).astype(I32)
    large = jnp.minimum(large, half - 1)
    return jnp.where(rel > 0, half, 0) + jnp.where(n < max_exact, n, large)


def _bias_lookup(rel, tab_ref, col):
    bucket = _t5_bucket(rel)
    acc = jnp.zeros(rel.shape, F32)
    for j in range(N_BUCKETS):
        acc = jnp.where(bucket == j, tab_ref[j, col], acc)
    return acc


def _diff_bias_body(tab_ref, e_ref):
    h = pl.program_id(0)

    def chunk(cc, _):
        c0 = pl.multiple_of(cc * LANES, LANES)
        i = lax.broadcasted_iota(I32, (DA_TQ, LANES), 0)
        c = lax.broadcasted_iota(I32, (DA_TQ, LANES), 1) + c0
        e_ref[0, :, pl.ds(c0, LANES)] = _bias_lookup(c - i + DA_DMIN, tab_ref, h) * LOG2E
        return 0

    lax.fori_loop(0, DA_WTOT // LANES, chunk, 0)


def _diff_bias(rel_bias):
    return pl.pallas_call(
        _diff_bias_body,
        grid=(N_DIFF_HEADS,),
        in_specs=[pl.BlockSpec(memory_space=pltpu.SMEM)],
        out_specs=pl.BlockSpec((1, DA_TQ, DA_WTOT), lambda h: (h, 0, 0)),
        out_shape=jax.ShapeDtypeStruct((N_DIFF_HEADS, DA_TQ, DA_WTOT), F32),
        compiler_params=_cparams(("arbitrary",)),
        name="diff_bias",
    )(rel_bias)


def _dil_bias_body(tab_ref, m_ref):
    p = pl.program_id(0)
    h = pl.program_id(1)
    dil = jnp.where(p == 0, DIL_PATTERNS[0][1], jnp.where(p == 1, DIL_PATTERNS[1][1], DIL_PATTERNS[2][1]))
    i = lax.broadcasted_iota(I32, (DL_BQ, DL_WIN), 0)
    c = lax.broadcasted_iota(I32, (DL_BQ, DL_WIN), 1)
    off = c - DL_HALF - i
    bias = _bias_lookup(off * dil, tab_ref, N_DIFF_HEADS + h) * LOG2E
    band = jnp.abs(off) <= DL_HALF
    m_ref[0, 0, 0] = jnp.where(band & (c >= DL_HALF), bias, NEG)
    m_ref[0, 0, 1] = jnp.where(band, bias, NEG)
    m_ref[0, 0, 2] = jnp.where(band & (c < DL_HALF + DL_BQ), bias, NEG)


def _dil_bias(rel_bias):
    n_pat = len(DIL_PATTERNS)
    return pl.pallas_call(
        _dil_bias_body,
        grid=(n_pat, N_DIL_HEADS),
        in_specs=[pl.BlockSpec(memory_space=pltpu.SMEM)],
        out_specs=pl.BlockSpec((1, 1, 3, DL_BQ, DL_WIN), lambda p, h: (p, h, 0, 0, 0)),
        out_shape=jax.ShapeDtypeStruct((n_pat, N_DIL_HEADS, 3, DL_BQ, DL_WIN), F32),
        compiler_params=_cparams(("arbitrary", "arbitrary")),
        name="dil_bias",
    )(rel_bias)


IP_TM = 512
IP_TN = 256
N_COLBLK = N_IN // LANES
N_DIL_COLBLK = 3 * DIL_WIDTH // LANES


def _in_proj_body(x_ref, g_ref, w_ref, p_ref, p4_ref, p16_ref, scr_ref):
    q_scale = HEAD_DIM ** -0.5 * LOG2E
    x = x_ref[0]
    ms = jnp.mean(x * x, axis=-1, keepdims=True)
    y = (x * lax.rsqrt(ms + NORM_EPS) * g_ref[...]).astype(BF16)
    for n in range(N_IN // IP_TN):
        c0 = n * IP_TN
        r = jnp.dot(y, w_ref[:, c0:c0 + IP_TN], preferred_element_type=F32)
        is_q = (c0 < DIFF_WIDTH) or (3 * DIFF_WIDTH <= c0 < 3 * DIFF_WIDTH + DIL_WIDTH)
        if is_q:
            r = r * q_scale
        for k in range(IP_TN // LANES):
            p_ref[0, 2 * n + k] = r[:, k * LANES:(k + 1) * LANES].astype(BF16)
        if c0 >= 3 * DIFF_WIDTH:
            for k in range(IP_TN // LANES):
                scr_ref[(c0 - 3 * DIFF_WIDTH) // LANES + k] = r[:, k * LANES:(k + 1) * LANES]
    for c in range(N_DIL_COLBLK):
        for d, dst_ref in ((4, p4_ref), (16, p16_ref)):
            for r_ in range(d):
                dst_ref[0, c, r_] = scr_ref[c, pl.ds(r_, IP_TM // d, stride=d), :].astype(BF16)


def _in_proj(x, g, w_bf16):
    B, S, _ = x.shape
    nt = S // IP_TM
    return pl.pallas_call(
        _in_proj_body,
        grid=(B, nt),
        in_specs=[
            pl.BlockSpec((1, IP_TM, D_MODEL), lambda b, t: (b, t, 0)),
            pl.BlockSpec((1, D_MODEL), lambda b, t: (0, 0)),
            pl.BlockSpec((D_MODEL, N_IN), lambda b, t: (0, 0)),
        ],
        out_specs=[
            pl.BlockSpec((1, N_COLBLK, IP_TM, LANES), lambda b, t: (b, 0, t, 0)),
            pl.BlockSpec((1, N_DIL_COLBLK, 4, IP_TM // 4, LANES), lambda b, t: (b, 0, 0, t, 0)),
            pl.BlockSpec((1, N_DIL_COLBLK, 16, IP_TM // 16, LANES), lambda b, t: (b, 0, 0, t, 0)),
        ],
        out_shape=[
            jax.ShapeDtypeStruct((B, N_COLBLK, S, LANES), BF16),
            jax.ShapeDtypeStruct((B, N_DIL_COLBLK, 4, S // 4, LANES), BF16),
            jax.ShapeDtypeStruct((B, N_DIL_COLBLK, 16, S // 16, LANES), BF16),
        ],
        scratch_shapes=[pltpu.VMEM((N_DIL_COLBLK, IP_TM, LANES), F32)],
        compiler_params=_cparams(("arbitrary", "arbitrary")),
        name="in_proj",
    )(x, g, w_bf16)


_NT_DIMS = (((1,), (1,)), ((), ()))


def _diff_attn_body(q_ref, k_ref, v_ref, e_ref, lq1_ref, lk1_ref, lq2_ref, lk2_ref, sg_ref, o_ref,
                    kx_ref, vx_ref, p_ref):
    S = q_ref.shape[2]
    TQ, TK = DA_TQ, DA_TK
    lam = (jnp.exp(jnp.sum(lq1_ref[...] * lk1_ref[...], axis=-1, keepdims=True))
           - jnp.exp(jnp.sum(lq2_ref[...] * lk2_ref[...], axis=-1, keepdims=True)) + LAMBDA_INIT)
    lane_q = lax.broadcasted_iota(I32, (TQ, LANES), 1)
    lo = lane_q < HEAD_DIM
    lane2 = lax.broadcasted_iota(I32, (2 * TQ, LANES), 1)

    def fill(i, _):
        r0 = pl.multiple_of(i * 512, 512)
        e0 = jnp.where(lax.broadcasted_iota(I32, (512, LANES), 1) == 0, 1.0, 0.0).astype(BF16)
        kx_ref[pl.ds(r0, 512), 0:LANES] = k_ref[0, 0, pl.ds(r0, 512), :]
        kx_ref[pl.ds(r0, 512), LANES:2 * LANES] = e0
        vx_ref[pl.ds(r0, 512), 0:LANES] = v_ref[0, 0, pl.ds(r0, 512), :]
        vx_ref[pl.ds(r0, 512), LANES:2 * LANES] = e0
        return 0

    lax.fori_loop(0, S // 512, fill, 0)

    def q_tile(q0, exact_shift, slot=0):
        ps_ref = p_ref.at[slot]
        q = q_ref[0, 0, pl.ds(q0, TQ), :]
        zero = jnp.zeros_like(q)
        qa = jnp.concatenate([jnp.where(lo, q, zero), jnp.where(lo, zero, q)], axis=0)

        def bias_tile(k0):
            off = pl.multiple_of(jnp.clip(k0 - q0, DA_DMIN, DA_DMAX) - DA_DMIN, LANES)
            return e_ref[0, :, pl.ds(off, TK)]

        def tile_max(k0, pm, width=TK):
            s = lax.dot_general(qa, k_ref[0, 0, pl.ds(k0, width), :], _NT_DIMS, preferred_element_type=F32)
            bias = bias_tile(k0)[:, 0:width]
            out = []
            for half in range(2):
                sb = s[half * TQ:(half + 1) * TQ] + bias
                t = pm[half]
                for j in range(width // LANES):
                    t = jnp.maximum(t, sb[:, j * LANES:(j + 1) * LANES])
                out.append(t)
            return tuple(out)

        neg = jnp.full((TQ, LANES), -jnp.inf, F32)
        if exact_shift:
            pm = lax.fori_loop(0, S // TK, lambda ki, pm: tile_max(pl.multiple_of(ki * TK, TK), pm), (neg, neg))
        else:
            pm = tile_max(0, (neg, neg), width=DA_EST)
        m = jnp.max(jnp.concatenate(pm, axis=0), axis=-1, keepdims=True)
        mcol = jnp.where(lane2 == 0, -m, 0.0).astype(BF16)
        qx = jnp.concatenate([qa, mcol], axis=1)

        for ki in range(S // TK):
            k0 = ki * TK
            s = lax.dot_general(qx, kx_ref[k0:k0 + TK, :], _NT_DIMS, preferred_element_type=F32)
            bias = bias_tile(k0)
            for half in range(2):
                pr = jnp.exp2(s[half * TQ:(half + 1) * TQ] + bias).astype(BF16)
                ps_ref[half * TQ:(half + 1) * TQ, k0:k0 + TK] = pr

        acc = jnp.dot(ps_ref[...], vx_ref[...], preferred_element_type=F32)
        o1 = acc[0:TQ, 0:LANES] / acc[0:TQ, LANES:LANES + 1]
        o2 = acc[TQ:2 * TQ, 0:LANES] / acc[TQ:2 * TQ, LANES:LANES + 1]
        o = o1 - lam * o2
        ms = jnp.mean(o * o, axis=-1, keepdims=True)
        o = o * lax.rsqrt(ms + SUBLN_EPS) * sg_ref[...] * (1.0 - LAMBDA_INIT)
        o_ref[0, 0, pl.ds(q0, TQ), :] = o.astype(BF16)
        return jnp.where(jnp.isfinite(o), 0.0, 1.0)

    def fast_loop(qi, bad):
        for u in range(DA_GROUP):
            q0 = pl.multiple_of((qi * DA_GROUP + u) * TQ, TQ)
            bad = jnp.maximum(bad, q_tile(q0, exact_shift=False, slot=u))
        return bad

    bad = lax.fori_loop(0, S // (TQ * DA_GROUP), fast_loop, jnp.zeros((TQ, LANES), F32))

    @pl.when(jnp.max(bad) > 0.0)
    def _():
        def exact_loop(qi, _):
            q_tile(pl.multiple_of(qi * TQ, TQ), exact_shift=True)
            return 0
        lax.fori_loop(0, S // TQ, exact_loop, 0)


def _diff_attn(p, e_bias, lq1, lk1, lq2, lk2, subln_g):
    B, _, S, _ = p.shape
    H = N_DIFF_HEADS
    vec = lambda n: pl.BlockSpec((1, n), lambda b, h: (0, 0))
    return pl.pallas_call(
        _diff_attn_body,
        grid=(B, H),
        in_specs=[
            pl.BlockSpec((1, 1, S, LANES), lambda b, h: (b, h, 0, 0)),
            pl.BlockSpec((1, 1, S, LANES), lambda b, h: (b, H + h, 0, 0)),
            pl.BlockSpec((1, 1, S, LANES), lambda b, h: (b, 2 * H + h, 0, 0)),
            pl.BlockSpec((1, DA_TQ, DA_WTOT), lambda b, h: (h, 0, 0)),
            vec(HEAD_DIM), vec(HEAD_DIM), vec(HEAD_DIM), vec(HEAD_DIM), vec(2 * HEAD_DIM),
        ],
        out_specs=pl.BlockSpec((1, 1, S, LANES), lambda b, h: (b, h, 0, 0)),
        out_shape=jax.ShapeDtypeStruct((B, H, S, LANES), BF16),
        scratch_shapes=[
            pltpu.VMEM((S, 2 * LANES), BF16),
            pltpu.VMEM((S, 2 * LANES), BF16),
            pltpu.VMEM((DA_GROUP, 2 * DA_TQ, S), BF16),
        ],
        compiler_params=_cparams(("arbitrary", "arbitrary")),
        name="diff_attn",
    )(p, p, p, e_bias, lq1, lk1, lq2, lk2, subln_g)


def _dil_attn_body(q1_ref, k1_ref, v1_ref, q4_ref, k4_ref, v4_ref, q16_ref, k16_ref, v16_ref, m_ref,
                   o_ref, kpad_ref, vlo_ref, vhi_ref, oacc_ref, lse_ref):
    S = q1_ref.shape[2]
    lane = lax.broadcasted_iota(I32, (DL_BQ, LANES), 1)
    lo = lane < HEAD_DIM
    zpad = jnp.zeros((DL_HALF, LANES), BF16)

    def ones_fill(i, _):
        r0 = pl.multiple_of(i * 512, 512)
        lo_f = lax.broadcasted_iota(I32, (512, LANES), 1) < HEAD_DIM
        vlo_ref[pl.ds(r0, 512), LANES:2 * LANES] = jnp.where(lo_f, 1.0, 0.0).astype(BF16)
        vhi_ref[pl.ds(r0, 512), LANES:2 * LANES] = jnp.where(lo_f, 0.0, 1.0).astype(BF16)
        return 0

    lax.fori_loop(0, vlo_ref.shape[0] // 512, ones_fill, 0)

    def run_pattern(p, dil, get_q, get_k, get_v):
        L = S // dil
        nblk = L // DL_BQ
        LP = L + 2 * DL_HALF

        def fill(r, _):
            base = pl.multiple_of(r * LP, DL_HALF)
            for off in (0, DL_HALF + L):
                kpad_ref[pl.ds(base + off, DL_HALF), :] = zpad
                vlo_ref[pl.ds(base + off, DL_HALF), 0:LANES] = zpad
                vhi_ref[pl.ds(base + off, DL_HALF), 0:LANES] = zpad
            kpad_ref[pl.ds(base + DL_HALF, L), :] = get_k(r)[...]
            v = get_v(r)[...]
            lo_v = lax.broadcasted_iota(I32, v.shape, 1) < HEAD_DIM
            zero = jnp.zeros_like(v)
            vlo_ref[pl.ds(base + DL_HALF, L), 0:LANES] = jnp.where(lo_v, v, zero)
            vhi_ref[pl.ds(base + DL_HALF, L), 0:LANES] = jnp.where(lo_v, zero, v)
            return 0

        lax.fori_loop(0, dil, fill, 0)

        def group(gi, _):
            for u in range(DL_GROUP):
                n = gi * DL_GROUP + u
                r = n // nblk
                blk = n % nblk
                q0 = pl.multiple_of(blk * DL_BQ, DL_BQ)
                w0 = pl.multiple_of(r * LP + q0, DL_HALF)
                q = get_q(r)[pl.ds(q0, DL_BQ), :]
                kw = kpad_ref[pl.ds(w0, DL_WIN), :]
                vw = jnp.concatenate([vlo_ref[pl.ds(w0, DL_WIN), :], vhi_ref[pl.ds(w0, DL_WIN), :]], axis=0)
                edge = jnp.where(blk == 0, 0, jnp.where(blk == nblk - 1, 2, 1))
                zero = jnp.zeros_like(q)
                probs, maxes = [], []
                for hh in range(2):
                    qm = jnp.where(lo, q, zero) if hh == 0 else jnp.where(lo, zero, q)
                    s = lax.dot_general(qm, kw, _NT_DIMS, preferred_element_type=F32) + m_ref[p, hh, edge]
                    m = jnp.max(s, axis=-1, keepdims=True)
                    probs.append(jnp.exp2(s - m).astype(BF16))
                    maxes.append(m)
                both = jnp.dot(jnp.concatenate(probs, axis=1), vw, preferred_element_type=F32)
                l = both[:, LANES:2 * LANES]
                o = both[:, 0:LANES] / l
                lse = jnp.where(lo, maxes[0], maxes[1]) + jnp.log2(l)
                if dil == 1:
                    oacc_ref[p, pl.ds(q0, DL_BQ), :] = o
                    lse_ref[p, pl.ds(q0, DL_BQ), :] = lse
                else:
                    oacc_ref[p, pl.ds(q0 * dil + r, DL_BQ, stride=dil), :] = o
                    lse_ref[p, pl.ds(q0 * dil + r, DL_BQ, stride=dil), :] = lse
            return 0

        lax.fori_loop(0, dil * nblk // DL_GROUP, group, 0)

    run_pattern(0, 1, lambda r: q1_ref.at[0, 0], lambda r: k1_ref.at[0, 0], lambda r: v1_ref.at[0, 0])
    run_pattern(1, 4, lambda r: q4_ref.at[0, 0, r], lambda r: k4_ref.at[0, 0, r], lambda r: v4_ref.at[0, 0, r])
    run_pattern(2, 16, lambda r: q16_ref.at[0, 0, r], lambda r: k16_ref.at[0, 0, r], lambda r: v16_ref.at[0, 0, r])

    MIX = 256

    def mix(i, _):
        r0 = pl.multiple_of(i * MIX, MIX)
        ls = [lse_ref[p, pl.ds(r0, MIX), :] for p in range(3)]
        mx = jnp.maximum(jnp.maximum(ls[0], ls[1]), ls[2])
        ws = [jnp.exp2(l - mx) for l in ls]
        den = ws[0] + ws[1] + ws[2]
        num = ws[0] * oacc_ref[0, pl.ds(r0, MIX), :]
        num += ws[1] * oacc_ref[1, pl.ds(r0, MIX), :]
        num += ws[2] * oacc_ref[2, pl.ds(r0, MIX), :]
        o_ref[0, 0, pl.ds(r0, MIX), :] = (num / den).astype(BF16)
        return 0

    lax.fori_loop(0, S // MIX, mix, 0)


def _dil_attn(p, p4, p16, m_bias):
    B, _, S, _ = p.shape
    NP = N_DIL_PAIRS
    base = 3 * N_DIFF_HEADS
    nat = lambda o: pl.BlockSpec((1, 1, S, LANES), lambda b, h: (b, base + o * NP + h, 0, 0))
    res = lambda o, d: pl.BlockSpec((1, 1, d, S // d, LANES), lambda b, h: (b, o * NP + h, 0, 0, 0))
    return pl.pallas_call(
        _dil_attn_body,
        grid=(B, NP),
        in_specs=[nat(0), nat(1), nat(2), res(0, 4), res(1, 4), res(2, 4), res(0, 16), res(1, 16), res(2, 16),
                  pl.BlockSpec((3, 2, 3, DL_BQ, DL_WIN), lambda b, h: (0, h, 0, 0, 0))],
        out_specs=pl.BlockSpec((1, 1, S, LANES), lambda b, h: (b, h, 0, 0)),
        out_shape=jax.ShapeDtypeStruct((B, NP, S, LANES), BF16),
        scratch_shapes=[
            pltpu.VMEM((S + 2 * DL_HALF * DIL_PATTERNS[-1][1], LANES), BF16),
            pltpu.VMEM((S + 2 * DL_HALF * DIL_PATTERNS[-1][1], 2 * LANES), BF16),
            pltpu.VMEM((S + 2 * DL_HALF * DIL_PATTERNS[-1][1], 2 * LANES), BF16),
            pltpu.VMEM((3, S, LANES), F32),
            pltpu.VMEM((3, S, LANES), F32),
        ],
        compiler_params=_cparams(("arbitrary", "arbitrary")),
        name="dil_attn",
    )(p, p, p, p4, p4, p4, p16, p16, p16, m_bias)


OP_TM = 512


def _store_token_major(ref, val):
    rows = val.shape[0]
    for j in range(ROW_CHUNKS):
        ref[0, pl.ds(j, rows, stride=ROW_CHUNKS), :] = val[:, j * LANES:(j + 1) * LANES]


def _out_proj_body(oa_ref, od_ref, x_ref, w_ref, g_ref, wr_ref, x2_ref, h2_ref, aff_ref):
    pieces = [oa_ref[0, i] for i in range(N_DIFF_HEADS)] + [od_ref[0, i] for i in range(N_DIL_PAIRS)]
    a = jnp.concatenate(pieces, axis=-1)
    x2 = x_ref[0] + jnp.dot(a, w_ref[...], preferred_element_type=F32)
    ms = jnp.mean(x2 * x2, axis=-1, keepdims=True)
    h = x2 * lax.rsqrt(ms + NORM_EPS) * g_ref[...]
    h_hi = h.astype(BF16)
    h_lo = (h - h_hi.astype(F32)).astype(BF16)
    wr = wr_ref[...]
    w_hi = wr.astype(BF16)
    w_lo = (wr - w_hi.astype(F32)).astype(BF16)
    logits = (lax.dot_general(w_hi, h_hi, _NT_DIMS, preferred_element_type=F32)
              + lax.dot_general(w_hi, h_lo, _NT_DIMS, preferred_element_type=F32)
              + lax.dot_general(w_lo, h_hi, _NT_DIMS, preferred_element_type=F32))
    mx = jnp.max(logits, axis=0, keepdims=True)
    ex = jnp.exp(logits - mx)
    aff_ref[0] = ex / jnp.sum(ex, axis=0, keepdims=True)
    x2_ref[0] = x2
    _store_token_major(h2_ref, h)


def _out_proj(oa, od, x, w_bf16, g, wr_t):
    B, S, _ = x.shape
    nt = S // OP_TM
    attn = lambda n: pl.BlockSpec((1, n, OP_TM, LANES), lambda b, t: (b, 0, t, 0))
    tok_major = pl.BlockSpec((1, OP_TM * ROW_CHUNKS, LANES), lambda b, t: (b, t, 0))
    return pl.pallas_call(
        _out_proj_body,
        grid=(B, nt),
        in_specs=[
            attn(N_DIFF_HEADS), attn(N_DIL_PAIRS),
            pl.BlockSpec((1, OP_TM, D_MODEL), lambda b, t: (b, t, 0)),
            pl.BlockSpec((D_MODEL, D_MODEL), lambda b, t: (0, 0)),
            pl.BlockSpec((1, D_MODEL), lambda b, t: (0, 0)),
            pl.BlockSpec((N_EXPERTS, D_MODEL), lambda b, t: (0, 0)),
        ],
        out_specs=[pl.BlockSpec((1, OP_TM, D_MODEL), lambda b, t: (b, t, 0)), tok_major,
                   pl.BlockSpec((1, N_EXPERTS, OP_TM), lambda b, t: (b, 0, t))],
        out_shape=[
            jax.ShapeDtypeStruct((B, S, D_MODEL), F32),
            jax.ShapeDtypeStruct((B, S * ROW_CHUNKS, LANES), F32),
            jax.ShapeDtypeStruct((B, N_EXPERTS, S), F32),
        ],
        compiler_params=_cparams(("arbitrary", "arbitrary")),
        name="out_proj",
    )(oa, od, x, w_bf16, g, wr_t)


RT_GROUP = 4


def _route_body(aff_ref, idx_ref, xh_ref, xl_ref, st_ref, en_ref):
    E, S = aff_ref.shape[1], aff_ref.shape[2]
    C = idx_ref.shape[2]
    n_chunks = S // LANES
    a = aff_ref[0]
    bits = pltpu.bitcast(a, I32)
    tau = jnp.zeros((E, 1), I32)
    for bit in range(30, -1, -1):
        cand = tau | (1 << bit)
        cnt = jnp.sum(jnp.where(bits >= cand, 1.0, 0.0).astype(F32), axis=1, keepdims=True)
        tau = jnp.where(cnt >= C, cand, tau)
    gt = bits > tau
    eq = bits == tau
    need = C - jnp.sum(jnp.where(gt, 1.0, 0.0).astype(F32), axis=1, keepdims=True)

    tri = (lax.broadcasted_iota(I32, (LANES, LANES), 0) <= lax.broadcasted_iota(I32, (LANES, LANES), 1)).astype(BF16)

    def chunk_cumsum(mj, carry):
        return jnp.dot(mj.astype(BF16), tri, preferred_element_type=F32) + carry

    sel_chunks = []
    carry = jnp.zeros((E, 1), F32)
    for j in range(n_chunks):
        cs = slice(j * LANES, (j + 1) * LANES)
        eq_j = jnp.where(eq[:, cs], 1.0, 0.0).astype(F32)
        inc = chunk_cumsum(eq_j, carry)
        carry = inc[:, LANES - 1:LANES]
        take_tie = jnp.where(inc - eq_j < need, eq_j, 0.0)
        sel_chunks.append(jnp.where(gt[:, cs], 1.0, take_tie))

    lane_e = lax.broadcasted_iota(I32, (E, LANES), 1)
    st = jnp.full((E, LANES), float(C), F32)
    en = jnp.full((E, LANES), float(C), F32)
    carry = jnp.zeros((E, 1), F32)
    for j in range(n_chunks):
        sel_j = sel_chunks[j]
        inc = chunk_cumsum(sel_j, carry)
        posm = jnp.where(sel_j > 0.0, inc - sel_j, -1.0)
        hi = jnp.floor(posm * 0.5)
        xh_ref[pl.ds(j * E, E), :] = hi
        xl_ref[pl.ds(j * E, E), :] = posm - 2.0 * hi
        st = jnp.where(lane_e == j, carry, st)
        carry = inc[:, LANES - 1:LANES]
        en = jnp.where(lane_e == j, carry, en)
    pad_rows = (LANES - n_chunks) * E
    xh_ref[pl.ds(n_chunks * E, pad_rows), :] = jnp.zeros((pad_rows, LANES), F32)
    xl_ref[pl.ds(n_chunks * E, pad_rows), :] = jnp.zeros((pad_rows, LANES), F32)
    st_ref[...] = st
    en_ref[...] = en

    c_iota = lax.broadcasted_iota(I32, (C, LANES), 0).astype(F32)
    lane_row = jnp.where(lax.broadcasted_iota(I32, (SUBLANES, LANES), 0) == 0,
                         lax.broadcasted_iota(I32, (SUBLANES, LANES), 1), 0).astype(F32).astype(BF16)

    def expert_group(g, _):
        for u in range(RT_GROUP):
            e = g * RT_GROUP + u
            st_e = st_ref[pl.ds(e, 1), :]
            en_e = en_ref[pl.ds(e, 1), :]
            in_chunk = jnp.where(st_e <= c_iota, jnp.where(c_iota < en_e, 1.0, 0.0), 0.0).astype(BF16)
            xh = xh_ref[pl.ds(e, LANES, stride=E), :].astype(BF16)
            xl = xl_ref[pl.ds(e, LANES, stride=E), :].astype(BF16)
            hl = jnp.dot(in_chunk, jnp.concatenate([xh, xl], axis=1), preferred_element_type=F32)
            slot_of = 2.0 * hl[:, 0:LANES] + hl[:, LANES:2 * LANES]
            hit = jnp.where(slot_of == c_iota, 1.0, 0.0).astype(BF16)
            chunk_id = lax.dot_general(lane_row, in_chunk, _NT_DIMS, preferred_element_type=F32)
            lane_id = lax.dot_general(lane_row, hit, _NT_DIMS, preferred_element_type=F32)
            idx_ref[0, pl.ds(e, 1), :] = (chunk_id[0:1, :] * float(LANES) + lane_id[0:1, :] + 0.5).astype(I32)
        return 0

    lax.fori_loop(0, E // RT_GROUP, expert_group, 0)


def _route(aff):
    B, E, S = aff.shape
    C = EC_K * S // E
    return pl.pallas_call(
        _route_body,
        grid=(B,),
        in_specs=[pl.BlockSpec((1, E, S), lambda b: (b, 0, 0))],
        out_specs=pl.BlockSpec((1, E, C), lambda b: (b, 0, 0)),
        out_shape=jax.ShapeDtypeStruct((B, E, C), I32),
        scratch_shapes=[pltpu.VMEM((LANES * E, LANES), F32), pltpu.VMEM((LANES * E, LANES), F32),
                        pltpu.VMEM((E, LANES), F32), pltpu.VMEM((E, LANES), F32)],
        compiler_params=_cparams(("arbitrary",)),
        name="route",
    )(aff)


GA_UNROLL = 16
GA_EXPERTS = 1


def _gather_body(idx_ref, h_ref, xe_ref, g_ref):
    C = idx_ref.shape[2] // GA_EXPERTS
    pitch = g_ref.shape[0] // ROW_CHUNKS

    for ex in range(GA_EXPERTS):
        def rows(i, _):
            for u in range(GA_UNROLL):
                r = i * GA_UNROLL + u
                t = idx_ref[0, 0, ex * C + r]
                src = pl.multiple_of(t * ROW_CHUNKS, ROW_CHUNKS)
                g_ref[pl.ds(r, ROW_CHUNKS, stride=pitch), :] = h_ref[0, pl.ds(src, ROW_CHUNKS), :]
            return 0

        lax.fori_loop(0, C // GA_UNROLL, rows, 0)
        for j in range(ROW_CHUNKS):
            xe_ref[ex, :, j * LANES:(j + 1) * LANES] = g_ref[j * pitch:j * pitch + C, :].astype(BF16)


def _gather(idx, h2):
    B, E, C = idx.shape
    SR = h2.shape[1]
    G = GA_EXPERTS
    return pl.pallas_call(
        _gather_body,
        grid=(B, E // G),
        in_specs=[
            pl.BlockSpec((1, 1, G * C), lambda b, e: (b * (E // G) + e, 0, 0), memory_space=pltpu.SMEM),
            pl.BlockSpec((1, SR, LANES), lambda b, e: (b, 0, 0), pipeline_mode=pl.Buffered(1)),
        ],
        out_specs=pl.BlockSpec((G, C, D_MODEL), lambda b, e: (e, b, 0)),
        out_shape=jax.ShapeDtypeStruct((E, B * C, D_MODEL), BF16),
        scratch_shapes=[pltpu.VMEM(((C + SUBLANES) * ROW_CHUNKS, LANES), F32)],
        compiler_params=_cparams(("arbitrary", "arbitrary")),
        name="gather",
    )(idx.reshape(B * E // G, 1, G * C), h2)


FF_TM = 1024
FF_TF = 1408
FF_CAST_ROWS = 128


def _cast_weight(src_ref, dst_ref):
    rows = dst_ref.shape[0]

    def chunk(i, _):
        r0 = pl.multiple_of(i * FF_CAST_ROWS, FF_CAST_ROWS)
        dst_ref[pl.ds(r0, FF_CAST_ROWS), :] = src_ref[0, pl.ds(r0, FF_CAST_ROWS), :].astype(BF16)
        return 0

    lax.fori_loop(0, rows // FF_CAST_ROWS, chunk, 0)


def _ffn_up_body(x_ref, wg_ref, wu_ref, he_ref, wgb_ref, wub_ref):
    @pl.when(pl.program_id(2) == 0)
    def _():
        _cast_weight(wg_ref, wgb_ref)
        _cast_weight(wu_ref, wub_ref)

    x = x_ref[0]
    hg = jnp.dot(x, wgb_ref[...], preferred_element_type=F32)
    hu = jnp.dot(x, wub_ref[...], preferred_element_type=F32)
    he_ref[0] = (hg * jax.nn.sigmoid(hg) * hu).astype(BF16)


def _ffn_down_body(he_ref, wd_ref, y_ref, wdb_ref):
    @pl.when(pl.program_id(1) == 0)
    def _():
        _cast_weight(wd_ref, wdb_ref)

    _store_token_major(y_ref, jnp.dot(he_ref[0], wdb_ref[...], preferred_element_type=F32))


def _ffn(xe, wg, wu, wd):
    E, M, _ = xe.shape
    assert M % FF_TM == 0, (M, FF_TM)
    nm = M // FF_TM
    nf = D_FF // FF_TF
    he = pl.pallas_call(
        _ffn_up_body,
        grid=(E, nf, nm),
        in_specs=[
            pl.BlockSpec((1, FF_TM, D_MODEL), lambda e, f, m: (e, m, 0)),
            pl.BlockSpec((1, D_MODEL, FF_TF), lambda e, f, m: (e, 0, f)),
            pl.BlockSpec((1, D_MODEL, FF_TF), lambda e, f, m: (e, 0, f)),
        ],
        out_specs=pl.BlockSpec((1, FF_TM, FF_TF), lambda e, f, m: (e, m, f)),
        out_shape=jax.ShapeDtypeStruct((E, M, D_FF), BF16),
        scratch_shapes=[pltpu.VMEM((D_MODEL, FF_TF), BF16), pltpu.VMEM((D_MODEL, FF_TF), BF16)],
        compiler_params=_cparams(("arbitrary", "arbitrary", "arbitrary")),
        name="expert_up",
    )(xe, wg, wu)
    return pl.pallas_call(
        _ffn_down_body,
        grid=(E, nm),
        in_specs=[
            pl.BlockSpec((1, FF_TM, D_FF), lambda e, m: (e, m, 0)),
            pl.BlockSpec((1, D_FF, D_MODEL), lambda e, m: (e, 0, 0)),
        ],
        out_specs=pl.BlockSpec((1, FF_TM * ROW_CHUNKS, LANES), lambda e, m: (e, m, 0)),
        out_shape=jax.ShapeDtypeStruct((E, M * ROW_CHUNKS, LANES), F32),
        scratch_shapes=[pltpu.VMEM((D_FF, D_MODEL), BF16)],
        compiler_params=_cparams(("arbitrary", "arbitrary")),
        name="expert_down",
    )(he, wd)


CB_UNROLL = 8
CB_TM = 1024
CB_EXPERTS = 2


def _combine_body(idx_ref, aff_ref, y_ref, x2_ref, g_ref, o_ref, acc_ref):
    E = N_EXPERTS // CB_EXPERTS
    C = idx_ref.shape[2] // CB_EXPERTS
    S = aff_ref.shape[2] // CB_EXPERTS
    s = pl.program_id(1)

    @pl.when(s == 0)
    def _():
        def zero(i, _):
            r0 = pl.multiple_of(i * 1024, 1024)
            acc_ref[pl.ds(r0, 1024), :] = jnp.zeros((1024, LANES), F32)
            return 0
        lax.fori_loop(0, acc_ref.shape[0] // 1024, zero, 0)

    @pl.when(s < E)
    def _():
        for ex in range(CB_EXPERTS):
            def rows(i, _):
                dsts, sums = [], []
                for u in range(CB_UNROLL):
                    r = i * CB_UNROLL + u
                    t = idx_ref[0, 0, ex * C + r]
                    gate = aff_ref[0, 0, ex * S + t]
                    dst = pl.multiple_of(t * ROW_CHUNKS, ROW_CHUNKS)
                    src = pl.multiple_of(r * ROW_CHUNKS, ROW_CHUNKS)
                    dsts.append(dst)
                    sums.append(acc_ref[pl.ds(dst, ROW_CHUNKS), :] + gate * y_ref[ex, pl.ds(src, ROW_CHUNKS), :])
                for dst, v in zip(dsts, sums):
                    acc_ref[pl.ds(dst, ROW_CHUNKS), :] = v
                return 0
            lax.fori_loop(0, C // CB_UNROLL, rows, 0)

    @pl.when(s >= E)
    def _():
        r0 = pl.multiple_of((s - E) * (CB_TM * ROW_CHUNKS), CB_TM * ROW_CHUNKS)
        moe = jnp.concatenate([acc_ref[pl.ds(r0 + j, CB_TM, stride=ROW_CHUNKS), :] for j in range(ROW_CHUNKS)],
                              axis=1)
        v = x2_ref[0] + moe
        ms = jnp.mean(v * v, axis=-1, keepdims=True)
        o_ref[0] = v * lax.rsqrt(ms + NORM_EPS) * g_ref[...]


def _combine(idx, aff, ye, x2, g_f):
    B, E, C = idx.shape
    S = x2.shape[1]
    SR = S * ROW_CHUNKS
    nfin = S // CB_TM
    G = CB_EXPERTS
    ns = E // G
    return pl.pallas_call(
        _combine_body,
        grid=(B, ns + nfin),
        in_specs=[
            pl.BlockSpec((1, 1, G * C), lambda b, s: (b * ns + jnp.minimum(s, ns - 1), 0, 0), memory_space=pltpu.SMEM),
            pl.BlockSpec((1, 1, G * S), lambda b, s: (b * ns + jnp.minimum(s, ns - 1), 0, 0), memory_space=pltpu.SMEM),
            pl.BlockSpec((G, C * ROW_CHUNKS, LANES), lambda b, s: (jnp.minimum(s, ns - 1), b, 0)),
            pl.BlockSpec((1, CB_TM, D_MODEL), lambda b, s: (b, jnp.maximum(s - ns, 0), 0)),
            pl.BlockSpec((1, D_MODEL), lambda b, s: (0, 0)),
        ],
        out_specs=pl.BlockSpec((1, CB_TM, D_MODEL), lambda b, s: (b, jnp.maximum(s - ns, 0), 0)),
        out_shape=jax.ShapeDtypeStruct((B, S, D_MODEL), F32),
        scratch_shapes=[pltpu.VMEM((SR, LANES), F32)],
        compiler_params=_cparams(("arbitrary", "arbitrary")),
        name="combine",
    )(idx.reshape(B * ns, 1, G * C), aff.reshape(B * ns, 1, G * S), ye, x2, g_f)


def kernel(x, norm1_g, w_in, lam_q1, lam_k1, lam_q2, lam_k2, subln_g, w_out, rel_bias,
           norm2_g, w_router, w_gate, w_up, w_down, norm_f_g):
    B, S, _ = x.shape
    C = EC_K * S // N_EXPERTS
    e_bias = _diff_bias(rel_bias)
    m_bias = _dil_bias(rel_bias)
    p, p4, p16 = _in_proj(x, norm1_g, w_in[0].astype(BF16))
    oa = _diff_attn(p, e_bias, lam_q1, lam_k1, lam_q2, lam_k2, subln_g)
    od = _dil_attn(p, p4, p16, m_bias)
    x2, h2, aff = _out_proj(oa, od, x, w_out[0].astype(BF16), norm2_g, w_router[0].T)
    idx = _route(aff)
    xe = _gather(idx, h2)
    ye = _ffn(xe, w_gate[0], w_up[0], w_down[0])
    return _combine(idx, aff, ye, x2, norm_f_g.reshape(1, D_MODEL))
```
